```python
import math
import jax
import jax.numpy as jnp
from jax import lax
import numpy as np

D_MODEL = 1024
BATCH = 8
SEQ = 2048
DEPTH = 1
DEC_BATCH = 32
DEC_SEQ = 8
PAST_LEN = 8192
PAGE_SIZE = 128

S5_WIDTH = D_MODEL // 2
S5_GROUP = 16
S5_GROUPS = S5_WIDTH // S5_GROUP
S5_STATE = 64
HEAD_DIM = 64
NSA_HEADS = D_MODEL // 128
NSA_KV_HEADS = 2
Q_PER_KV = NSA_HEADS // NSA_KV_HEADS
NSA_WIDTH = NSA_HEADS * HEAD_DIM
KV_WIDTH = NSA_KV_HEADS * HEAD_DIM
CMP_LEN = 32
CMP_STRIDE = 16
CMP_HIDDEN = 2 * HEAD_DIM
SLC_BLOCK = 64
SLC_TOPK = 16
WINDOW = 512
WIN_QBLOCK = 128
SLC_QCHUNK = 32
D_FF = 4 * D_MODEL
ROPE_THETA = 10000.0
EPS = 1e-6
NEG_INF = -1e30
FORCED = 1e9
SCALE = HEAD_DIM ** -0.5
IN_SIZES = (S5_WIDTH, NSA_WIDTH) + (KV_WIDTH,) * 6 + (3 * NSA_HEADS, 2 * D_MODEL)
N_IN = sum(IN_SIZES)

kernel_name = 'hybrid_s5_nsa_decode_step'


def _rmsnorm(x, w):
    xf = x.astype(jnp.float32)
    r = lax.rsqrt(jnp.mean(xf * xf, axis=-1, keepdims=True) + EPS)
    return (xf * r * w.astype(jnp.float32)).astype(x.dtype)


def _rope(x, pos):
    half = HEAD_DIM // 2
    inv = ROPE_THETA ** (-jnp.arange(half, dtype=jnp.float32) / half)
    ang = pos.astype(jnp.float32)[:, None] * inv[None, :]
    cos, sin = jnp.cos(ang)[:, None, :], jnp.sin(ang)[:, None, :]
    x1 = x[..., :half].astype(jnp.float32)
    x2 = x[..., half:].astype(jnp.float32)
    return jnp.concatenate([x1 * cos - x2 * sin, x2 * cos + x1 * sin], axis=-1).astype(x.dtype)


def _masked_softmax(s, mask):
    s = jnp.where(mask, s.astype(jnp.float32), NEG_INF)
    m = jnp.max(s, axis=-1, keepdims=True)
    p = jnp.where(mask, jnp.exp(s - m), 0.0)
    return p / jnp.maximum(jnp.sum(p, axis=-1, keepdims=True), 1e-30)


def _project(x, pos, p):
    B, T, _ = x.shape
    z = _rmsnorm(x, p['norm_mix_w']) @ p['w_in']
    offs = [int(o) for o in np.cumsum(IN_SIZES)[:-1]]
    u, q, kc, vc, ks, vs, kw, vw, g_nsa, g_mix = jnp.split(z, offs, axis=-1)
    kv = lambda t: t.reshape(B, T, NSA_KV_HEADS, HEAD_DIM)
    q = _rope(_rmsnorm(q.reshape(B, T, NSA_HEADS, HEAD_DIM), p['q_norm_w']), pos)
    kc = _rope(_rmsnorm(kv(kc), p['k_norm_w'][0]), pos)
    ks = _rope(_rmsnorm(kv(ks), p['k_norm_w'][1]), pos)
    kw = _rope(_rmsnorm(kv(kw), p['k_norm_w'][2]), pos)
    return u, q, kc, kv(vc), ks, kv(vs), kw, kv(vw), g_nsa, g_mix


def _complex_affine_combine(earlier, later):
    a1r, a1i, b1r, b1i = earlier
    a2r, a2i, b2r, b2i = later
    ar = a1r * a2r - a1i * a2i
    ai = a1r * a2i + a1i * a2r
    br = a2r * b1r - a2i * b1i + b2r
    bi = a2r * b1i + a2i * b1r + b2i
    return ar, ai, br, bi


def _s5(u, state_re, state_im, p):
    B, T, _ = u.shape
    f32 = jnp.float32
    uf = u.astype(f32)
    ug = uf.reshape(B, T, S5_GROUPS, S5_GROUP)
    lr, li = p['s5_lam_re'].astype(f32), p['s5_lam_im'].astype(f32)
    dt = jnp.exp(p['s5_log_dt'].astype(f32))[:, None]
    mag = jnp.exp(lr * dt)
    ab_re, ab_im = mag * jnp.cos(li * dt), mag * jnp.sin(li * dt)
    den = lr * lr + li * li
    f_re = ((ab_re - 1.0) * lr + ab_im * li) / den
    f_im = (ab_im * lr - (ab_re - 1.0) * li) / den
    br, bi = p['s5_b_re'].astype(f32), p['s5_b_im'].astype(f32)
    bb_re = f_re[..., None] * br - f_im[..., None] * bi
    bb_im = f_re[..., None] * bi + f_im[..., None] * br
    bu_re = jnp.einsum('btgh,gph->btgp', ug, bb_re)
    bu_im = jnp.einsum('btgh,gph->btgp', ug, bb_im)
    if state_re is not None:
        sr, si = state_re.astype(f32), state_im.astype(f32)
        bu_re = bu_re.at[:, 0].add(ab_re * sr - ab_im * si)
        bu_im = bu_im.at[:, 0].add(ab_re * si + ab_im * sr)
    a_re = jnp.broadcast_to(ab_re, bu_re.shape)
    a_im = jnp.broadcast_to(ab_im, bu_re.shape)
    _, _, x_re, x_im = lax.associative_scan(_complex_affine_combine, (a_re, a_im, bu_re, bu_im), axis=1)
    y = (jnp.einsum('btgp,ghp->btgh', x_re, p['s5_c_re'].astype(f32))
         - jnp.einsum('btgp,ghp->btgh', x_im, p['s5_c_im'].astype(f32)))
    y = y.reshape(B, T, S5_WIDTH) + p['s5_d'].astype(f32) * uf
    return y.astype(u.dtype), x_re[:, -1], x_im[:, -1]


def _compress(rows, pe, w1, w2):
    L = rows.shape[1]
    n_cmp = (L - CMP_LEN) // CMP_STRIDE + 1
    idx = np.arange(n_cmp)[:, None] * CMP_STRIDE + np.arange(CMP_LEN)[None, :]
    blk = rows[:, idx] + pe[None, None, :, None, :]
    hid = jax.nn.gelu(jnp.einsum('bnlhd,ldf->bnhf', blk, w1))
    return hid @ w2


def _cmp_attention(q, k_rows, v_rows, q_pos, p):
    B, T = q.shape[:2]
    kc = _compress(k_rows, p['cmp_pe_k'], p['cmp_wk1'], p['cmp_wk2'])
    vc = _compress(v_rows, p['cmp_pe_v'], p['cmp_wv1'], p['cmp_wv2'])
    n_cmp = kc.shape[1]
    qg = q.reshape(B, T, NSA_KV_HEADS, Q_PER_KV, HEAD_DIM)
    s = jnp.einsum('bthgd,bnhd->bthgn', qg, kc) * SCALE
    blk_end = jnp.arange(n_cmp) * CMP_STRIDE + CMP_LEN - 1
    mask = (blk_end[None, :] <= q_pos[:, None])[None, :, None, None, :]
    prob = _masked_softmax(s, mask)
    o = jnp.einsum('bthgn,bnhd->bthgd', prob.astype(vc.dtype), vc)
    return o.reshape(B, T, NSA_HEADS, HEAD_DIM), jnp.sum(prob, axis=3)


def _slc_attention(q, k_rows, v_rows, q_pos, p_grp):
    B, T = q.shape[:2]
    L = k_rows.shape[1]
    n_slc = -(-L // SLC_BLOCK)
    n_cmp = p_grp.shape[-1]
    topk = min(SLC_TOPK, n_slc)
    cs = np.arange(n_cmp) * CMP_STRIDE
    ss = np.arange(n_slc) * SLC_BLOCK
    ov = np.clip(np.minimum(cs[:, None] + CMP_LEN, ss[None, :] + SLC_BLOCK)
                 - np.maximum(cs[:, None], ss[None, :]), 0, None) / CMP_LEN
    imp = jnp.einsum('bthn,nj->bthj', p_grp, jnp.asarray(ov, jnp.float32))
    q_blk = q_pos // SLC_BLOCK
    j = jnp.arange(n_slc)
    valid = (j[None, :] <= q_blk[:, None])[None, :, None, :]
    forced = ((j[None, :] == 0) | (j[None, :] == q_blk[:, None])
              | (j[None, :] == q_blk[:, None] - 1))[None, :, None, :]
    score = jnp.where(forced, FORCED, jnp.where(valid, imp, NEG_INF))
    _, idx = lax.top_k(score, topk)
    sel_valid = jnp.take_along_axis(jnp.broadcast_to(valid, score.shape), idx, axis=-1)
    pad = n_slc * SLC_BLOCK - L
    to_blocks = lambda t: jnp.pad(t, ((0, 0), (0, pad), (0, 0), (0, 0))).reshape(
        B, n_slc, SLC_BLOCK, NSA_KV_HEADS, HEAD_DIM).transpose(0, 3, 1, 2, 4)
    kb, vb = to_blocks(k_rows), to_blocks(v_rows)
    qg = q.reshape(B, T, NSA_KV_HEADS, Q_PER_KV, HEAD_DIM)
    C = SLC_QCHUNK if T % SLC_QCHUNK == 0 else T
    nc = T // C
    bi = jnp.arange(B)[:, None, None, None]
    hi = jnp.arange(NSA_KV_HEADS)[None, None, :, None]

    def chunk(args):
        qc, ic, vc_, pc = args
        kg = kb[bi, hi, ic]
        vg = vb[bi, hi, ic]
        s = jnp.einsum('bchgd,bchkrd->bchgkr', qc, kg) * SCALE
        kpos = ic[..., None] * SLC_BLOCK + jnp.arange(SLC_BLOCK)
        mask = vc_[..., None] & (kpos <= pc[None, :, None, None, None])
        prob = _masked_softmax(s.reshape(B, C, NSA_KV_HEADS, Q_PER_KV, -1),
                               mask.reshape(B, C, NSA_KV_HEADS, 1, -1)).reshape(s.shape)
        return jnp.einsum('bchgkr,bchkrd->bchgd', prob.astype(vg.dtype), vg)

    to_chunks = lambda t: t.reshape((B, nc, C) + t.shape[2:]).swapaxes(0, 1)
    o = lax.map(chunk, (to_chunks(qg), to_chunks(idx), to_chunks(sel_valid), q_pos.reshape(nc, C)))
    return o.swapaxes(0, 1).reshape(B, T, NSA_HEADS, HEAD_DIM)


def _win_prompt(q, k, v):
    B, T = q.shape[:2]
    nb = T // WIN_QBLOCK
    span = WINDOW + WIN_QBLOCK
    padw = ((0, 0), (WINDOW, 0), (0, 0), (0, 0))
    kp, vp = jnp.pad(k, padw), jnp.pad(v, padw)
    idx = np.arange(nb)[:, None] * WIN_QBLOCK + np.arange(span)[None, :]
    kband, vband = kp[:, idx], vp[:, idx]
    kpos = idx - WINDOW
    qpos = np.arange(T).reshape(nb, WIN_QBLOCK)
    rel = qpos[:, :, None] - kpos[:, None, :]
    mask = (rel >= 0) & (rel < WINDOW) & (kpos[:, None, :] >= 0)
    qg = q.reshape(B, nb, WIN_QBLOCK, NSA_KV_HEADS, Q_PER_KV, HEAD_DIM)
    s = jnp.einsum('bnqhgd,bnkhd->bnqhgk', qg, kband) * SCALE
    prob = _masked_softmax(s, mask[None, :, :, None, None, :])
    o = jnp.einsum('bnqhgk,bnkhd->bnqhgd', prob.astype(vband.dtype), vband)
    return o.reshape(B, T, NSA_HEADS, HEAD_DIM)


def _win_sample(q, k_all, v_all, q_pos, k_pos):
    B, T = q.shape[:2]
    qg = q.reshape(B, T, NSA_KV_HEADS, Q_PER_KV, HEAD_DIM)
    s = jnp.einsum('bthgd,bkhd->bthgk', qg, k_all) * SCALE
    rel = q_pos[:, None] - k_pos[None, :]
    mask = ((rel >= 0) & (rel < WINDOW))[None, :, None, None, :]
    prob = _masked_softmax(s, mask)
    o = jnp.einsum('bthgk,bkhd->bthgd', prob.astype(v_all.dtype), v_all)
    return o.reshape(B, T, NSA_HEADS, HEAD_DIM)


def _layer(x, pos, p, past):
    B, T, _ = x.shape
    u, q, kc, vc, ks, vs, kw, vw, g_nsa, g_mix = _project(x, pos, p)
    if past is None:
        s5_y, s5_re, s5_im = _s5(u, None, None, p)
        kc_all, vc_all, ks_all, vs_all = kc, vc, ks, vs
        o_win = _win_prompt(q, kw, vw)
        keep = min(WINDOW, T)
        win_k, win_v = kw[:, T - keep:], vw[:, T - keep:]
    else:
        s5_y, s5_re, s5_im = _s5(u, past['s5_re'], past['s5_im'], p)
        kc_all = jnp.concatenate([past['k_cmp'], kc], axis=1)
        vc_all = jnp.concatenate([past['v_cmp'], vc], axis=1)
        ks_all = jnp.concatenate([past['k_slc'], ks], axis=1)
        vs_all = jnp.concatenate([past['v_slc'], vs], axis=1)
        wb = past['k_win'].shape[1]
        kw_all = jnp.concatenate([past['k_win'], kw], axis=1)
        vw_all = jnp.concatenate([past['v_win'], vw], axis=1)
        k_pos = pos[0] - wb + jnp.arange(wb + T)
        o_win = _win_sample(q, kw_all, vw_all, pos, k_pos)
        win_k, win_v = kw_all[:, T:], vw_all[:, T:]
    o_cmp, p_grp = _cmp_attention(q, kc_all, vc_all, pos, p)
    o_slc = _slc_attention(q, ks_all, vs_all, pos, p_grp)
    g = jax.nn.sigmoid(g_nsa.reshape(B, T, NSA_HEADS, 3))
    o = g[..., 0:1] * o_cmp + g[..., 1:2] * o_slc + g[..., 2:3] * o_win
    nsa_out = o.reshape(B, T, NSA_WIDTH) @ p['nsa_w_up']
    yg = jax.nn.gelu(s5_y)
    s5_out = (yg * jax.nn.sigmoid(yg @ p['s5_w_glu'])) @ p['s5_w_up']
    gate = jax.nn.sigmoid(g_mix)
    merged = gate[..., :D_MODEL] * s5_out + gate[..., D_MODEL:] * nsa_out
    x = x + merged @ p['w_out']
    h = _rmsnorm(x, p['norm_ffn_w'])
    x = x + jnp.square(jax.nn.relu(h @ p['w_ffn1'])) @ p['w_ffn2']
    return x, (kc, vc, ks, vs, win_k, win_v, s5_re, s5_im)


def setup_inputs(seed: int = 0) -> dict:
    key = jax.random.key(seed)
    keys = iter(jax.random.split(key, 48))

    def nrm(shape, scale=1.0):
        return scale * jax.random.normal(next(keys), shape, jnp.float32)

    n_pages = PAST_LEN // PAGE_SIZE
    n_used = DEC_BATCH * n_pages
    n_pool = n_used + max(1, n_used // 4)
    win_buf = min(WINDOW, PAST_LEN)
    kv_page = (DEPTH, n_pool, PAGE_SIZE, NSA_KV_HEADS, HEAD_DIM)
    kv_win = (DEPTH, DEC_BATCH, win_buf, NSA_KV_HEADS, HEAD_DIM)
    s5_st = (DEPTH, DEC_BATCH, S5_GROUPS, S5_STATE)
    lam = (DEPTH, S5_GROUPS, S5_STATE)
    n_idx = jnp.arange(S5_STATE, dtype=jnp.float32)
    inputs = {}
    inputs['x_prompt'] = nrm((BATCH, SEQ, D_MODEL))
    inputs['x_sample'] = nrm((DEC_BATCH, DEC_SEQ, D_MODEL))
    inputs['cache_k_cmp'] = nrm(kv_page)
    inputs['cache_v_cmp'] = nrm(kv_page)
    inputs['cache_k_slc'] = nrm(kv_page)
    inputs['cache_v_slc'] = nrm(kv_page)
    inputs['state_k_win'] = nrm(kv_win)
    inputs['state_v_win'] = nrm(kv_win)
    inputs['state_s5_re'] = nrm(s5_st, 0.3)
    inputs['state_s5_im'] = nrm(s5_st, 0.3)
    inputs['page_table'] = jax.random.permutation(next(keys), n_pool)[:n_used].reshape(
        DEC_BATCH, n_pages).astype(jnp.int32)
    inputs['norm_mix_w'] = 1.0 + nrm((DEPTH, D_MODEL), 0.02)
    inputs['w_in'] = nrm((DEPTH, D_MODEL, N_IN), D_MODEL ** -0.5)
    inputs['s5_lam_re'] = -0.5 + nrm(lam, 0.02)
    inputs['s5_lam_im'] = math.pi * n_idx + nrm(lam, 0.02)
    inputs['s5_log_dt'] = jax.random.uniform(next(keys), (DEPTH, S5_GROUPS), jnp.float32,
                                             math.log(1e-3), math.log(1e-1))
    inputs['s5_b_re'] = nrm((DEPTH, S5_GROUPS, S5_STATE, S5_GROUP), (2 * S5_GROUP) ** -0.5)
    inputs['s5_b_im'] = nrm((DEPTH, S5_GROUPS, S5_STATE, S5_GROUP), (2 * S5_GROUP) ** -0.5)
    inputs['s5_c_re'] = nrm((DEPTH, S5_GROUPS, S5_GROUP, S5_STATE), S5_STATE ** -0.5)
    inputs['s5_c_im'] = nrm((DEPTH, S5_GROUPS, S5_GROUP, S5_STATE), S5_STATE ** -0.5)
    inputs['s5_d'] = nrm((DEPTH, S5_WIDTH))
    inputs['s5_w_glu'] = nrm((DEPTH, S5_WIDTH, S5_WIDTH), S5_WIDTH ** -0.5)
    inputs['s5_w_up'] = nrm((DEPTH, S5_WIDTH, D_MODEL), S5_WIDTH ** -0.5)
    inputs['q_norm_w'] = 1.0 + nrm((DEPTH, HEAD_DIM), 0.02)
    inputs['k_norm_w'] = 1.0 + nrm((DEPTH, 3, HEAD_DIM), 0.02)
    inputs['cmp_pe_k'] = nrm((DEPTH, CMP_LEN, HEAD_DIM), 0.1)
    inputs['cmp_wk1'] = nrm((DEPTH, CMP_LEN, HEAD_DIM, CMP_HIDDEN), (CMP_LEN * HEAD_DIM) ** -0.5)
    inputs['cmp_wk2'] = nrm((DEPTH, CMP_HIDDEN, HEAD_DIM), CMP_HIDDEN ** -0.5)
    inputs['cmp_pe_v'] = nrm((DEPTH, CMP_LEN, HEAD_DIM), 0.1)
    inputs['cmp_wv1'] = nrm((DEPTH, CMP_LEN, HEAD_DIM, CMP_HIDDEN), (CMP_LEN * HEAD_DIM) ** -0.5)
    inputs['cmp_wv2'] = nrm((DEPTH, CMP_HIDDEN, HEAD_DIM), CMP_HIDDEN ** -0.5)
    inputs['nsa_w_up'] = nrm((DEPTH, NSA_WIDTH, D_MODEL), NSA_WIDTH ** -0.5)
    inputs['w_out'] = nrm((DEPTH, D_MODEL, D_MODEL), D_MODEL ** -0.5)
    inputs['norm_ffn_w'] = 1.0 + nrm((DEPTH, D_MODEL), 0.02)
    inputs['w_ffn1'] = nrm((DEPTH, D_MODEL, D_FF), D_MODEL ** -0.5)
    inputs['w_ffn2'] = nrm((DEPTH, D_FF, D_MODEL), D_FF ** -0.5)
    return inputs


def reference(x_prompt, x_sample, cache_k_cmp, cache_v_cmp, cache_k_slc, cache_v_slc,
              state_k_win, state_v_win, state_s5_re, state_s5_im, page_table,
              norm_mix_w, w_in, s5_lam_re, s5_lam_im, s5_log_dt, s5_b_re, s5_b_im,
              s5_c_re, s5_c_im, s5_d, s5_w_glu, s5_w_up, q_norm_w, k_norm_w,
              cmp_pe_k, cmp_wk1, cmp_wk2, cmp_pe_v, cmp_wv1, cmp_wv2, nsa_w_up,
              w_out, norm_ffn_w, w_ffn1, w_ffn2):
    n_seq, n_pages = page_table.shape
    past_len = n_pages * PAGE_SIZE
    pos_p = jnp.arange(x_prompt.shape[1], dtype=jnp.int32)
    pos_s = past_len + jnp.arange(x_sample.shape[1], dtype=jnp.int32)
    y_prompt, y_sample = x_prompt, x_sample
    outs_p, outs_s = [], []
    for l in range(DEPTH):
        prm = dict(norm_mix_w=norm_mix_w[l], w_in=w_in[l], s5_lam_re=s5_lam_re[l],
                   s5_lam_im=s5_lam_im[l], s5_log_dt=s5_log_dt[l], s5_b_re=s5_b_re[l],
                   s5_b_im=s5_b_im[l], s5_c_re=s5_c_re[l], s5_c_im=s5_c_im[l], s5_d=s5_d[l],
                   s5_w_glu=s5_w_glu[l], s5_w_up=s5_w_up[l], q_norm_w=q_norm_w[l],
                   k_norm_w=k_norm_w[l], cmp_pe_k=cmp_pe_k[l], cmp_wk1=cmp_wk1[l],
                   cmp_wk2=cmp_wk2[l], cmp_pe_v=cmp_pe_v[l], cmp_wv1=cmp_wv1[l],
                   cmp_wv2=cmp_wv2[l], nsa_w_up=nsa_w_up[l], w_out=w_out[l],
                   norm_ffn_w=norm_ffn_w[l], w_ffn1=w_ffn1[l], w_ffn2=w_ffn2[l])
        gather = lambda c: c[l][page_table].reshape(n_seq, past_len, NSA_KV_HEADS, HEAD_DIM)
        past = dict(k_cmp=gather(cache_k_cmp), v_cmp=gather(cache_v_cmp),
                    k_slc=gather(cache_k_slc), v_slc=gather(cache_v_slc),
                    k_win=state_k_win[l], v_win=state_v_win[l],
                    s5_re=state_s5_re[l], s5_im=state_s5_im[l])
        y_prompt, st_p = _layer(y_prompt, pos_p, prm, None)
        y_sample, st_s = _layer(y_sample, pos_s, prm, past)
        outs_p.append(st_p)
        outs_s.append(st_s)
    k_cmp_p, v_cmp_p, k_slc_p, v_slc_p, k_win_p, v_win_p, s5_re_p, s5_im_p = [
        jnp.stack([o[i] for o in outs_p]) for i in range(8)]
    k_cmp_s, v_cmp_s, k_slc_s, v_slc_s, k_win_s, v_win_s, s5_re_s, s5_im_s = [
        jnp.stack([o[i] for o in outs_s]) for i in range(8)]
    return (y_prompt, y_sample, k_cmp_p, k_cmp_s, v_cmp_p, v_cmp_s, k_slc_p, k_slc_s,
            v_slc_p, v_slc_s, k_win_p, k_win_s, v_win_p, v_win_s,
            s5_re_p, s5_re_s, s5_im_p, s5_im_s)
```

```python
import functools

import numpy as np
import jax
import jax.numpy as jnp
from jax import lax
from jax.experimental import pallas as pl
from jax.experimental.pallas import tpu as pltpu

F32 = jnp.float32
BF16 = jnp.bfloat16

D_MODEL = 1024
PAGE_SIZE = 128
S5_WIDTH = 512
S5_GROUPS = 32
S5_STATE = 64
S5_LANES = S5_GROUPS * S5_STATE
HEAD_DIM = 64
NSA_HEADS = 8
NSA_KV_HEADS = 2
Q_PER_KV = 4
NSA_WIDTH = 512
KV_WIDTH = 128
CMP_LEN = 32
CMP_STRIDE = 16
CMP_HIDDEN = 128
SLC_BLOCK = 64
SLC_TOPK = 16
WINDOW = 512
ROPE_THETA = 10000.0
EPS = 1e-6
NEG_INF = -1e30
FORCED = 1e9
SCALE = HEAD_DIM ** -0.5
N_GATE = 3 * NSA_HEADS
OFF_GMIX = 2 * 512 + 6 * KV_WIDTH + N_GATE
W_A_COLS = 1920

LANES = 128
SUBLANES = 8
VMEM_LIMIT = 56 * 1024 * 1024


def _params(sem):
    return pltpu.CompilerParams(dimension_semantics=sem, vmem_limit_bytes=VMEM_LIMIT)


def _const_spec(shape):
    nd = len(shape)
    return pl.BlockSpec(shape, lambda *_: (0,) * nd)


def _dot(a, b):
    return jnp.dot(a, b, preferred_element_type=F32)


def _dot_nt(a, b):
    return lax.dot_general(a, b, (((1,), (1,)), ((), ())), preferred_element_type=F32)


def _rms(x, w):
    r = lax.rsqrt(jnp.mean(x * x, axis=-1, keepdims=True) + EPS)
    return x * r * w


def _lane_iota(shape):
    return lax.broadcasted_iota(jnp.int32, shape, len(shape) - 1)


def _row_iota(shape):
    return lax.broadcasted_iota(jnp.int32, shape, len(shape) - 2)


def _split3(x):
    hi = x.astype(BF16)
    r1 = x - hi.astype(F32)
    mid = r1.astype(BF16)
    lo = (r1 - mid.astype(F32)).astype(BF16)
    return hi, mid, lo


def _inproj_body(x_ref, nw_ref, w_ref, qkw_ref, seg_ref, cos_ref, sin_ref,
                 u_ref, q_ref, kc_ref, ks_ref, kw_ref, vc_ref, vs_ref, vw_ref, g_ref):
    xn = _rms(x_ref[...], nw_ref[...]).astype(BF16)
    z = _dot(xn, w_ref[...])
    u_ref[...] = z[:, :512]
    cos = cos_ref[...]
    sin = sin_ref[...]
    seg = seg_ref[...]
    first_half = (_lane_iota((1, LANES)) % HEAD_DIM) < (HEAD_DIM // 2)

    def norm_rope(t, w):
        ss = _dot((t * t).astype(BF16), seg)
        t = t * lax.rsqrt(ss * (1.0 / HEAD_DIM) + EPS) * w
        rot = jnp.where(first_half, pltpu.roll(t, LANES - HEAD_DIM // 2, 1), pltpu.roll(t, HEAD_DIM // 2, 1))
        return t * cos + rot * sin

    for j in range(4):
        q_ref[:, j * LANES:(j + 1) * LANES] = norm_rope(
            z[:, 512 + j * LANES:512 + (j + 1) * LANES], qkw_ref[:, j * LANES:(j + 1) * LANES])
    for j, r in enumerate((kc_ref, ks_ref, kw_ref)):
        r[...] = norm_rope(z[:, 1024 + j * LANES:1024 + (j + 1) * LANES],
                           qkw_ref[:, 512 + j * LANES:512 + (j + 1) * LANES])
    vc_ref[...] = z[:, 1408:1536]
    vs_ref[...] = z[:, 1536:1664]
    vw_ref[...] = z[:, 1664:1792]
    g_ref[...] = z[:, 1792:1920]


def _inproj(x, cos_t, sin_t, nw, w_a, qkw, seg, tt):
    B, T, _ = x.shape
    row = lambda w: pl.BlockSpec((None, tt, w), lambda b, i: (b, i, 0))
    tab = pl.BlockSpec((tt, LANES), lambda b, i: (i, 0))
    kv = jax.ShapeDtypeStruct((B, T, KV_WIDTH), F32)
    out_shape = [jax.ShapeDtypeStruct((B, T, 512), F32), jax.ShapeDtypeStruct((B, T, 512), F32)] + [kv] * 7
    return pl.pallas_call(
        _inproj_body,
        grid=(B, T // tt),
        in_specs=[row(D_MODEL), _const_spec((1, D_MODEL)), _const_spec((D_MODEL, W_A_COLS)),
                  _const_spec((1, 896)), _const_spec((LANES, LANES)), tab, tab],
        out_specs=[row(512), row(512)] + [row(KV_WIDTH)] * 7,
        out_shape=out_shape,
        compiler_params=_params(("parallel", "parallel")),
        name="inproj",
    )(x, nw, w_a, qkw, seg, cos_t, sin_t)


def _s5_body(u_ref, sre0_ref, sim0_ref, lre_ref, lim_ref, ldt_ref, bre_ref, bim_ref, cre_ref, cim_ref,
             d_ref, wglu_ref, out_ref, sre_ref, sim_ref,
             bbar_ref, a_ref, st_ref, xr_ref, xi_ref, *, tc):
    c = pl.program_id(1)
    rows = tc * SUBLANES

    @pl.when(c == 0)
    def _():
        lr, li = lre_ref[...], lim_ref[...]
        dt = jnp.exp(ldt_ref[...])
        mag = jnp.exp(lr * dt)
        ab_re, ab_im = mag * jnp.cos(li * dt), mag * jnp.sin(li * dt)
        den = lr * lr + li * li
        f_re = ((ab_re - 1.0) * lr + ab_im * li) / den
        f_im = (ab_im * lr - (ab_re - 1.0) * li) / den
        bre, bim = bre_ref[...], bim_ref[...]
        bbar_ref[:, :S5_LANES] = (f_re * bre - f_im * bim).astype(BF16)
        bbar_ref[:, S5_LANES:] = (f_re * bim + f_im * bre).astype(BF16)
        a_ref[0] = jnp.broadcast_to(ab_re, (SUBLANES, S5_LANES))
        a_ref[1] = jnp.broadcast_to(ab_im, (SUBLANES, S5_LANES))
        st_ref[0] = sre0_ref[...]
        st_ref[1] = sim0_ref[...]

    u = u_ref[...].reshape(rows, S5_WIDTH)
    ub = u.astype(BF16)
    xr_ref[...] = _dot(ub, bbar_ref[:, :S5_LANES])
    xi_ref[...] = _dot(ub, bbar_ref[:, S5_LANES:])

    slab = 512
    for lc in range(S5_LANES // slab):
        sl = slice(lc * slab, (lc + 1) * slab)
        ar, ai = a_ref[0, :, sl], a_ref[1, :, sl]

        def step(t, carry, sl=sl, ar=ar, ai=ai):
            sr, si = carry
            r0 = pl.multiple_of(t * SUBLANES, SUBLANES)
            nr = ar * sr - ai * si + xr_ref[pl.ds(r0, SUBLANES), sl]
            ni = ar * si + ai * sr + xi_ref[pl.ds(r0, SUBLANES), sl]
            xr_ref[pl.ds(r0, SUBLANES), sl] = nr
            xi_ref[pl.ds(r0, SUBLANES), sl] = ni
            return nr, ni

        sr, si = lax.fori_loop(0, tc, step, (st_ref[0, :, sl], st_ref[1, :, sl]), unroll=min(tc, 8))
        st_ref[0, :, sl] = sr
        st_ref[1, :, sl] = si

    y = (_dot(xr_ref[...].astype(BF16), cre_ref[...]) - _dot(xi_ref[...].astype(BF16), cim_ref[...])
         + d_ref[...] * u)
    yg = jax.nn.gelu(y)
    glu = yg * jax.nn.sigmoid(_dot(yg.astype(BF16), wglu_ref[...]))
    out_ref[...] = glu.reshape(tc, SUBLANES, S5_WIDTH)

    @pl.when(c == pl.num_programs(1) - 1)
    def _():
        sre_ref[...] = st_ref[0]
        sim_ref[...] = st_ref[1]


def _s5(u_tm, sre0, sim0, lre, lim, ldt, bre, bim, cre, cim, d, wglu, tc):
    T, B, _ = u_tm.shape
    st_spec = pl.BlockSpec((SUBLANES, S5_LANES), lambda b, c: (b, 0))
    blk = pl.BlockSpec((tc, SUBLANES, S5_WIDTH), lambda b, c: (c, b, 0))
    st = jax.ShapeDtypeStruct((B, S5_LANES), F32)
    return pl.pallas_call(
        functools.partial(_s5_body, tc=tc),
        grid=(B // SUBLANES, T // tc),
        in_specs=[blk, st_spec, st_spec,
                  _const_spec((1, S5_LANES)), _const_spec((1, S5_LANES)), _const_spec((1, S5_LANES)),
                  _const_spec((S5_WIDTH, S5_LANES)), _const_spec((S5_WIDTH, S5_LANES)),
                  _const_spec((S5_LANES, S5_WIDTH)), _const_spec((S5_LANES, S5_WIDTH)),
                  _const_spec((1, S5_WIDTH)), _const_spec((S5_WIDTH, S5_WIDTH))],
        out_specs=[blk, st_spec, st_spec],
        out_shape=[jax.ShapeDtypeStruct((T, B, S5_WIDTH), F32), st, st],
        scratch_shapes=[pltpu.VMEM((S5_WIDTH, 2 * S5_LANES), BF16),
                        pltpu.VMEM((2, SUBLANES, S5_LANES), F32),
                        pltpu.VMEM((2, SUBLANES, S5_LANES), F32),
                        pltpu.VMEM((tc * SUBLANES, S5_LANES), F32),
                        pltpu.VMEM((tc * SUBLANES, S5_LANES), F32)],
        compiler_params=_params(("parallel", "arbitrary")),
        name="s5",
    )(u_tm, sre0, sim0, lre, lim, ldt, bre, bim, cre, cim, d, wglu)


def _dup_head(x, kv):
    own = (_lane_iota((1, LANES)) >> 6) == kv
    return jnp.where(own, x, pltpu.roll(x, HEAD_DIM, 1))


def _head_lanes(x, half):
    return jnp.where((_lane_iota((1, LANES)) >> 6) == half, x, 0.0)


def _masked_softmax_parts(s, mask):
    s = jnp.where(mask, s, NEG_INF)
    m = jnp.max(s, axis=-1, keepdims=True)
    p = jnp.where(mask, jnp.exp(s - m), 0.0)
    l = jnp.maximum(jnp.sum(p, axis=-1, keepdims=True), 1e-30)
    return p, l


def _attend(qm, k, v, mask):
    p, l = _masked_softmax_parts(_dot_nt(qm, k) * SCALE, mask)
    return _dot(p.astype(BF16), v) / l


def _online_update(m_ref, l_ref, acc_ref, idx, s, mask, v):
    s = jnp.where(mask, s, NEG_INF)
    m_old = m_ref[idx]
    m_new = jnp.maximum(m_old, jnp.max(s, axis=-1, keepdims=True))
    p = jnp.where(mask, jnp.exp(s - m_new[:, 0:1]), 0.0)
    alpha = jnp.exp(m_old - m_new)
    l_ref[idx] = alpha * l_ref[idx] + jnp.sum(p, axis=-1, keepdims=True)
    acc_ref[idx] = alpha * acc_ref[idx] + _dot(p.astype(BF16), v)
    m_ref[idx] = m_new


def _cmp_hidden(h, w1_ref, pe_ref):
    m = h.shape[0]
    pe = pe_ref[...].astype(BF16)
    bias = _dot(pe, w1_ref[:, :256])[0:1] + _dot(pe, w1_ref[:, 256:])[1:2]
    return jax.nn.gelu(h[:, :256] + pltpu.roll(h[:, 256:], m - 1, 0) + bias)


def _overlap(n_rows, n_cols, transpose):
    shape = (n_cols, n_rows) if transpose else (n_rows, n_cols)
    n = _row_iota(shape) if not transpose else _lane_iota(shape)
    j = _lane_iota(shape) if not transpose else _row_iota(shape)
    lo = jnp.maximum(n * CMP_STRIDE, j * SLC_BLOCK)
    hi = jnp.minimum(n * CMP_STRIDE + CMP_LEN, j * SLC_BLOCK + SLC_BLOCK)
    return (jnp.maximum(hi - lo, 0).astype(F32) * (1.0 / CMP_LEN)).astype(BF16)


def _gate_tile(gs, col0, col1, comp):
    n = gs.shape[0]
    a = jnp.broadcast_to(gs[:, col0 + comp:col0 + comp + 1], (n, LANES))
    b = jnp.broadcast_to(gs[:, col1 + comp:col1 + comp + 1], (n, LANES))
    return jnp.where(_lane_iota((1, LANES)) < HEAD_DIM, a, b)


def _nsa_prompt_body(q_ref, kc16_ref, vc16_ref, ks_ref, kw_ref, vs_ref, vw_ref, g_ref,
                     w1k_ref, w2k_ref, pek_ref, w1v_ref, w2v_ref, pev_ref,
                     o_ref,
                     ksd_ref, vsd_ref, kwd_ref, vwd_ref, sel_ref, selt_ref, ocmp_ref, pg_ref,
                     qm_ref, m_ref, l_ref, acc_ref, *, T, TQ):
    kv = pl.program_id(1)
    qi = pl.program_id(2)
    m_cmp = T // CMP_STRIDE
    n_cmp = m_cmp - 1
    n_slc = T // SLC_BLOCK
    topk = min(SLC_TOPK, n_slc)
    lane = _lane_iota((1, LANES))
    left = lane < HEAD_DIM

    @pl.when(qi == 0)
    def _():
        ksd_ref[...] = _dup_head(ks_ref[...], kv).astype(BF16)
        vsd_ref[...] = _dup_head(vs_ref[...], kv).astype(BF16)
        kwd_ref[...] = _dup_head(kw_ref[...], kv).astype(BF16)
        vwd_ref[...] = _dup_head(vw_ref[...], kv).astype(BF16)

        def compressed(x16_ref, w1_ref, pe_ref, w2_ref):
            hid = _cmp_hidden(_dot(x16_ref[...].astype(BF16), w1_ref[...]), w1_ref, pe_ref)
            hid = jnp.where(kv == 0, hid[:, :LANES], hid[:, LANES:])
            return _dot(hid.astype(BF16), w2_ref[...]).astype(BF16)

        kcmp = compressed(kc16_ref, w1k_ref, pek_ref, w2k_ref)
        vcmp = compressed(vc16_ref, w1v_ref, pev_ref, w2v_ref)

        def cmp_rows(ci, carry):
            r0 = pl.multiple_of(ci * TQ, TQ)
            qpos = r0 + _row_iota((TQ, m_cmp))
            n = _lane_iota((TQ, m_cmp))
            mask = (n * CMP_STRIDE + CMP_LEN - 1 <= qpos) & (n < n_cmp)
            pg = jnp.zeros((TQ, m_cmp), F32)
            for pp in range(2):
                qt = q_ref[pl.ds(r0, TQ), pp * LANES:(pp + 1) * LANES]
                oc = []
                for e in range(2):
                    p, l = _masked_softmax_parts(_dot_nt(_head_lanes(qt, e).astype(BF16), kcmp) * SCALE, mask)
                    pg = pg + p / l
                    oc.append(_dot(p.astype(BF16), vcmp) / l)
                ocmp_ref[pl.ds(r0, TQ), pp * LANES:(pp + 1) * LANES] = jnp.where(left, oc[0], oc[1])
            pg_ref[pl.ds(r0, TQ), :] = pg
            return carry

        lax.fori_loop(0, T // TQ, cmp_rows, 0)

        ovt = _overlap(m_cmp, n_slc, transpose=True)
        imp = sum(_dot_nt(ovt, part) for part in _split3(pg_ref[...]))
        j = _row_iota((n_slc, T))
        qblk = _lane_iota((n_slc, T)) // SLC_BLOCK
        valid = j <= qblk
        forced = (j == 0) | (j == qblk) | (j == qblk - 1)
        score = jnp.where(forced, FORCED, jnp.where(valid, imp, NEG_INF))
        rank = jnp.zeros((n_slc, T), jnp.int32)
        for i in range(n_slc):
            si = score[i:i + 1, :]
            beats = (si > score) | ((si == score) & (i < j))
            rank = rank + beats.astype(jnp.int32)
        selt_ref[...] = jnp.zeros(selt_ref.shape, F32)
        selt_ref[0:n_slc, :] = jnp.where((rank < topk) & valid, 1.0, 0.0)
        sel_ref[...] = selt_ref[...].T.astype(BF16)

    r0 = pl.multiple_of(qi * TQ, TQ)
    for pp in range(2):
        qt = q_ref[pl.ds(r0, TQ), pp * LANES:(pp + 1) * LANES]
        for e in range(2):
            qm_ref[2 * pp + e] = _head_lanes(qt, e).astype(BF16)
    qpos = r0 + _row_iota((TQ, TQ))
    sel_q = sel_ref[pl.ds(r0, TQ), :]

    def reset():
        m_ref[...] = jnp.full(m_ref.shape, NEG_INF, F32)
        l_ref[...] = jnp.zeros(l_ref.shape, F32)
        acc_ref[...] = jnp.zeros(acc_ref.shape, F32)

    def result(h):
        return acc_ref[h] / jnp.maximum(l_ref[h], 1e-30)

    reset()

    def slc_tile(kb, carry):
        k0 = pl.multiple_of(kb * TQ, TQ)
        kpos = k0 + _lane_iota((TQ, TQ))
        kblk = (k0 + _lane_iota((LANES, TQ))) >> 6
        expand = jnp.where(kblk == _row_iota((LANES, TQ)), 1.0, 0.0).astype(BF16)
        mask = (_dot(sel_q, expand) > 0.5) & (kpos <= qpos)
        k, v = ksd_ref[pl.ds(k0, TQ), :], vsd_ref[pl.ds(k0, TQ), :]
        for h in range(Q_PER_KV):
            _online_update(m_ref, l_ref, acc_ref, h, _dot_nt(qm_ref[h], k) * SCALE, mask, v)
        return carry

    lax.fori_loop(0, qi + 1, slc_tile, 0)
    o_slc = [result(h) for h in range(Q_PER_KV)]

    reset()

    def win_tile(kb, carry):
        k0 = pl.multiple_of(kb * TQ, TQ)
        rel = qpos - (k0 + _lane_iota((TQ, TQ)))
        mask = (rel >= 0) & (rel < WINDOW)
        k, v = kwd_ref[pl.ds(k0, TQ), :], vwd_ref[pl.ds(k0, TQ), :]
        for h in range(Q_PER_KV):
            _online_update(m_ref, l_ref, acc_ref, h, _dot_nt(qm_ref[h], k) * SCALE, mask, v)
        return carry

    lax.fori_loop(jnp.maximum(qi - WINDOW // TQ, 0), qi + 1, win_tile, 0)

    g = g_ref[pl.ds(r0, TQ), :]
    gs = jax.nn.sigmoid(jnp.where(kv == 0, g, pltpu.roll(g, LANES - 3 * Q_PER_KV, 1)))
    for pp in range(2):
        c0, c1 = 3 * (2 * pp), 3 * (2 * pp + 1)
        o_ref[pl.ds(r0, TQ), pp * LANES:(pp + 1) * LANES] = (
            _gate_tile(gs, c0, c1, 0) * ocmp_ref[pl.ds(r0, TQ), pp * LANES:(pp + 1) * LANES]
            + _gate_tile(gs, c0, c1, 1) * jnp.where(left, o_slc[2 * pp], o_slc[2 * pp + 1])
            + _gate_tile(gs, c0, c1, 2) * jnp.where(left, result(2 * pp), result(2 * pp + 1)))


def _nsa_prompt(q, kc, vc, ks, vs, kw, vw, g, cw, tq):
    B, T, _ = q.shape
    m_cmp = T // CMP_STRIDE
    kc16 = kc.reshape(B, m_cmp, CMP_STRIDE * KV_WIDTH)
    vc16 = vc.reshape(B, m_cmp, CMP_STRIDE * KV_WIDTH)
    full = lambda w: pl.BlockSpec((None, T, w), lambda b, h, i: (b, 0, 0))
    x16 = pl.BlockSpec((None, m_cmp, CMP_STRIDE * KV_WIDTH), lambda b, h, i: (b, 0, 0))
    qo = pl.BlockSpec((None, T, 2 * LANES), lambda b, h, i: (b, 0, h))
    wspecs = [_const_spec(w.shape) for w in cw]
    head_acc = pltpu.VMEM((Q_PER_KV, tq, LANES), F32)
    return pl.pallas_call(
        functools.partial(_nsa_prompt_body, T=T, TQ=tq),
        grid=(B, NSA_KV_HEADS, T // tq),
        in_specs=[qo, x16, x16, full(KV_WIDTH), full(KV_WIDTH), full(KV_WIDTH), full(KV_WIDTH), full(KV_WIDTH)] + wspecs,
        out_specs=qo,
        out_shape=jax.ShapeDtypeStruct((B, T, NSA_WIDTH), F32),
        scratch_shapes=[pltpu.VMEM((T, LANES), BF16)] * 4 + [
            pltpu.VMEM((T, LANES), BF16), pltpu.VMEM((LANES, T), F32),
            pltpu.VMEM((T, 2 * LANES), F32), pltpu.VMEM((T, m_cmp), F32),
            pltpu.VMEM((Q_PER_KV, tq, LANES), BF16), head_acc, head_acc, head_acc],
        compiler_params=_params(("parallel", "arbitrary", "arbitrary")),
        name="nsa_prompt",
    )(q, kc16, vc16, ks, kw, vs, vw, g, *cw)


PAGES_PER_STEP = 16


def _stack_heads(q, kv):
    return jnp.concatenate(
        [_head_lanes(q[:, (2 * kv + hl // 2) * LANES:(2 * kv + hl // 2 + 1) * LANES], hl % 2)
         for hl in range(Q_PER_KV)], axis=0)


def _nsa_sample_cmp_body(pt_ref, q_ref, *refs, P, TS):
    n_pg = PAGES_PER_STEP
    kpages, vpages = refs[:n_pg], refs[n_pg:2 * n_pg]
    w1k_ref, w2k_ref, pek_ref, w1v_ref, w2v_ref, pev_ref = refs[2 * n_pg:2 * n_pg + 6]
    ocmp_ref, sel_ref, hk_ref, hv_ref = refs[2 * n_pg + 6:]
    c = pl.program_id(1)
    groups = n_pg * PAGE_SIZE // CMP_STRIDE
    m_cmp = P // CMP_STRIDE
    n_cmp = (P + TS - CMP_LEN) // CMP_STRIDE + 1
    n_slc = -(-(P + TS) // SLC_BLOCK)
    n_slc_pad = sel_ref.shape[-1]
    topk = min(SLC_TOPK, n_slc)

    r0 = pl.multiple_of(c * groups, groups)
    xk = jnp.concatenate([r[...] for r in kpages], axis=0).astype(BF16)
    hk_ref[pl.ds(r0, groups), :] = _dot(xk, w1k_ref[...])
    xv = jnp.concatenate([r[...] for r in vpages], axis=0).astype(BF16)
    hv_ref[pl.ds(r0, groups), :] = _dot(xv, w1v_ref[...])

    @pl.when(c == pl.num_programs(1) - 1)
    def _():
        hidk = _cmp_hidden(hk_ref[...], w1k_ref, pek_ref).astype(BF16)
        hidv = _cmp_hidden(hv_ref[...], w1v_ref, pev_ref).astype(BF16)
        q = q_ref[...]
        rows = Q_PER_KV * TS
        t = _row_iota((rows, 1)) % TS
        qpos = P + t
        n = _lane_iota((rows, m_cmp))
        mask = (n * CMP_STRIDE + CMP_LEN - 1 <= qpos) & (n < n_cmp)
        ov = _overlap(m_cmp, n_slc_pad, transpose=False)
        jl = _lane_iota((TS, n_slc_pad))
        qblk = (P + _row_iota((TS, 1))) // SLC_BLOCK
        valid = (jl <= qblk) & (jl < n_slc)
        forced = ((jl == 0) | (jl == qblk) | (jl == qblk - 1)) & (jl < n_slc)
        for kv in range(NSA_KV_HEADS):
            kcmp = _dot(hidk[:, kv * LANES:(kv + 1) * LANES], w2k_ref[...]).astype(BF16)
            vcmp = _dot(hidv[:, kv * LANES:(kv + 1) * LANES], w2v_ref[...]).astype(BF16)
            qs = _stack_heads(q, kv).astype(BF16)
            p, l = _masked_softmax_parts(_dot_nt(qs, kcmp) * SCALE, mask)
            ocmp_ref[kv] = _dot(p.astype(BF16), vcmp) / l
            pn = p / l
            pg = pn[0:TS] + pn[TS:2 * TS] + pn[2 * TS:3 * TS] + pn[3 * TS:4 * TS]
            imp = sum(_dot(part, ov) for part in _split3(pg))
            score = jnp.where(forced, FORCED, jnp.where(valid, imp, NEG_INF))
            rank = jnp.zeros((TS, n_slc_pad), jnp.int32)
            for i in range(n_slc):
                si = score[:, i:i + 1]
                beats = (si > score) | ((si == score) & (i < jl))
                rank = rank + beats.astype(jnp.int32)
            sel_ref[kv] = jnp.where((rank < topk) & valid, 1.0, 0.0)


def _page_specs(shape):
    return [pl.BlockSpec((None,) + shape, lambda b, c, pt, k=k: (pt[b, c * PAGES_PER_STEP + k], 0, 0))
            for k in range(PAGES_PER_STEP)]


def _nsa_sample_cmp(page_table, q, ck, cv, cw):
    S, TS, _ = q.shape
    n_pages = page_table.shape[1]
    P = n_pages * PAGE_SIZE
    n_pool = ck.shape[0]
    gpp = PAGE_SIZE // CMP_STRIDE
    ck16 = ck.reshape(n_pool, gpp, CMP_STRIDE * KV_WIDTH)
    cv16 = cv.reshape(n_pool, gpp, CMP_STRIDE * KV_WIDTH)
    m_cmp = P // CMP_STRIDE
    n_slc = -(-(P + TS) // SLC_BLOCK)
    n_slc_pad = -(-n_slc // LANES) * LANES
    steps = n_pages // PAGES_PER_STEP
    pages = _page_specs((gpp, CMP_STRIDE * KV_WIDTH))
    wspecs = [pl.BlockSpec(w.shape, lambda b, c, pt, nd=w.ndim: (0,) * nd) for w in cw]
    grid_spec = pltpu.PrefetchScalarGridSpec(
        num_scalar_prefetch=1,
        grid=(S, steps),
        in_specs=[pl.BlockSpec((None, TS, NSA_WIDTH), lambda b, c, pt: (b, 0, 0))] + pages + pages + wspecs,
        out_specs=[pl.BlockSpec((None, NSA_KV_HEADS, Q_PER_KV * TS, LANES), lambda b, c, pt: (b, 0, 0, 0)),
                   pl.BlockSpec((None, NSA_KV_HEADS, TS, n_slc_pad), lambda b, c, pt: (b, 0, 0, 0))],
        scratch_shapes=[pltpu.VMEM((m_cmp, 4 * LANES), F32), pltpu.VMEM((m_cmp, 4 * LANES), F32)],
    )
    return pl.pallas_call(
        functools.partial(_nsa_sample_cmp_body, P=P, TS=TS),
        grid_spec=grid_spec,
        out_shape=[jax.ShapeDtypeStruct((S, NSA_KV_HEADS, Q_PER_KV * TS, LANES), F32),
                   jax.ShapeDtypeStruct((S, NSA_KV_HEADS, TS, n_slc_pad), F32)],
        compiler_params=_params(("parallel", "arbitrary")),
        name="nsa_sample_cmp",
    )(page_table, q, *([ck16] * PAGES_PER_STEP), *([cv16] * PAGES_PER_STEP), *cw)


def _nsa_sample_slc_body(pt_ref, q_ref, sel_ref, ocmp_ref, g_ref, ksn_ref, vsn_ref, kwn_ref, vwn_ref,
                         kwin_ref, vwin_ref, *refs, P, TS):
    n_pg = PAGES_PER_STEP
    kpages, vpages = refs[:n_pg], refs[n_pg:2 * n_pg]
    o_ref, kwout_ref, vwout_ref, m_ref, l_ref, acc_ref = refs[2 * n_pg:]
    c = pl.program_id(1)
    rows = Q_PER_KV * TS
    nk = n_pg * PAGE_SIZE
    wb = kwin_ref.shape[0]
    n_slc_pad = sel_ref.shape[-1]
    pad = LANES - TS
    q = q_ref[...]
    t = _row_iota((rows, 1)) % TS
    qpos = P + t

    @pl.when(c == 0)
    def _():
        m_ref[...] = jnp.full(m_ref.shape, NEG_INF, F32)
        l_ref[...] = jnp.zeros(l_ref.shape, F32)
        acc_ref[...] = jnp.zeros(acc_ref.shape, F32)
        kwout_ref[0:wb - TS, :] = kwin_ref[TS:wb, :]
        kwout_ref[wb - TS:wb, :] = kwn_ref[...]
        vwout_ref[0:wb - TS, :] = vwin_ref[TS:wb, :]
        vwout_ref[wb - TS:wb, :] = vwn_ref[...]

    kall = jnp.concatenate([r[...] for r in kpages], axis=0)
    vall = jnp.concatenate([r[...] for r in vpages], axis=0)
    kpos = c * nk + _lane_iota((1, nk))
    expand = jnp.where((kpos >> 6) == _row_iota((n_slc_pad, nk)), 1.0, 0.0).astype(BF16)
    for kv in range(NSA_KV_HEADS):
        qs = _stack_heads(q, kv).astype(BF16)
        sel4 = jnp.concatenate([sel_ref[kv]] * Q_PER_KV, axis=0).astype(BF16)
        mask = (_dot(sel4, expand) > 0.5) & (kpos <= qpos)
        kd = _dup_head(kall, kv).astype(BF16)
        _online_update(m_ref, l_ref, acc_ref, kv, _dot_nt(qs, kd) * SCALE, mask, _dup_head(vall, kv).astype(BF16))

    @pl.when(c == pl.num_programs(1) - 1)
    def _():
        zpad = jnp.zeros((pad, LANES), F32)
        i = _lane_iota((1, LANES))
        gs = jax.nn.sigmoid(g_ref[...])
        for kv in range(NSA_KV_HEADS):
            qs = _stack_heads(q, kv).astype(BF16)
            sel4 = jnp.concatenate([sel_ref[kv]] * Q_PER_KV, axis=0)
            jn = P // SLC_BLOCK
            mask_n = (sel4[:, jn:jn + 1] > 0.5) & (i <= t) & (i < TS)
            kn = _dup_head(jnp.concatenate([ksn_ref[...], zpad], axis=0), kv).astype(BF16)
            vn = _dup_head(jnp.concatenate([vsn_ref[...], zpad], axis=0), kv).astype(BF16)
            _online_update(m_ref, l_ref, acc_ref, kv, _dot_nt(qs, kn) * SCALE, mask_n, vn)
            o_slc = acc_ref[kv] / jnp.maximum(l_ref[kv], 1e-30)
            kw_all = jnp.concatenate([kwin_ref[...], kwn_ref[...], zpad], axis=0)
            vw_all = jnp.concatenate([vwin_ref[...], vwn_ref[...], zpad], axis=0)
            iw = _lane_iota((1, wb + LANES))
            rel = t + wb - iw
            mask_w = (rel >= 0) & (rel < WINDOW) & (iw < wb + TS)
            o_win = _attend(qs, _dup_head(kw_all, kv).astype(BF16), _dup_head(vw_all, kv).astype(BF16), mask_w)
            gcol = lambda comp: jnp.concatenate(
                [gs[:, 3 * (Q_PER_KV * kv + hl) + comp:3 * (Q_PER_KV * kv + hl) + comp + 1]
                 for hl in range(Q_PER_KV)], axis=0)
            o = gcol(0) * ocmp_ref[kv] + gcol(1) * o_slc + gcol(2) * o_win
            for pp in range(2):
                a = o[(2 * pp) * TS:(2 * pp + 1) * TS]
                b = o[(2 * pp + 1) * TS:(2 * pp + 2) * TS]
                tile = 2 * kv + pp
                o_ref[:, tile * LANES:(tile + 1) * LANES] = jnp.where(i < HEAD_DIM, a, b)


def _nsa_sample_slc(page_table, q, sel, ocmp, g, ksn, vsn, kwn, vwn, kwin, vwin, ck, cv):
    S, TS, _ = q.shape
    n_pages = page_table.shape[1]
    P = n_pages * PAGE_SIZE
    wb = kwin.shape[1]
    n_slc_pad = sel.shape[-1]
    rows = Q_PER_KV * TS
    steps = n_pages // PAGES_PER_STEP
    pages = _page_specs((PAGE_SIZE, KV_WIDTH))
    per_seq = lambda shape: pl.BlockSpec((None,) + shape, lambda b, c, pt, nd=len(shape): (b,) + (0,) * nd)
    grid_spec = pltpu.PrefetchScalarGridSpec(
        num_scalar_prefetch=1,
        grid=(S, steps),
        in_specs=[per_seq((TS, NSA_WIDTH)), per_seq((NSA_KV_HEADS, TS, n_slc_pad)),
                  per_seq((NSA_KV_HEADS, rows, LANES)), per_seq((TS, LANES)),
                  per_seq((TS, KV_WIDTH)), per_seq((TS, KV_WIDTH)), per_seq((TS, KV_WIDTH)), per_seq((TS, KV_WIDTH)),
                  per_seq((wb, KV_WIDTH)), per_seq((wb, KV_WIDTH))] + pages + pages,
        out_specs=[per_seq((TS, NSA_WIDTH)), per_seq((wb, KV_WIDTH)), per_seq((wb, KV_WIDTH))],
        scratch_shapes=[pltpu.VMEM((NSA_KV_HEADS, rows, LANES), F32)] * 3,
    )
    return pl.pallas_call(
        functools.partial(_nsa_sample_slc_body, P=P, TS=TS),
        grid_spec=grid_spec,
        out_shape=[jax.ShapeDtypeStruct((S, TS, NSA_WIDTH), F32),
                   jax.ShapeDtypeStruct((S, wb, KV_WIDTH), F32), jax.ShapeDtypeStruct((S, wb, KV_WIDTH), F32)],
        compiler_params=_params(("parallel", "arbitrary")),
        name="nsa_sample_slc",
    )(page_table, q, sel, ocmp, g, ksn, vsn, kwn, vwn, kwin, vwin,
      *([ck] * PAGES_PER_STEP), *([cv] * PAGES_PER_STEP))


def _merge_body(x_ref, glu_ref, o_ref, nw_ref, wg_ref, wup_ref, wnsa_ref, wout_ref, out_ref):
    x = x_ref[...]
    gate = jax.nn.sigmoid(_dot(_rms(x, nw_ref[...]).astype(BF16), wg_ref[...]))
    s5_out = _dot(glu_ref[...].astype(BF16), wup_ref[...])
    nsa_out = _dot(o_ref[...].astype(BF16), wnsa_ref[...])
    merged = gate[:, :D_MODEL] * s5_out + gate[:, D_MODEL:] * nsa_out
    out_ref[...] = x + _dot(merged.astype(BF16), wout_ref[...])


def _merge(x, glu, o, nw, wg, wup, wnsa, wout, tt):
    B, T, _ = x.shape
    row = lambda w: pl.BlockSpec((None, tt, w), lambda b, i: (b, i, 0))
    return pl.pallas_call(
        _merge_body,
        grid=(B, T // tt),
        in_specs=[row(D_MODEL), row(S5_WIDTH), row(NSA_WIDTH), _const_spec(nw.shape), _const_spec(wg.shape),
                  _const_spec(wup.shape), _const_spec(wnsa.shape), _const_spec(wout.shape)],
        out_specs=row(D_MODEL),
        out_shape=jax.ShapeDtypeStruct((B, T, D_MODEL), F32),
        compiler_params=_params(("parallel", "parallel")),
        name="merge",
    )(x, glu, o, nw, wg, wup, wnsa, wout)


def _ffn_body(x_ref, nw_ref, w1_ref, w2_ref, out_ref):
    x = x_ref[...]
    a = jnp.maximum(_dot(_rms(x, nw_ref[...]).astype(BF16), w1_ref[...]), 0.0)
    out_ref[...] = x + _dot((a * a).astype(BF16), w2_ref[...])


def _ffn(x, nw, w1, w2, tt):
    B, T, _ = x.shape
    row = pl.BlockSpec((None, tt, D_MODEL), lambda b, i: (b, i, 0))
    return pl.pallas_call(
        _ffn_body,
        grid=(B, T // tt),
        in_specs=[row, _const_spec(nw.shape), _const_spec(w1.shape), _const_spec(w2.shape)],
        out_specs=row,
        out_shape=jax.ShapeDtypeStruct((B, T, D_MODEL), F32),
        compiler_params=_params(("parallel", "parallel")),
        name="ffn",
    )(x, nw, w1, w2)


def _rope_tables(pos):
    half = HEAD_DIM // 2
    inv = ROPE_THETA ** (-jnp.arange(half, dtype=F32) / half)
    ang = pos.astype(F32)[:, None] * inv[None, :]
    cos, sin = jnp.cos(ang), jnp.sin(ang)
    return (jnp.concatenate([cos, cos, cos, cos], axis=-1),
            jnp.concatenate([-sin, sin, -sin, sin], axis=-1))


def _block_diag(w):
    G, a, b = w.shape
    return jnp.einsum('gab,gk->gakb', w, jnp.eye(G, dtype=w.dtype)).reshape(G * a, G * b)


def _cmp_weights(pe, w1, w2):
    eye = jnp.eye(NSA_KV_HEADS, dtype=F32)
    half = CMP_LEN // 2
    big = lambda w: jnp.einsum('rdf,hk->rhdkf', w, eye).reshape(half * KV_WIDTH, NSA_KV_HEADS * CMP_HIDDEN)
    w1b = jnp.concatenate([big(w1[:half]), big(w1[half:])], axis=1).astype(BF16)
    w2d = jnp.concatenate([w2, w2], axis=1).astype(BF16)
    flat = lambda p: jnp.broadcast_to(p[:, None, :], (half, NSA_KV_HEADS, HEAD_DIM)).reshape(1, half * KV_WIDTH)
    pe8 = jnp.concatenate([flat(pe[:half]), flat(pe[half:]), jnp.zeros((SUBLANES - 2, half * KV_WIDTH), F32)], axis=0)
    return w1b, w2d, pe8


def kernel(x_prompt, x_sample, cache_k_cmp, cache_v_cmp, cache_k_slc, cache_v_slc, state_k_win, state_v_win, state_s5_re, state_s5_im, page_table, norm_mix_w, w_in, s5_lam_re, s5_lam_im, s5_log_dt, s5_b_re, s5_b_im, s5_c_re, s5_c_im, s5_d, s5_w_glu, s5_w_up, q_norm_w, k_norm_w, cmp_pe_k, cmp_wk1, cmp_wk2, cmp_pe_v, cmp_wv1, cmp_wv2, nsa_w_up, w_out, norm_ffn_w, w_ffn1, w_ffn2):
    B, T, _ = x_prompt.shape
    S, TS, _ = x_sample.shape
    n_pages = page_table.shape[1]
    P = n_pages * PAGE_SIZE
    n_pool = cache_k_cmp.shape[1]
    assert norm_mix_w.shape[0] == 1 and B % SUBLANES == 0 and S % SUBLANES == 0
    assert TS < CMP_STRIDE and P % SLC_BLOCK == 0 and n_pages % PAGES_PER_STEP == 0

    w = w_in[0]
    cols = lambda a, n: w[:, a:a + n]
    w_a = jnp.concatenate(
        [cols(0, 512), cols(512, 512), cols(1024, 128), cols(1280, 128), cols(1536, 128),
         cols(1152, 128), cols(1408, 128), cols(1664, 128), cols(1792, N_GATE),
         jnp.zeros((D_MODEL, W_A_COLS - 1792 - N_GATE), F32)], axis=1).astype(BF16)
    w_g = w[:, OFF_GMIX:].astype(BF16)
    nw_mix = norm_mix_w[0][None, :]
    qkw = jnp.concatenate([jnp.tile(q_norm_w[0], NSA_HEADS)]
                          + [jnp.tile(k_norm_w[0, i], NSA_KV_HEADS) for i in range(3)])[None, :]
    li = np.arange(LANES)
    seg = jnp.asarray((li[:, None] // HEAD_DIM) == (li[None, :] // HEAD_DIM), BF16)
    lre, lim = s5_lam_re[0].reshape(1, S5_LANES), s5_lam_im[0].reshape(1, S5_LANES)
    ldt = jnp.repeat(s5_log_dt[0], S5_STATE)[None, :]
    bre = _block_diag(jnp.swapaxes(s5_b_re[0], 1, 2))
    bim = _block_diag(jnp.swapaxes(s5_b_im[0], 1, 2))
    cre = _block_diag(jnp.swapaxes(s5_c_re[0], 1, 2)).astype(BF16)
    cim = _block_diag(jnp.swapaxes(s5_c_im[0], 1, 2)).astype(BF16)
    s5d = s5_d[0][None, :]
    wglu = s5_w_glu[0].astype(BF16)
    wup = s5_w_up[0].astype(BF16)
    cw = _cmp_weights(cmp_pe_k[0], cmp_wk1[0], cmp_wk2[0]) + _cmp_weights(cmp_pe_v[0], cmp_wv1[0], cmp_wv2[0])
    wnsa = nsa_w_up[0].astype(BF16)
    wout = w_out[0].astype(BF16)
    nw_ffn = norm_ffn_w[0][None, :]
    w1 = w_ffn1[0].astype(BF16)
    w2 = w_ffn2[0].astype(BF16)

    def trunk(x, glu, o, tt):
        return _ffn(_merge(x, glu, o, nw_mix, w_g, wup, wnsa, wout, tt), nw_ffn, w1, w2, tt)

    cos_p, sin_p = _rope_tables(jnp.arange(T, dtype=jnp.int32))
    u, q, kc, ks, kw, vc, vs, vw, g = _inproj(x_prompt, cos_p, sin_p, nw_mix, w_a, qkw, seg, 512)
    zeros_st = jnp.zeros((B, S5_LANES), F32)
    glu_tm, sre_p, sim_p = _s5(jnp.swapaxes(u, 0, 1), zeros_st, zeros_st, lre, lim, ldt, bre, bim, cre, cim, s5d, wglu, 64)
    o_p = _nsa_prompt(q, kc, vc, ks, vs, kw, vw, g, cw, 256)
    y_prompt = trunk(x_prompt, jnp.swapaxes(glu_tm, 0, 1), o_p, 512)

    n_s = S * TS
    cos_s, sin_s = _rope_tables(P + jnp.arange(TS, dtype=jnp.int32))
    cos_s, sin_s = jnp.tile(cos_s, (S, 1)), jnp.tile(sin_s, (S, 1))
    xs = x_sample.reshape(1, n_s, D_MODEL)
    us, qs, kcs, kss, kws, vcs, vss, vws, gs = _inproj(xs, cos_s, sin_s, nw_mix, w_a, qkw, seg, n_s)
    seq = lambda a: a.reshape(S, TS, a.shape[-1])
    u_tm = jnp.swapaxes(seq(us), 0, 1)
    glu_s_tm, sre_s, sim_s = _s5(u_tm, state_s5_re[0].reshape(S, S5_LANES), state_s5_im[0].reshape(S, S5_LANES),
                                 lre, lim, ldt, bre, bim, cre, cim, s5d, wglu, TS)
    pool = lambda c: c[0].reshape(n_pool, PAGE_SIZE, KV_WIDTH)
    ocmp_s, sel_s = _nsa_sample_cmp(page_table, seq(qs), pool(cache_k_cmp), pool(cache_v_cmp), cw)
    win = lambda s: s[0].reshape(S, s.shape[2], KV_WIDTH)
    o_s, kwin_s, vwin_s = _nsa_sample_slc(page_table, seq(qs), sel_s, ocmp_s, seq(gs), seq(kss), seq(vss), seq(kws),
                                          seq(vws), win(state_k_win), win(state_v_win),
                                          pool(cache_k_slc), pool(cache_v_slc))
    glu_s = jnp.swapaxes(glu_s_tm, 0, 1).reshape(1, n_s, S5_WIDTH)
    y_sample = trunk(xs, glu_s, o_s.reshape(1, n_s, NSA_WIDTH), n_s).reshape(S, TS, D_MODEL)

    heads_p = lambda a: a.reshape(1, B, a.shape[1], NSA_KV_HEADS, HEAD_DIM)
    heads_s = lambda a: a.reshape(1, S, -1, NSA_KV_HEADS, HEAD_DIM)
    keep = min(WINDOW, T)
    st_p = lambda a: a.reshape(1, B, S5_GROUPS, S5_STATE)
    st_s = lambda a: a.reshape(1, S, S5_GROUPS, S5_STATE)
    return (y_prompt, y_sample,
            heads_p(kc), heads_s(kcs), heads_p(vc), heads_s(vcs),
            heads_p(ks), heads_s(kss), heads_p(vs), heads_s(vss),
            heads_p(kw[:, T - keep:]), heads_s(kwin_s), heads_p(vw[:, T - keep:]), heads_s(vwin_s),
            st_p(sre_p), st_s(sre_s), st_p(sim_p), st_s(sim_s))
```

```python
import functools

import numpy as np
import jax
import jax.numpy as jnp
from jax import lax
from jax.experimental import pallas as pl
from jax.experimental.pallas import tpu as pltpu

F32 = jnp.float32
BF16 = jnp.bfloat16

D_MODEL = 1024
PAGE_SIZE = 128
S5_WIDTH = 512
S5_GROUPS = 32
S5_STATE = 64
S5_LANES = S5_GROUPS * S5_STATE
HEAD_DIM = 64
NSA_HEADS = 8
NSA_KV_HEADS = 2
Q_PER_KV = 4
NSA_WIDTH = 512
KV_WIDTH = 128
CMP_LEN = 32
CMP_STRIDE = 16
CMP_HIDDEN = 128
SLC_BLOCK = 64
SLC_TOPK = 16
WINDOW = 512
ROPE_THETA = 10000.0
EPS = 1e-6
NEG_INF = -1e30
FORCED = 1e9
SCALE = HEAD_DIM ** -0.5
N_GATE = 3 * NSA_HEADS
OFF_GMIX = 2 * 512 + 6 * KV_WIDTH + N_GATE
W_A_COLS = 1920

LANES = 128
SUBLANES = 8
VMEM_LIMIT = 56 * 1024 * 1024


def _params(sem):
    return pltpu.CompilerParams(dimension_semantics=sem, vmem_limit_bytes=VMEM_LIMIT)


def _const_spec(shape):
    nd = len(shape)
    return pl.BlockSpec(shape, lambda *_: (0,) * nd)


def _dot(a, b):
    return jnp.dot(a, b, preferred_element_type=F32)


def _dot_nt(a, b):
    return lax.dot_general(a, b, (((1,), (1,)), ((), ())), preferred_element_type=F32)


def _rms(x, w):
    r = lax.rsqrt(jnp.mean(x * x, axis=-1, keepdims=True) + EPS)
    return x * r * w


def _lane_iota(shape):
    return lax.broadcasted_iota(jnp.int32, shape, len(shape) - 1)


def _row_iota(shape):
    return lax.broadcasted_iota(jnp.int32, shape, len(shape) - 2)


def _split3(x):
    hi = x.astype(BF16)
    r1 = x - hi.astype(F32)
    mid = r1.astype(BF16)
    lo = (r1 - mid.astype(F32)).astype(BF16)
    return hi, mid, lo


def _inproj_body(x_ref, nw_ref, w_ref, qkw_ref, seg_ref, cos_ref, sin_ref,
                 u_ref, q_ref, kc_ref, ks_ref, kw_ref, vc_ref, vs_ref, vw_ref, g_ref):
    xn = _rms(x_ref[...], nw_ref[...]).astype(BF16)
    z = _dot(xn, w_ref[...])
    u_ref[...] = z[:, :512]
    cos = cos_ref[...]
    sin = sin_ref[...]
    seg = seg_ref[...]
    first_half = (_lane_iota((1, LANES)) % HEAD_DIM) < (HEAD_DIM // 2)

    def norm_rope(t, w):
        ss = _dot((t * t).astype(BF16), seg)
        t = t * lax.rsqrt(ss * (1.0 / HEAD_DIM) + EPS) * w
        rot = jnp.where(first_half, pltpu.roll(t, LANES - HEAD_DIM // 2, 1), pltpu.roll(t, HEAD_DIM // 2, 1))
        return t * cos + rot * sin

    for j in range(4):
        q_ref[:, j * LANES:(j + 1) * LANES] = norm_rope(
            z[:, 512 + j * LANES:512 + (j + 1) * LANES], qkw_ref[:, j * LANES:(j + 1) * LANES])
    for j, r in enumerate((kc_ref, ks_ref, kw_ref)):
        r[...] = norm_rope(z[:, 1024 + j * LANES:1024 + (j + 1) * LANES],
                           qkw_ref[:, 512 + j * LANES:512 + (j + 1) * LANES])
    vc_ref[...] = z[:, 1408:1536]
    vs_ref[...] = z[:, 1536:1664]
    vw_ref[...] = z[:, 1664:1792]
    g_ref[...] = z[:, 1792:1920]


def _inproj(x, cos_t, sin_t, nw, w_a, qkw, seg, tt):
    B, T, _ = x.shape
    row = lambda w: pl.BlockSpec((None, tt, w), lambda b, i: (b, i, 0))
    tab = pl.BlockSpec((tt, LANES), lambda b, i: (i, 0))
    kv = jax.ShapeDtypeStruct((B, T, KV_WIDTH), F32)
    out_shape = [jax.ShapeDtypeStruct((B, T, 512), F32), jax.ShapeDtypeStruct((B, T, 512), F32)] + [kv] * 7
    return pl.pallas_call(
        _inproj_body,
        grid=(B, T // tt),
        in_specs=[row(D_MODEL), _const_spec((1, D_MODEL)), _const_spec((D_MODEL, W_A_COLS)),
                  _const_spec((1, 896)), _const_spec((LANES, LANES)), tab, tab],
        out_specs=[row(512), row(512)] + [row(KV_WIDTH)] * 7,
        out_shape=out_shape,
        compiler_params=_params(("parallel", "parallel")),
        name="inproj",
    )(x, nw, w_a, qkw, seg, cos_t, sin_t)


def _s5_body(u_ref, sre0_ref, sim0_ref, lre_ref, lim_ref, ldt_ref, bre_ref, bim_ref, cre_ref, cim_ref,
             d_ref, wglu_ref, out_ref, sre_ref, sim_ref,
             bbar_ref, a_ref, st_ref, xr_ref, xi_ref, *, tc):
    c = pl.program_id(1)
    rows = tc * SUBLANES

    @pl.when(c == 0)
    def _():
        lr, li = lre_ref[...], lim_ref[...]
        dt = jnp.exp(ldt_ref[...])
        mag = jnp.exp(lr * dt)
        ab_re, ab_im = mag * jnp.cos(li * dt), mag * jnp.sin(li * dt)
        den = lr * lr + li * li
        f_re = ((ab_re - 1.0) * lr + ab_im * li) / den
        f_im = (ab_im * lr - (ab_re - 1.0) * li) / den
        bre, bim = bre_ref[...], bim_ref[...]
        bbar_ref[:, :S5_LANES] = (f_re * bre - f_im * bim).astype(BF16)
        bbar_ref[:, S5_LANES:] = (f_re * bim + f_im * bre).astype(BF16)
        a_ref[0] = jnp.broadcast_to(ab_re, (SUBLANES, S5_LANES))
        a_ref[1] = jnp.broadcast_to(ab_im, (SUBLANES, S5_LANES))
        st_ref[0] = sre0_ref[...]
        st_ref[1] = sim0_ref[...]

    u = u_ref[...].reshape(rows, S5_WIDTH)
    ub = u.astype(BF16)
    xr_ref[...] = _dot(ub, bbar_ref[:, :S5_LANES])
    xi_ref[...] = _dot(ub, bbar_ref[:, S5_LANES:])

    slab = 512
    for lc in range(S5_LANES // slab):
        sl = slice(lc * slab, (lc + 1) * slab)
        ar, ai = a_ref[0, :, sl], a_ref[1, :, sl]

        def step(t, carry, sl=sl, ar=ar, ai=ai):
            sr, si = carry
            r0 = pl.multiple_of(t * SUBLANES, SUBLANES)
            nr = ar * sr - ai * si + xr_ref[pl.ds(r0, SUBLANES), sl]
            ni = ar * si + ai * sr + xi_ref[pl.ds(r0, SUBLANES), sl]
            xr_ref[pl.ds(r0, SUBLANES), sl] = nr
            xi_ref[pl.ds(r0, SUBLANES), sl] = ni
            return nr, ni

        sr, si = lax.fori_loop(0, tc, step, (st_ref[0, :, sl], st_ref[1, :, sl]), unroll=min(tc, 8))
        st_ref[0, :, sl] = sr
        st_ref[1, :, sl] = si

    y = (_dot(xr_ref[...].astype(BF16), cre_ref[...]) - _dot(xi_ref[...].astype(BF16), cim_ref[...])
         + d_ref[...] * u)
    yg = jax.nn.gelu(y)
    glu = yg * jax.nn.sigmoid(_dot(yg.astype(BF16), wglu_ref[...]))
    out_ref[...] = glu.reshape(tc, SUBLANES, S5_WIDTH)

    @pl.when(c == pl.num_programs(1) - 1)
    def _():
        sre_ref[...] = st_ref[0]
        sim_ref[...] = st_ref[1]


def _s5(u_tm, sre0, sim0, lre, lim, ldt, bre, bim, cre, cim, d, wglu, tc):
    T, B, _ = u_tm.shape
    st_spec = pl.BlockSpec((SUBLANES, S5_LANES), lambda b, c: (b, 0))
    blk = pl.BlockSpec((tc, SUBLANES, S5_WIDTH), lambda b, c: (c, b, 0))
    st = jax.ShapeDtypeStruct((B, S5_LANES), F32)
    return pl.pallas_call(
        functools.partial(_s5_body, tc=tc),
        grid=(B // SUBLANES, T // tc),
        in_specs=[blk, st_spec, st_spec,
                  _const_spec((1, S5_LANES)), _const_spec((1, S5_LANES)), _const_spec((1, S5_LANES)),
                  _const_spec((S5_WIDTH, S5_LANES)), _const_spec((S5_WIDTH, S5_LANES)),
                  _const_spec((S5_LANES, S5_WIDTH)), _const_spec((S5_LANES, S5_WIDTH)),
                  _const_spec((1, S5_WIDTH)), _const_spec((S5_WIDTH, S5_WIDTH))],
        out_specs=[blk, st_spec, st_spec],
        out_shape=[jax.ShapeDtypeStruct((T, B, S5_WIDTH), F32), st, st],
        scratch_shapes=[pltpu.VMEM((S5_WIDTH, 2 * S5_LANES), BF16),
                        pltpu.VMEM((2, SUBLANES, S5_LANES), F32),
                        pltpu.VMEM((2, SUBLANES, S5_LANES), F32),
                        pltpu.VMEM((tc * SUBLANES, S5_LANES), F32),
                        pltpu.VMEM((tc * SUBLANES, S5_LANES), F32)],
        compiler_params=_params(("parallel", "arbitrary")),
        name="s5",
    )(u_tm, sre0, sim0, lre, lim, ldt, bre, bim, cre, cim, d, wglu)


def _dup_head(x, kv):
    own = (_lane_iota((1, LANES)) >> 6) == kv
    return jnp.where(own, x, pltpu.roll(x, HEAD_DIM, 1))


def _head_lanes(x, half):
    return jnp.where((_lane_iota((1, LANES)) >> 6) == half, x, 0.0)


def _masked_softmax_parts(s, mask):
    s = jnp.where(mask, s, NEG_INF)
    m = jnp.max(s, axis=-1, keepdims=True)
    p = jnp.where(mask, jnp.exp(s - m), 0.0)
    l = jnp.maximum(jnp.sum(p, axis=-1, keepdims=True), 1e-30)
    return p, l


def _attend(qm, k, v, mask):
    p, l = _masked_softmax_parts(_dot_nt(qm, k) * SCALE, mask)
    return _dot(p.astype(BF16), v) / l


def _online_update(m_ref, l_ref, acc_ref, idx, s, mask, v):
    s = jnp.where(mask, s, NEG_INF)
    m_old = m_ref[idx]
    m_new = jnp.maximum(m_old, jnp.max(s, axis=-1, keepdims=True))
    p = jnp.where(mask, jnp.exp(s - m_new[:, 0:1]), 0.0)
    alpha = jnp.exp(m_old - m_new)
    l_ref[idx] = alpha * l_ref[idx] + jnp.sum(p, axis=-1, keepdims=True)
    acc_ref[idx] = alpha * acc_ref[idx] + _dot(p.astype(BF16), v)
    m_ref[idx] = m_new


def _cmp_hidden(h, w1_ref, pe_ref):
    m = h.shape[0]
    pe = pe_ref[...].astype(BF16)
    bias = _dot(pe, w1_ref[:, :256])[0:1] + _dot(pe, w1_ref[:, 256:])[1:2]
    return jax.nn.gelu(h[:, :256] + pltpu.roll(h[:, 256:], m - 1, 0) + bias)


def _overlap(n_rows, n_cols, transpose):
    shape = (n_cols, n_rows) if transpose else (n_rows, n_cols)
    n = _row_iota(shape) if not transpose else _lane_iota(shape)
    j = _lane_iota(shape) if not transpose else _row_iota(shape)
    lo = jnp.maximum(n * CMP_STRIDE, j * SLC_BLOCK)
    hi = jnp.minimum(n * CMP_STRIDE + CMP_LEN, j * SLC_BLOCK + SLC_BLOCK)
    return (jnp.maximum(hi - lo, 0).astype(F32) * (1.0 / CMP_LEN)).astype(BF16)


def _gate_tile(gs, col0, col1, comp):
    n = gs.shape[0]
    a = jnp.broadcast_to(gs[:, col0 + comp:col0 + comp + 1], (n, LANES))
    b = jnp.broadcast_to(gs[:, col1 + comp:col1 + comp + 1], (n, LANES))
    return jnp.where(_lane_iota((1, LANES)) < HEAD_DIM, a, b)


def _ones_row_rows(n):
    return jnp.where(_row_iota((HEAD_DIM, n)) == 0, 1.0, 0.0)


def _online_update_t(m_ref, acc_ref, h, s_t, v_t):
    m_old = m_ref[h]
    m_new = jnp.maximum(m_old, jnp.max(s_t, axis=0, keepdims=True))
    p_t = jnp.exp(s_t - m_new[0:1])
    alpha = jnp.exp(m_old - m_new)
    acc_ref[h] = alpha[0:1] * acc_ref[h] + _dot(v_t, p_t.astype(BF16))
    m_ref[h] = m_new


def _softmax_result_t(m_ref, acc_ref, h):
    acc = acc_ref[h]
    out = acc[0:HEAD_DIM] / jnp.maximum(acc[HEAD_DIM:HEAD_DIM + 1], 1e-30)
    return jnp.where(m_ref[h][0:1] > 0.5 * NEG_INF, out, 0.0)


def _nsa_prompt_body(q_ref, kc16_ref, vc16_ref, ks_ref, kw_ref, vs_ref, vw_ref, g_ref,
                     w1k_ref, w2k_ref, pek_ref, w1v_ref, w2v_ref, pev_ref,
                     o_ref,
                     ksd_ref, kwd_ref, vst_ref, vwt_ref, kcmp_ref, vcmpt_ref,
                     qm_ref, ms_ref, accs_ref, mw_ref, accw_ref, *, T, TQ):
    kv = pl.program_id(1)
    qi = pl.program_id(2)
    m_cmp = T // CMP_STRIDE
    n_cmp = m_cmp - 1
    n_slc = T // SLC_BLOCK
    topk = min(SLC_TOPK, n_slc)
    n_kt = T // TQ

    @pl.when(qi == 0)
    def _():
        ksd_ref[...] = _dup_head(ks_ref[...], kv).astype(BF16)
        kwd_ref[...] = _dup_head(kw_ref[...], kv).astype(BF16)
        ones = _ones_row_rows(TQ).astype(BF16)
        for kt in range(n_kt):
            for src, dst in ((vs_ref, vst_ref), (vw_ref, vwt_ref)):
                vt = src[kt * TQ:(kt + 1) * TQ, :].T
                dst[kt, 0:HEAD_DIM, :] = jnp.where(kv == 0, vt[0:HEAD_DIM], vt[HEAD_DIM:]).astype(BF16)
                dst[kt, HEAD_DIM:, :] = ones

        def compressed(x16_ref, w1_ref, pe_ref, w2_ref):
            hid = _cmp_hidden(_dot(x16_ref[...].astype(BF16), w1_ref[...]), w1_ref, pe_ref)
            hid = jnp.where(kv == 0, hid[:, :LANES], hid[:, LANES:])
            return _dot(hid.astype(BF16), w2_ref[...])

        kcmp_ref[...] = compressed(kc16_ref, w1k_ref, pek_ref, w2k_ref).astype(BF16)
        vcmp_t = compressed(vc16_ref, w1v_ref, pev_ref, w2v_ref).T
        vcmpt_ref[0:HEAD_DIM, :] = vcmp_t[0:HEAD_DIM].astype(BF16)
        vcmpt_ref[HEAD_DIM:, :] = jnp.zeros((HEAD_DIM, m_cmp), BF16)

    r0 = pl.multiple_of(qi * TQ, TQ)
    for pp in range(2):
        qt = q_ref[pl.ds(r0, TQ), pp * LANES:(pp + 1) * LANES] * SCALE
        for e in range(2):
            qm_ref[2 * pp + e] = _head_lanes(qt, e).astype(BF16)
    qpos = r0 + _lane_iota((1, TQ))

    n = _row_iota((m_cmp, TQ))
    mask_c = (n * CMP_STRIDE + CMP_LEN - 1 <= qpos) & (n < n_cmp)
    pg = jnp.zeros((m_cmp, TQ), F32)
    o_cmp = []
    for h in range(Q_PER_KV):
        s_t = jnp.where(mask_c, _dot_nt(kcmp_ref[...], qm_ref[h]), NEG_INF)
        p_t = jnp.where(mask_c, jnp.exp(s_t - jnp.max(s_t, axis=0, keepdims=True)), 0.0)
        inv_l = 1.0 / jnp.maximum(jnp.sum(p_t, axis=0, keepdims=True), 1e-30)
        pg = pg + p_t * inv_l
        o_cmp.append(_dot(vcmpt_ref[...], p_t.astype(BF16))[0:HEAD_DIM] * inv_l)

    ovt = _overlap(m_cmp, n_slc, transpose=True)
    imp = sum(_dot(ovt, part) for part in _split3(pg))
    j = _row_iota((n_slc, TQ))
    qblk = qpos // SLC_BLOCK
    valid = j <= qblk
    forced = (j == 0) | (j == qblk) | (j == qblk - 1)
    score = jnp.where(forced, FORCED, jnp.where(valid, imp, NEG_INF))
    rank = jnp.zeros((n_slc, TQ), jnp.int32)
    for i in range(n_slc):
        si = score[i:i + 1, :]
        beats = (si > score) | ((si == score) & (i < j))
        rank = rank + beats.astype(jnp.int32)
    sel = jnp.concatenate([jnp.where((rank < topk) & valid, 1.0, 0.0),
                           jnp.zeros((LANES - n_slc, TQ), F32)], axis=0).astype(BF16)

    for m_r, acc_r in ((ms_ref, accs_ref), (mw_ref, accw_ref)):
        m_r[...] = jnp.full(m_r.shape, NEG_INF, F32)
        acc_r[...] = jnp.zeros(acc_r.shape, F32)

    def slc_tile(kb, carry):
        k0 = pl.multiple_of(kb * TQ, TQ)
        kpos = k0 + _row_iota((TQ, 1))
        expand = jnp.where(((k0 + _row_iota((TQ, LANES))) >> 6) == _lane_iota((TQ, LANES)), 1.0, 0.0).astype(BF16)
        bias = jnp.where((_dot(expand, sel) > 0.5) & (kpos <= qpos), 0.0, NEG_INF)
        k = ksd_ref[pl.ds(k0, TQ), :]
        for h in range(Q_PER_KV):
            _online_update_t(ms_ref, accs_ref, h, _dot_nt(k, qm_ref[h]) + bias, vst_ref[kb])
        return carry

    lax.fori_loop(0, qi + 1, slc_tile, 0)

    def win_tile(kb, carry):
        k0 = pl.multiple_of(kb * TQ, TQ)
        rel = qpos - (k0 + _row_iota((TQ, 1)))
        bias = jnp.where((rel >= 0) & (rel < WINDOW), 0.0, NEG_INF)
        k = kwd_ref[pl.ds(k0, TQ), :]
        for h in range(Q_PER_KV):
            _online_update_t(mw_ref, accw_ref, h, _dot_nt(k, qm_ref[h]) + bias, vwt_ref[kb])
        return carry

    lax.fori_loop(jnp.maximum(qi - WINDOW // TQ, 0), qi + 1, win_tile, 0)

    g = g_ref[pl.ds(r0, TQ), :]
    gs_t = jax.nn.sigmoid(jnp.where(kv == 0, g, pltpu.roll(g, LANES - 3 * Q_PER_KV, 1))).T
    for pp in range(2):
        halves = []
        for e in range(2):
            h = 2 * pp + e
            halves.append(gs_t[3 * h:3 * h + 1] * o_cmp[h]
                          + gs_t[3 * h + 1:3 * h + 2] * _softmax_result_t(ms_ref, accs_ref, h)
                          + gs_t[3 * h + 2:3 * h + 3] * _softmax_result_t(mw_ref, accw_ref, h))
        o_ref[pl.ds(r0, TQ), pp * LANES:(pp + 1) * LANES] = jnp.concatenate(halves, axis=0).T


def _nsa_prompt(q, kc, vc, ks, vs, kw, vw, g, cw, tq):
    B, T, _ = q.shape
    m_cmp = T // CMP_STRIDE
    kc16 = kc.reshape(B, m_cmp, CMP_STRIDE * KV_WIDTH)
    vc16 = vc.reshape(B, m_cmp, CMP_STRIDE * KV_WIDTH)
    full = lambda w: pl.BlockSpec((None, T, w), lambda b, h, i: (b, 0, 0))
    x16 = pl.BlockSpec((None, m_cmp, CMP_STRIDE * KV_WIDTH), lambda b, h, i: (b, 0, 0))
    qo = pl.BlockSpec((None, T, 2 * LANES), lambda b, h, i: (b, 0, h))
    wspecs = [_const_spec(w.shape) for w in cw]
    k_dup = pltpu.VMEM((T, LANES), BF16)
    v_t = pltpu.VMEM((T // tq, LANES, tq), BF16)
    run_max = pltpu.VMEM((Q_PER_KV, SUBLANES, tq), F32)
    run_acc = pltpu.VMEM((Q_PER_KV, LANES, tq), F32)
    return pl.pallas_call(
        functools.partial(_nsa_prompt_body, T=T, TQ=tq),
        grid=(B, NSA_KV_HEADS, T // tq),
        in_specs=[qo, x16, x16, full(KV_WIDTH), full(KV_WIDTH), full(KV_WIDTH), full(KV_WIDTH), full(KV_WIDTH)] + wspecs,
        out_specs=qo,
        out_shape=jax.ShapeDtypeStruct((B, T, NSA_WIDTH), F32),
        scratch_shapes=[k_dup, k_dup, v_t, v_t,
                        pltpu.VMEM((m_cmp, LANES), BF16), pltpu.VMEM((LANES, m_cmp), BF16),
                        pltpu.VMEM((Q_PER_KV, tq, LANES), BF16), run_max, run_acc, run_max, run_acc],
        compiler_params=_params(("parallel", "arbitrary", "arbitrary")),
        name="nsa_prompt",
    )(q, kc16, vc16, ks, kw, vs, vw, g, *cw)


PAGES_PER_STEP = 16


def _stack_heads(q, kv):
    return jnp.concatenate(
        [_head_lanes(q[:, (2 * kv + hl // 2) * LANES:(2 * kv + hl // 2 + 1) * LANES], hl % 2)
         for hl in range(Q_PER_KV)], axis=0)


def _nsa_sample_cmp_body(pt_ref, q_ref, *refs, P, TS):
    n_pg = PAGES_PER_STEP
    kpages, vpages = refs[:n_pg], refs[n_pg:2 * n_pg]
    w1k_ref, w2k_ref, pek_ref, w1v_ref, w2v_ref, pev_ref = refs[2 * n_pg:2 * n_pg + 6]
    ocmp_ref, sel_ref, hk_ref, hv_ref = refs[2 * n_pg + 6:]
    c = pl.program_id(1)
    groups = n_pg * PAGE_SIZE // CMP_STRIDE
    m_cmp = P // CMP_STRIDE
    n_cmp = (P + TS - CMP_LEN) // CMP_STRIDE + 1
    n_slc = -(-(P + TS) // SLC_BLOCK)
    n_slc_pad = sel_ref.shape[-1]
    topk = min(SLC_TOPK, n_slc)

    r0 = pl.multiple_of(c * groups, groups)
    xk = jnp.concatenate([r[...] for r in kpages], axis=0).astype(BF16)
    hk_ref[pl.ds(r0, groups), :] = _dot(xk, w1k_ref[...])
    xv = jnp.concatenate([r[...] for r in vpages], axis=0).astype(BF16)
    hv_ref[pl.ds(r0, groups), :] = _dot(xv, w1v_ref[...])

    @pl.when(c == pl.num_programs(1) - 1)
    def _():
        hidk = _cmp_hidden(hk_ref[...], w1k_ref, pek_ref).astype(BF16)
        hidv = _cmp_hidden(hv_ref[...], w1v_ref, pev_ref).astype(BF16)
        q = q_ref[...]
        rows = Q_PER_KV * TS
        t = _row_iota((rows, 1)) % TS
        qpos = P + t
        n = _lane_iota((rows, m_cmp))
        mask = (n * CMP_STRIDE + CMP_LEN - 1 <= qpos) & (n < n_cmp)
        ov = _overlap(m_cmp, n_slc_pad, transpose=False)
        jl = _lane_iota((TS, n_slc_pad))
        qblk = (P + _row_iota((TS, 1))) // SLC_BLOCK
        valid = (jl <= qblk) & (jl < n_slc)
        forced = ((jl == 0) | (jl == qblk) | (jl == qblk - 1)) & (jl < n_slc)
        for kv in range(NSA_KV_HEADS):
            kcmp = _dot(hidk[:, kv * LANES:(kv + 1) * LANES], w2k_ref[...]).astype(BF16)
            vcmp = _dot(hidv[:, kv * LANES:(kv + 1) * LANES], w2v_ref[...]).astype(BF16)
            qs = _stack_heads(q, kv).astype(BF16)
            p, l = _masked_softmax_parts(_dot_nt(qs, kcmp) * SCALE, mask)
            ocmp_ref[kv] = _dot(p.astype(BF16), vcmp) / l
            pn = p / l
            pg = pn[0:TS] + pn[TS:2 * TS] + pn[2 * TS:3 * TS] + pn[3 * TS:4 * TS]
            imp = sum(_dot(part, ov) for part in _split3(pg))
            score = jnp.where(forced, FORCED, jnp.where(valid, imp, NEG_INF))
            rank = jnp.zeros((TS, n_slc_pad), jnp.int32)
            for i in range(n_slc):
                si = score[:, i:i + 1]
                beats = (si > score) | ((si == score) & (i < jl))
                rank = rank + beats.astype(jnp.int32)
            sel_ref[kv] = jnp.where((rank < topk) & valid, 1.0, 0.0)


def _page_specs(shape):
    return [pl.BlockSpec((None,) + shape, lambda b, c, pt, k=k: (pt[b, c * PAGES_PER_STEP + k], 0, 0))
            for k in range(PAGES_PER_STEP)]


def _nsa_sample_cmp(page_table, q, ck, cv, cw):
    S, TS, _ = q.shape
    n_pages = page_table.shape[1]
    P = n_pages * PAGE_SIZE
    n_pool = ck.shape[0]
    gpp = PAGE_SIZE // CMP_STRIDE
    ck16 = ck.reshape(n_pool, gpp, CMP_STRIDE * KV_WIDTH)
    cv16 = cv.reshape(n_pool, gpp, CMP_STRIDE * KV_WIDTH)
    m_cmp = P // CMP_STRIDE
    n_slc = -(-(P + TS) // SLC_BLOCK)
    n_slc_pad = -(-n_slc // LANES) * LANES
    steps = n_pages // PAGES_PER_STEP
    pages = _page_specs((gpp, CMP_STRIDE * KV_WIDTH))
    wspecs = [pl.BlockSpec(w.shape, lambda b, c, pt, nd=w.ndim: (0,) * nd) for w in cw]
    grid_spec = pltpu.PrefetchScalarGridSpec(
        num_scalar_prefetch=1,
        grid=(S, steps),
        in_specs=[pl.BlockSpec((None, TS, NSA_WIDTH), lambda b, c, pt: (b, 0, 0))] + pages + pages + wspecs,
        out_specs=[pl.BlockSpec((None, NSA_KV_HEADS, Q_PER_KV * TS, LANES), lambda b, c, pt: (b, 0, 0, 0)),
                   pl.BlockSpec((None, NSA_KV_HEADS, TS, n_slc_pad), lambda b, c, pt: (b, 0, 0, 0))],
        scratch_shapes=[pltpu.VMEM((m_cmp, 4 * LANES), F32), pltpu.VMEM((m_cmp, 4 * LANES), F32)],
    )
    return pl.pallas_call(
        functools.partial(_nsa_sample_cmp_body, P=P, TS=TS),
        grid_spec=grid_spec,
        out_shape=[jax.ShapeDtypeStruct((S, NSA_KV_HEADS, Q_PER_KV * TS, LANES), F32),
                   jax.ShapeDtypeStruct((S, NSA_KV_HEADS, TS, n_slc_pad), F32)],
        compiler_params=_params(("parallel", "arbitrary")),
        name="nsa_sample_cmp",
    )(page_table, q, *([ck16] * PAGES_PER_STEP), *([cv16] * PAGES_PER_STEP), *cw)


def _nsa_sample_slc_body(pt_ref, q_ref, sel_ref, ocmp_ref, g_ref, ksn_ref, vsn_ref, kwn_ref, vwn_ref,
                         kwin_ref, vwin_ref, *refs, P, TS):
    n_pg = PAGES_PER_STEP
    kpages, vpages = refs[:n_pg], refs[n_pg:2 * n_pg]
    o_ref, kwout_ref, vwout_ref, m_ref, l_ref, acc_ref = refs[2 * n_pg:]
    c = pl.program_id(1)
    rows = Q_PER_KV * TS
    nk = n_pg * PAGE_SIZE
    wb = kwin_ref.shape[0]
    n_slc_pad = sel_ref.shape[-1]
    pad = LANES - TS
    q = q_ref[...]
    t = _row_iota((rows, 1)) % TS
    qpos = P + t

    @pl.when(c == 0)
    def _():
        m_ref[...] = jnp.full(m_ref.shape, NEG_INF, F32)
        l_ref[...] = jnp.zeros(l_ref.shape, F32)
        acc_ref[...] = jnp.zeros(acc_ref.shape, F32)
        kwout_ref[0:wb - TS, :] = kwin_ref[TS:wb, :]
        kwout_ref[wb - TS:wb, :] = kwn_ref[...]
        vwout_ref[0:wb - TS, :] = vwin_ref[TS:wb, :]
        vwout_ref[wb - TS:wb, :] = vwn_ref[...]

    kall = jnp.concatenate([r[...] for r in kpages], axis=0)
    vall = jnp.concatenate([r[...] for r in vpages], axis=0)
    kpos = c * nk + _lane_iota((1, nk))
    expand = jnp.where((kpos >> 6) == _row_iota((n_slc_pad, nk)), 1.0, 0.0).astype(BF16)
    for kv in range(NSA_KV_HEADS):
        qs = _stack_heads(q, kv).astype(BF16)
        sel4 = jnp.concatenate([sel_ref[kv]] * Q_PER_KV, axis=0).astype(BF16)
        mask = (_dot(sel4, expand) > 0.5) & (kpos <= qpos)
        kd = _dup_head(kall, kv).astype(BF16)
        _online_update(m_ref, l_ref, acc_ref, kv, _dot_nt(qs, kd) * SCALE, mask, _dup_head(vall, kv).astype(BF16))

    @pl.when(c == pl.num_programs(1) - 1)
    def _():
        zpad = jnp.zeros((pad, LANES), F32)
        i = _lane_iota((1, LANES))
        gs = jax.nn.sigmoid(g_ref[...])
        for kv in range(NSA_KV_HEADS):
            qs = _stack_heads(q, kv).astype(BF16)
            sel4 = jnp.concatenate([sel_ref[kv]] * Q_PER_KV, axis=0)
            jn = P // SLC_BLOCK
            mask_n = (sel4[:, jn:jn + 1] > 0.5) & (i <= t) & (i < TS)
            kn = _dup_head(jnp.concatenate([ksn_ref[...], zpad], axis=0), kv).astype(BF16)
            vn = _dup_head(jnp.concatenate([vsn_ref[...], zpad], axis=0), kv).astype(BF16)
            _online_update(m_ref, l_ref, acc_ref, kv, _dot_nt(qs, kn) * SCALE, mask_n, vn)
            o_slc = acc_ref[kv] / jnp.maximum(l_ref[kv], 1e-30)
            kw_all = jnp.concatenate([kwin_ref[...], kwn_ref[...], zpad], axis=0)
            vw_all = jnp.concatenate([vwin_ref[...], vwn_ref[...], zpad], axis=0)
            iw = _lane_iota((1, wb + LANES))
            rel = t + wb - iw
            mask_w = (rel >= 0) & (rel < WINDOW) & (iw < wb + TS)
            o_win = _attend(qs, _dup_head(kw_all, kv).astype(BF16), _dup_head(vw_all, kv).astype(BF16), mask_w)
            gcol = lambda comp: jnp.concatenate(
                [gs[:, 3 * (Q_PER_KV * kv + hl) + comp:3 * (Q_PER_KV * kv + hl) + comp + 1]
                 for hl in range(Q_PER_KV)], axis=0)
            o = gcol(0) * ocmp_ref[kv] + gcol(1) * o_slc + gcol(2) * o_win
            for pp in range(2):
                a = o[(2 * pp) * TS:(2 * pp + 1) * TS]
                b = o[(2 * pp + 1) * TS:(2 * pp + 2) * TS]
                tile = 2 * kv + pp
                o_ref[:, tile * LANES:(tile + 1) * LANES] = jnp.where(i < HEAD_DIM, a, b)


def _nsa_sample_slc(page_table, q, sel, ocmp, g, ksn, vsn, kwn, vwn, kwin, vwin, ck, cv):
    S, TS, _ = q.shape
    n_pages = page_table.shape[1]
    P = n_pages * PAGE_SIZE
    wb = kwin.shape[1]
    n_slc_pad = sel.shape[-1]
    rows = Q_PER_KV * TS
    steps = n_pages // PAGES_PER_STEP
    pages = _page_specs((PAGE_SIZE, KV_WIDTH))
    per_seq = lambda shape: pl.BlockSpec((None,) + shape, lambda b, c, pt, nd=len(shape): (b,) + (0,) * nd)
    grid_spec = pltpu.PrefetchScalarGridSpec(
        num_scalar_prefetch=1,
        grid=(S, steps),
        in_specs=[per_seq((TS, NSA_WIDTH)), per_seq((NSA_KV_HEADS, TS, n_slc_pad)),
                  per_seq((NSA_KV_HEADS, rows, LANES)), per_seq((TS, LANES)),
                  per_seq((TS, KV_WIDTH)), per_seq((TS, KV_WIDTH)), per_seq((TS, KV_WIDTH)), per_seq((TS, KV_WIDTH)),
                  per_seq((wb, KV_WIDTH)), per_seq((wb, KV_WIDTH))] + pages + pages,
        out_specs=[per_seq((TS, NSA_WIDTH)), per_seq((wb, KV_WIDTH)), per_seq((wb, KV_WIDTH))],
        scratch_shapes=[pltpu.VMEM((NSA_KV_HEADS, rows, LANES), F32)] * 3,
    )
    return pl.pallas_call(
        functools.partial(_nsa_sample_slc_body, P=P, TS=TS),
        grid_spec=grid_spec,
        out_shape=[jax.ShapeDtypeStruct((S, TS, NSA_WIDTH), F32),
                   jax.ShapeDtypeStruct((S, wb, KV_WIDTH), F32), jax.ShapeDtypeStruct((S, wb, KV_WIDTH), F32)],
        compiler_params=_params(("parallel", "arbitrary")),
        name="nsa_sample_slc",
    )(page_table, q, sel, ocmp, g, ksn, vsn, kwn, vwn, kwin, vwin,
      *([ck] * PAGES_PER_STEP), *([cv] * PAGES_PER_STEP))


def _merge_body(x_ref, glu_ref, o_ref, nw_ref, wg_ref, wup_ref, wnsa_ref, wout_ref, out_ref):
    x = x_ref[...]
    gate = jax.nn.sigmoid(_dot(_rms(x, nw_ref[...]).astype(BF16), wg_ref[...]))
    s5_out = _dot(glu_ref[...].astype(BF16), wup_ref[...])
    nsa_out = _dot(o_ref[...].astype(BF16), wnsa_ref[...])
    merged = gate[:, :D_MODEL] * s5_out + gate[:, D_MODEL:] * nsa_out
    out_ref[...] = x + _dot(merged.astype(BF16), wout_ref[...])


def _merge(x, glu, o, nw, wg, wup, wnsa, wout, tt):
    B, T, _ = x.shape
    row = lambda w: pl.BlockSpec((None, tt, w), lambda b, i: (b, i, 0))
    return pl.pallas_call(
        _merge_body,
        grid=(B, T // tt),
        in_specs=[row(D_MODEL), row(S5_WIDTH), row(NSA_WIDTH), _const_spec(nw.shape), _const_spec(wg.shape),
                  _const_spec(wup.shape), _const_spec(wnsa.shape), _const_spec(wout.shape)],
        out_specs=row(D_MODEL),
        out_shape=jax.ShapeDtypeStruct((B, T, D_MODEL), F32),
        compiler_params=_params(("parallel", "parallel")),
        name="merge",
    )(x, glu, o, nw, wg, wup, wnsa, wout)


def _ffn_body(x_ref, nw_ref, w1_ref, w2_ref, out_ref):
    x = x_ref[...]
    a = jnp.maximum(_dot(_rms(x, nw_ref[...]).astype(BF16), w1_ref[...]), 0.0)
    out_ref[...] = x + _dot((a * a).astype(BF16), w2_ref[...])


def _ffn(x, nw, w1, w2, tt):
    B, T, _ = x.shape
    row = pl.BlockSpec((None, tt, D_MODEL), lambda b, i: (b, i, 0))
    return pl.pallas_call(
        _ffn_body,
        grid=(B, T // tt),
        in_specs=[row, _const_spec(nw.shape), _const_spec(w1.shape), _const_spec(w2.shape)],
        out_specs=row,
        out_shape=jax.ShapeDtypeStruct((B, T, D_MODEL), F32),
        compiler_params=_params(("parallel", "parallel")),
        name="ffn",
    )(x, nw, w1, w2)


def _rope_tables(pos):
    half = HEAD_DIM // 2
    inv = ROPE_THETA ** (-jnp.arange(half, dtype=F32) / half)
    ang = pos.astype(F32)[:, None] * inv[None, :]
    cos, sin = jnp.cos(ang), jnp.sin(ang)
    return (jnp.concatenate([cos, cos, cos, cos], axis=-1),
            jnp.concatenate([-sin, sin, -sin, sin], axis=-1))


def _block_diag(w):
    G, a, b = w.shape
    return jnp.einsum('gab,gk->gakb', w, jnp.eye(G, dtype=w.dtype)).reshape(G * a, G * b)


def _cmp_weights(pe, w1, w2):
    eye = jnp.eye(NSA_KV_HEADS, dtype=F32)
    half = CMP_LEN // 2
    big = lambda w: jnp.einsum('rdf,hk->rhdkf', w, eye).reshape(half * KV_WIDTH, NSA_KV_HEADS * CMP_HIDDEN)
    w1b = jnp.concatenate([big(w1[:half]), big(w1[half:])], axis=1).astype(BF16)
    w2d = jnp.concatenate([w2, w2], axis=1).astype(BF16)
    flat = lambda p: jnp.broadcast_to(p[:, None, :], (half, NSA_KV_HEADS, HEAD_DIM)).reshape(1, half * KV_WIDTH)
    pe8 = jnp.concatenate([flat(pe[:half]), flat(pe[half:]), jnp.zeros((SUBLANES - 2, half * KV_WIDTH), F32)], axis=0)
    return w1b, w2d, pe8


def kernel(x_prompt, x_sample, cache_k_cmp, cache_v_cmp, cache_k_slc, cache_v_slc, state_k_win, state_v_win, state_s5_re, state_s5_im, page_table, norm_mix_w, w_in, s5_lam_re, s5_lam_im, s5_log_dt, s5_b_re, s5_b_im, s5_c_re, s5_c_im, s5_d, s5_w_glu, s5_w_up, q_norm_w, k_norm_w, cmp_pe_k, cmp_wk1, cmp_wk2, cmp_pe_v, cmp_wv1, cmp_wv2, nsa_w_up, w_out, norm_ffn_w, w_ffn1, w_ffn2):
    B, T, _ = x_prompt.shape
    S, TS, _ = x_sample.shape
    n_pages = page_table.shape[1]
    P = n_pages * PAGE_SIZE
    n_pool = cache_k_cmp.shape[1]
    assert norm_mix_w.shape[0] == 1 and B % SUBLANES == 0 and S % SUBLANES == 0
    assert TS < CMP_STRIDE and P % SLC_BLOCK == 0 and n_pages % PAGES_PER_STEP == 0

    w = w_in[0]
    cols = lambda a, n: w[:, a:a + n]
    w_a = jnp.concatenate(
        [cols(0, 512), cols(512, 512), cols(1024, 128), cols(1280, 128), cols(1536, 128),
         cols(1152, 128), cols(1408, 128), cols(1664, 128), cols(1792, N_GATE),
         jnp.zeros((D_MODEL, W_A_COLS - 1792 - N_GATE), F32)], axis=1).astype(BF16)
    w_g = w[:, OFF_GMIX:].astype(BF16)
    nw_mix = norm_mix_w[0][None, :]
    qkw = jnp.concatenate([jnp.tile(q_norm_w[0], NSA_HEADS)]
                          + [jnp.tile(k_norm_w[0, i], NSA_KV_HEADS) for i in range(3)])[None, :]
    li = np.arange(LANES)
    seg = jnp.asarray((li[:, None] // HEAD_DIM) == (li[None, :] // HEAD_DIM), BF16)
    lre, lim = s5_lam_re[0].reshape(1, S5_LANES), s5_lam_im[0].reshape(1, S5_LANES)
    ldt = jnp.repeat(s5_log_dt[0], S5_STATE)[None, :]
    bre = _block_diag(jnp.swapaxes(s5_b_re[0], 1, 2))
    bim = _block_diag(jnp.swapaxes(s5_b_im[0], 1, 2))
    cre = _block_diag(jnp.swapaxes(s5_c_re[0], 1, 2)).astype(BF16)
    cim = _block_diag(jnp.swapaxes(s5_c_im[0], 1, 2)).astype(BF16)
    s5d = s5_d[0][None, :]
    wglu = s5_w_glu[0].astype(BF16)
    wup = s5_w_up[0].astype(BF16)
    cw = _cmp_weights(cmp_pe_k[0], cmp_wk1[0], cmp_wk2[0]) + _cmp_weights(cmp_pe_v[0], cmp_wv1[0], cmp_wv2[0])
    wnsa = nsa_w_up[0].astype(BF16)
    wout = w_out[0].astype(BF16)
    nw_ffn = norm_ffn_w[0][None, :]
    w1 = w_ffn1[0].astype(BF16)
    w2 = w_ffn2[0].astype(BF16)

    def trunk(x, glu, o, tt):
        return _ffn(_merge(x, glu, o, nw_mix, w_g, wup, wnsa, wout, tt), nw_ffn, w1, w2, tt)

    cos_p, sin_p = _rope_tables(jnp.arange(T, dtype=jnp.int32))
    u, q, kc, ks, kw, vc, vs, vw, g = _inproj(x_prompt, cos_p, sin_p, nw_mix, w_a, qkw, seg, 512)
    zeros_st = jnp.zeros((B, S5_LANES), F32)
    glu_tm, sre_p, sim_p = _s5(jnp.swapaxes(u, 0, 1), zeros_st, zeros_st, lre, lim, ldt, bre, bim, cre, cim, s5d, wglu, 64)
    o_p = _nsa_prompt(q, kc, vc, ks, vs, kw, vw, g, cw, 256)
    y_prompt = trunk(x_prompt, jnp.swapaxes(glu_tm, 0, 1), o_p, 512)

    n_s = S * TS
    cos_s, sin_s = _rope_tables(P + jnp.arange(TS, dtype=jnp.int32))
    cos_s, sin_s = jnp.tile(cos_s, (S, 1)), jnp.tile(sin_s, (S, 1))
    xs = x_sample.reshape(1, n_s, D_MODEL)
    us, qs, kcs, kss, kws, vcs, vss, vws, gs = _inproj(xs, cos_s, sin_s, nw_mix, w_a, qkw, seg, n_s)
    seq = lambda a: a.reshape(S, TS, a.shape[-1])
    u_tm = jnp.swapaxes(seq(us), 0, 1)
    glu_s_tm, sre_s, sim_s = _s5(u_tm, state_s5_re[0].reshape(S, S5_LANES), state_s5_im[0].reshape(S, S5_LANES),
                                 lre, lim, ldt, bre, bim, cre, cim, s5d, wglu, TS)
    pool = lambda c: c[0].reshape(n_pool, PAGE_SIZE, KV_WIDTH)
    ocmp_s, sel_s = _nsa_sample_cmp(page_table, seq(qs), pool(cache_k_cmp), pool(cache_v_cmp), cw)
    win = lambda s: s[0].reshape(S, s.shape[2], KV_WIDTH)
    o_s, kwin_s, vwin_s = _nsa_sample_slc(page_table, seq(qs), sel_s, ocmp_s, seq(gs), seq(kss), seq(vss), seq(kws),
                                          seq(vws), win(state_k_win), win(state_v_win),
                                          pool(cache_k_slc), pool(cache_v_slc))
    glu_s = jnp.swapaxes(glu_s_tm, 0, 1).reshape(1, n_s, S5_WIDTH)
    y_sample = trunk(xs, glu_s, o_s.reshape(1, n_s, NSA_WIDTH), n_s).reshape(S, TS, D_MODEL)

    heads_p = lambda a: a.reshape(1, B, a.shape[1], NSA_KV_HEADS, HEAD_DIM)
    heads_s = lambda a: a.reshape(1, S, -1, NSA_KV_HEADS, HEAD_DIM)
    keep = min(WINDOW, T)
    st_p = lambda a: a.reshape(1, B, S5_GROUPS, S5_STATE)
    st_s = lambda a: a.reshape(1, S, S5_GROUPS, S5_STATE)
    return (y_prompt, y_sample,
            heads_p(kc), heads_s(kcs), heads_p(vc), heads_s(vcs),
            heads_p(ks), heads_s(kss), heads_p(vs), heads_s(vss),
            heads_p(kw[:, T - keep:]), heads_s(kwin_s), heads_p(vw[:, T - keep:]), heads_s(vwin_s),
            st_p(sre_p), st_s(sre_s), st_p(sim_p), st_s(sim_s))
```

```python
import functools

import numpy as np
import jax
import jax.numpy as jnp
from jax import lax
from jax.experimental import pallas as pl
from jax.experimental.pallas import tpu as pltpu

F32 = jnp.float32
BF16 = jnp.bfloat16

D_MODEL = 1024
PAGE_SIZE = 128
S5_WIDTH = 512
S5_GROUPS = 32
S5_STATE = 64
S5_LANES = S5_GROUPS * S5_STATE
HEAD_DIM = 64
NSA_HEADS = 8
NSA_KV_HEADS = 2
Q_PER_KV = 4
NSA_WIDTH = 512
KV_WIDTH = 128
CMP_LEN = 32
CMP_STRIDE = 16
CMP_HIDDEN = 128
SLC_BLOCK = 64
SLC_TOPK = 16
WINDOW = 512
ROPE_THETA = 10000.0
EPS = 1e-6
NEG_INF = -1e30
FORCED = 1e9
SCALE = HEAD_DIM ** -0.5
N_GATE = 3 * NSA_HEADS
OFF_GMIX = 2 * 512 + 6 * KV_WIDTH + N_GATE
W_A_COLS = 1920

LANES = 128
SUBLANES = 8
VMEM_LIMIT = 56 * 1024 * 1024


def _params(sem):
    return pltpu.CompilerParams(dimension_semantics=sem, vmem_limit_bytes=VMEM_LIMIT)


def _const_spec(shape):
    nd = len(shape)
    return pl.BlockSpec(shape, lambda *_: (0,) * nd)


def _dot(a, b):
    return jnp.dot(a, b, preferred_element_type=F32)


def _dot_nt(a, b):
    return lax.dot_general(a, b, (((1,), (1,)), ((), ())), preferred_element_type=F32)


def _rms(x, w):
    r = lax.rsqrt(jnp.mean(x * x, axis=-1, keepdims=True) + EPS)
    return x * r * w


def _lane_iota(shape):
    return lax.broadcasted_iota(jnp.int32, shape, len(shape) - 1)


def _row_iota(shape):
    return lax.broadcasted_iota(jnp.int32, shape, len(shape) - 2)


def _split3(x):
    hi = x.astype(BF16)
    r1 = x - hi.astype(F32)
    mid = r1.astype(BF16)
    lo = (r1 - mid.astype(F32)).astype(BF16)
    return hi, mid, lo


def _inproj_body(x_ref, nw_ref, w_ref, qkw_ref, seg_ref, cos_ref, sin_ref,
                 u_ref, q_ref, kc_ref, ks_ref, kw_ref, vc_ref, vs_ref, vw_ref, g_ref):
    xn = _rms(x_ref[...], nw_ref[...]).astype(BF16)
    z = _dot(xn, w_ref[...])
    u_ref[...] = z[:, :512]
    cos = cos_ref[...]
    sin = sin_ref[...]
    seg = seg_ref[...]
    first_half = (_lane_iota((1, LANES)) % HEAD_DIM) < (HEAD_DIM // 2)

    def norm_rope(t, w):
        ss = _dot((t * t).astype(BF16), seg)
        t = t * lax.rsqrt(ss * (1.0 / HEAD_DIM) + EPS) * w
        rot = jnp.where(first_half, pltpu.roll(t, LANES - HEAD_DIM // 2, 1), pltpu.roll(t, HEAD_DIM // 2, 1))
        return t * cos + rot * sin

    for j in range(4):
        q_ref[:, j * LANES:(j + 1) * LANES] = norm_rope(
            z[:, 512 + j * LANES:512 + (j + 1) * LANES], qkw_ref[:, j * LANES:(j + 1) * LANES])
    for j, r in enumerate((kc_ref, ks_ref, kw_ref)):
        r[...] = norm_rope(z[:, 1024 + j * LANES:1024 + (j + 1) * LANES],
                           qkw_ref[:, 512 + j * LANES:512 + (j + 1) * LANES])
    vc_ref[...] = z[:, 1408:1536]
    vs_ref[...] = z[:, 1536:1664]
    vw_ref[...] = z[:, 1664:1792]
    g_ref[...] = z[:, 1792:1920]


def _inproj(x, cos_t, sin_t, nw, w_a, qkw, seg, tt):
    B, T, _ = x.shape
    row = lambda w: pl.BlockSpec((None, tt, w), lambda b, i: (b, i, 0))
    tab = pl.BlockSpec((tt, LANES), lambda b, i: (i, 0))
    kv = jax.ShapeDtypeStruct((B, T, KV_WIDTH), F32)
    out_shape = [jax.ShapeDtypeStruct((B, T, 512), F32), jax.ShapeDtypeStruct((B, T, 512), F32)] + [kv] * 7
    return pl.pallas_call(
        _inproj_body,
        grid=(B, T // tt),
        in_specs=[row(D_MODEL), _const_spec((1, D_MODEL)), _const_spec((D_MODEL, W_A_COLS)),
                  _const_spec((1, 896)), _const_spec((LANES, LANES)), tab, tab],
        out_specs=[row(512), row(512)] + [row(KV_WIDTH)] * 7,
        out_shape=out_shape,
        compiler_params=_params(("parallel", "parallel")),
        name="inproj",
    )(x, nw, w_a, qkw, seg, cos_t, sin_t)


def _s5_body(u_ref, sre0_ref, sim0_ref, lre_ref, lim_ref, ldt_ref, bre_ref, bim_ref, cre_ref, cim_ref,
             d_ref, wglu_ref, out_ref, sre_ref, sim_ref,
             bbar_ref, a_ref, st_ref, xr_ref, xi_ref, *, tc):
    c = pl.program_id(1)
    rows = tc * SUBLANES

    @pl.when(c == 0)
    def _():
        lr, li = lre_ref[...], lim_ref[...]
        dt = jnp.exp(ldt_ref[...])
        mag = jnp.exp(lr * dt)
        ab_re, ab_im = mag * jnp.cos(li * dt), mag * jnp.sin(li * dt)
        den = lr * lr + li * li
        f_re = ((ab_re - 1.0) * lr + ab_im * li) / den
        f_im = (ab_im * lr - (ab_re - 1.0) * li) / den
        bre, bim = bre_ref[...], bim_ref[...]
        bbar_ref[:, :S5_LANES] = (f_re * bre - f_im * bim).astype(BF16)
        bbar_ref[:, S5_LANES:] = (f_re * bim + f_im * bre).astype(BF16)
        a_ref[0] = jnp.broadcast_to(ab_re, (SUBLANES, S5_LANES))
        a_ref[1] = jnp.broadcast_to(ab_im, (SUBLANES, S5_LANES))
        st_ref[0] = sre0_ref[...]
        st_ref[1] = sim0_ref[...]

    u = u_ref[...].reshape(rows, S5_WIDTH)
    ub = u.astype(BF16)
    xr_ref[...] = _dot(ub, bbar_ref[:, :S5_LANES])
    xi_ref[...] = _dot(ub, bbar_ref[:, S5_LANES:])

    slab = 512
    for lc in range(S5_LANES // slab):
        sl = slice(lc * slab, (lc + 1) * slab)
        ar, ai = a_ref[0, :, sl], a_ref[1, :, sl]

        def step(t, carry, sl=sl, ar=ar, ai=ai):
            sr, si = carry
            r0 = pl.multiple_of(t * SUBLANES, SUBLANES)
            nr = ar * sr - ai * si + xr_ref[pl.ds(r0, SUBLANES), sl]
            ni = ar * si + ai * sr + xi_ref[pl.ds(r0, SUBLANES), sl]
            xr_ref[pl.ds(r0, SUBLANES), sl] = nr
            xi_ref[pl.ds(r0, SUBLANES), sl] = ni
            return nr, ni

        sr, si = lax.fori_loop(0, tc, step, (st_ref[0, :, sl], st_ref[1, :, sl]), unroll=min(tc, 8))
        st_ref[0, :, sl] = sr
        st_ref[1, :, sl] = si

    y = (_dot(xr_ref[...].astype(BF16), cre_ref[...]) - _dot(xi_ref[...].astype(BF16), cim_ref[...])
         + d_ref[...] * u)
    yg = jax.nn.gelu(y)
    glu = yg * jax.nn.sigmoid(_dot(yg.astype(BF16), wglu_ref[...]))
    out_ref[...] = glu.reshape(tc, SUBLANES, S5_WIDTH)

    @pl.when(c == pl.num_programs(1) - 1)
    def _():
        sre_ref[...] = st_ref[0]
        sim_ref[...] = st_ref[1]


def _s5(u_tm, sre0, sim0, lre, lim, ldt, bre, bim, cre, cim, d, wglu, tc):
    T, B, _ = u_tm.shape
    st_spec = pl.BlockSpec((SUBLANES, S5_LANES), lambda b, c: (b, 0))
    blk = pl.BlockSpec((tc, SUBLANES, S5_WIDTH), lambda b, c: (c, b, 0))
    st = jax.ShapeDtypeStruct((B, S5_LANES), F32)
    return pl.pallas_call(
        functools.partial(_s5_body, tc=tc),
        grid=(B // SUBLANES, T // tc),
        in_specs=[blk, st_spec, st_spec,
                  _const_spec((1, S5_LANES)), _const_spec((1, S5_LANES)), _const_spec((1, S5_LANES)),
                  _const_spec((S5_WIDTH, S5_LANES)), _const_spec((S5_WIDTH, S5_LANES)),
                  _const_spec((S5_LANES, S5_WIDTH)), _const_spec((S5_LANES, S5_WIDTH)),
                  _const_spec((1, S5_WIDTH)), _const_spec((S5_WIDTH, S5_WIDTH))],
        out_specs=[blk, st_spec, st_spec],
        out_shape=[jax.ShapeDtypeStruct((T, B, S5_WIDTH), F32), st, st],
        scratch_shapes=[pltpu.VMEM((S5_WIDTH, 2 * S5_LANES), BF16),
                        pltpu.VMEM((2, SUBLANES, S5_LANES), F32),
                        pltpu.VMEM((2, SUBLANES, S5_LANES), F32),
                        pltpu.VMEM((tc * SUBLANES, S5_LANES), F32),
                        pltpu.VMEM((tc * SUBLANES, S5_LANES), F32)],
        compiler_params=_params(("parallel", "arbitrary")),
        name="s5",
    )(u_tm, sre0, sim0, lre, lim, ldt, bre, bim, cre, cim, d, wglu)


def _dup_head(x, kv):
    own = (_lane_iota((1, LANES)) >> 6) == kv
    return jnp.where(own, x, pltpu.roll(x, HEAD_DIM, 1))


def _head_lanes(x, half):
    return jnp.where((_lane_iota((1, LANES)) >> 6) == half, x, 0.0)


def _masked_softmax_parts(s, mask):
    s = jnp.where(mask, s, NEG_INF)
    m = jnp.max(s, axis=-1, keepdims=True)
    p = jnp.where(mask, jnp.exp(s - m), 0.0)
    l = jnp.maximum(jnp.sum(p, axis=-1, keepdims=True), 1e-30)
    return p, l


def _cmp_hidden(h, w1_ref, pe_ref):
    m = h.shape[0]
    pe = pe_ref[...].astype(BF16)
    bias = _dot(pe, w1_ref[:, :256])[0:1] + _dot(pe, w1_ref[:, 256:])[1:2]
    return jax.nn.gelu(h[:, :256] + pltpu.roll(h[:, 256:], m - 1, 0) + bias)


def _overlap(n_rows, n_cols, transpose):
    shape = (n_cols, n_rows) if transpose else (n_rows, n_cols)
    n = _row_iota(shape) if not transpose else _lane_iota(shape)
    j = _lane_iota(shape) if not transpose else _row_iota(shape)
    lo = jnp.maximum(n * CMP_STRIDE, j * SLC_BLOCK)
    hi = jnp.minimum(n * CMP_STRIDE + CMP_LEN, j * SLC_BLOCK + SLC_BLOCK)
    return (jnp.maximum(hi - lo, 0).astype(F32) * (1.0 / CMP_LEN)).astype(BF16)


def _ones_row_rows(n):
    return jnp.where(_row_iota((HEAD_DIM, n)) == 0, 1.0, 0.0)


def _online_update_t(m_ref, acc_ref, h, s_t, v_t):
    m_old = m_ref[h]
    m_new = jnp.maximum(m_old, jnp.max(s_t, axis=0, keepdims=True))
    p_t = jnp.exp(s_t - m_new[0:1])
    alpha = jnp.exp(m_old - m_new)
    acc_ref[h] = alpha[0:1] * acc_ref[h] + _dot(v_t, p_t.astype(BF16))
    m_ref[h] = m_new


def _softmax_result_t(m_ref, acc_ref, h):
    acc = acc_ref[h]
    out = acc[0:HEAD_DIM] / jnp.maximum(acc[HEAD_DIM:HEAD_DIM + 1], 1e-30)
    return jnp.where(m_ref[h][0:1] > 0.5 * NEG_INF, out, 0.0)


def _nsa_prompt_body(q_ref, kc16_ref, vc16_ref, ks_ref, kw_ref, vs_ref, vw_ref, g_ref,
                     w1k_ref, w2k_ref, pek_ref, w1v_ref, w2v_ref, pev_ref,
                     o_ref,
                     ksd_ref, kwd_ref, vst_ref, vwt_ref, kcmp_ref, vcmpt_ref,
                     qm_ref, ms_ref, accs_ref, mw_ref, accw_ref, *, T, TQ):
    kv = pl.program_id(1)
    qi = pl.program_id(2)
    m_cmp = T // CMP_STRIDE
    n_cmp = m_cmp - 1
    n_slc = T // SLC_BLOCK
    topk = min(SLC_TOPK, n_slc)
    n_kt = T // TQ

    @pl.when(qi == 0)
    def _():
        ksd_ref[...] = _dup_head(ks_ref[...], kv).astype(BF16)
        kwd_ref[...] = _dup_head(kw_ref[...], kv).astype(BF16)
        ones = _ones_row_rows(TQ).astype(BF16)
        for kt in range(n_kt):
            for src, dst in ((vs_ref, vst_ref), (vw_ref, vwt_ref)):
                vt = src[kt * TQ:(kt + 1) * TQ, :].T
                dst[kt, 0:HEAD_DIM, :] = jnp.where(kv == 0, vt[0:HEAD_DIM], vt[HEAD_DIM:]).astype(BF16)
                dst[kt, HEAD_DIM:, :] = ones

        def compressed(x16_ref, w1_ref, pe_ref, w2_ref):
            hid = _cmp_hidden(_dot(x16_ref[...].astype(BF16), w1_ref[...]), w1_ref, pe_ref)
            hid = jnp.where(kv == 0, hid[:, :LANES], hid[:, LANES:])
            return _dot(hid.astype(BF16), w2_ref[...])

        kcmp_ref[...] = compressed(kc16_ref, w1k_ref, pek_ref, w2k_ref).astype(BF16)
        vcmp_t = compressed(vc16_ref, w1v_ref, pev_ref, w2v_ref).T
        vcmpt_ref[0:HEAD_DIM, :] = vcmp_t[0:HEAD_DIM].astype(BF16)
        vcmpt_ref[HEAD_DIM:, :] = jnp.zeros((HEAD_DIM, m_cmp), BF16)

    r0 = pl.multiple_of(qi * TQ, TQ)
    for pp in range(2):
        qt = q_ref[pl.ds(r0, TQ), pp * LANES:(pp + 1) * LANES] * SCALE
        for e in range(2):
            qm_ref[2 * pp + e] = _head_lanes(qt, e).astype(BF16)
    qpos = r0 + _lane_iota((1, TQ))

    n = _row_iota((m_cmp, TQ))
    mask_c = (n * CMP_STRIDE + CMP_LEN - 1 <= qpos) & (n < n_cmp)
    pg = jnp.zeros((m_cmp, TQ), F32)
    o_cmp = []
    for h in range(Q_PER_KV):
        s_t = jnp.where(mask_c, _dot_nt(kcmp_ref[...], qm_ref[h]), NEG_INF)
        p_t = jnp.where(mask_c, jnp.exp(s_t - jnp.max(s_t, axis=0, keepdims=True)), 0.0)
        inv_l = 1.0 / jnp.maximum(jnp.sum(p_t, axis=0, keepdims=True), 1e-30)
        pg = pg + p_t * inv_l
        o_cmp.append(_dot(vcmpt_ref[...], p_t.astype(BF16))[0:HEAD_DIM] * inv_l)

    ovt = _overlap(m_cmp, n_slc, transpose=True)
    imp = sum(_dot(ovt, part) for part in _split3(pg))
    j = _row_iota((n_slc, TQ))
    qblk = qpos // SLC_BLOCK
    valid = j <= qblk
    forced = (j == 0) | (j == qblk) | (j == qblk - 1)
    score = jnp.where(forced, FORCED, jnp.where(valid, imp, NEG_INF))
    rank = jnp.zeros((n_slc, TQ), jnp.int32)
    for i in range(n_slc):
        si = score[i:i + 1, :]
        beats = (si > score) | ((si == score) & (i < j))
        rank = rank + beats.astype(jnp.int32)
    sel = jnp.concatenate([jnp.where((rank < topk) & valid, 1.0, 0.0),
                           jnp.zeros((LANES - n_slc, TQ), F32)], axis=0).astype(BF16)

    for m_r, acc_r in ((ms_ref, accs_ref), (mw_ref, accw_ref)):
        m_r[...] = jnp.full(m_r.shape, NEG_INF, F32)
        acc_r[...] = jnp.zeros(acc_r.shape, F32)

    def slc_tile(kb, carry):
        k0 = pl.multiple_of(kb * TQ, TQ)
        kpos = k0 + _row_iota((TQ, 1))
        expand = jnp.where(((k0 + _row_iota((TQ, LANES))) >> 6) == _lane_iota((TQ, LANES)), 1.0, 0.0).astype(BF16)
        bias = jnp.where((_dot(expand, sel) > 0.5) & (kpos <= qpos), 0.0, NEG_INF)
        k = ksd_ref[pl.ds(k0, TQ), :]
        for h in range(Q_PER_KV):
            _online_update_t(ms_ref, accs_ref, h, _dot_nt(k, qm_ref[h]) + bias, vst_ref[kb])
        return carry

    lax.fori_loop(0, qi + 1, slc_tile, 0)

    def win_tile(kb, carry):
        k0 = pl.multiple_of(kb * TQ, TQ)
        rel = qpos - (k0 + _row_iota((TQ, 1)))
        bias = jnp.where((rel >= 0) & (rel < WINDOW), 0.0, NEG_INF)
        k = kwd_ref[pl.ds(k0, TQ), :]
        for h in range(Q_PER_KV):
            _online_update_t(mw_ref, accw_ref, h, _dot_nt(k, qm_ref[h]) + bias, vwt_ref[kb])
        return carry

    lax.fori_loop(jnp.maximum(qi - WINDOW // TQ, 0), qi + 1, win_tile, 0)

    g = g_ref[pl.ds(r0, TQ), :]
    gs_t = jax.nn.sigmoid(jnp.where(kv == 0, g, pltpu.roll(g, LANES - 3 * Q_PER_KV, 1))).T
    for pp in range(2):
        halves = []
        for e in range(2):
            h = 2 * pp + e
            halves.append(gs_t[3 * h:3 * h + 1] * o_cmp[h]
                          + gs_t[3 * h + 1:3 * h + 2] * _softmax_result_t(ms_ref, accs_ref, h)
                          + gs_t[3 * h + 2:3 * h + 3] * _softmax_result_t(mw_ref, accw_ref, h))
        o_ref[pl.ds(r0, TQ), pp * LANES:(pp + 1) * LANES] = jnp.concatenate(halves, axis=0).T


def _nsa_prompt(q, kc, vc, ks, vs, kw, vw, g, cw, tq):
    B, T, _ = q.shape
    m_cmp = T // CMP_STRIDE
    kc16 = kc.reshape(B, m_cmp, CMP_STRIDE * KV_WIDTH)
    vc16 = vc.reshape(B, m_cmp, CMP_STRIDE * KV_WIDTH)
    full = lambda w: pl.BlockSpec((None, T, w), lambda b, h, i: (b, 0, 0))
    x16 = pl.BlockSpec((None, m_cmp, CMP_STRIDE * KV_WIDTH), lambda b, h, i: (b, 0, 0))
    qo = pl.BlockSpec((None, T, 2 * LANES), lambda b, h, i: (b, 0, h))
    wspecs = [_const_spec(w.shape) for w in cw]
    k_dup = pltpu.VMEM((T, LANES), BF16)
    v_t = pltpu.VMEM((T // tq, LANES, tq), BF16)
    run_max = pltpu.VMEM((Q_PER_KV, SUBLANES, tq), F32)
    run_acc = pltpu.VMEM((Q_PER_KV, LANES, tq), F32)
    return pl.pallas_call(
        functools.partial(_nsa_prompt_body, T=T, TQ=tq),
        grid=(B, NSA_KV_HEADS, T // tq),
        in_specs=[qo, x16, x16, full(KV_WIDTH), full(KV_WIDTH), full(KV_WIDTH), full(KV_WIDTH), full(KV_WIDTH)] + wspecs,
        out_specs=qo,
        out_shape=jax.ShapeDtypeStruct((B, T, NSA_WIDTH), F32),
        scratch_shapes=[k_dup, k_dup, v_t, v_t,
                        pltpu.VMEM((m_cmp, LANES), BF16), pltpu.VMEM((LANES, m_cmp), BF16),
                        pltpu.VMEM((Q_PER_KV, tq, LANES), BF16), run_max, run_acc, run_max, run_acc],
        compiler_params=_params(("parallel", "arbitrary", "arbitrary")),
        name="nsa_prompt",
    )(q, kc16, vc16, ks, kw, vs, vw, g, *cw)


PAGES_PER_STEP = 16
BLOCKS_PER_STEP = PAGES_PER_STEP * PAGE_SIZE // SLC_BLOCK


def _page_specs():
    return [pl.BlockSpec((None, KV_WIDTH, PAGE_SIZE), lambda b, c, pt, k=k: (pt[b, c * PAGES_PER_STEP + k], 0, 0))
            for k in range(PAGES_PER_STEP)]


def _stack_heads(q, kv, lane_half):
    rows = []
    for hl in range(Q_PER_KV):
        tile = 2 * kv + hl // 2
        x = _head_lanes(q[:, tile * LANES:(tile + 1) * LANES], hl % 2)
        rows.append(x if lane_half(hl) == hl % 2 else pltpu.roll(x, HEAD_DIM, 1))
    return jnp.concatenate(rows, axis=0)


def _nsa_sample_cmp_body(pt_ref, q_ref, *refs, P, TS):
    n_pg = PAGES_PER_STEP
    kpages, vpages = refs[:n_pg], refs[n_pg:2 * n_pg]
    w1k_ref, w2k_ref, pek_ref, w1v_ref, w2v_ref, pev_ref = refs[2 * n_pg:2 * n_pg + 6]
    ocmp_ref, sel_ref, hk_ref, hv_ref, xk_ref, xv_ref = refs[2 * n_pg + 6:]
    c = pl.program_id(1)
    groups = n_pg * PAGE_SIZE // CMP_STRIDE
    m_cmp = P // CMP_STRIDE
    n_cmp = (P + TS - CMP_LEN) // CMP_STRIDE + 1
    n_slc = -(-(P + TS) // SLC_BLOCK)
    n_slc_pad = -(-n_slc // LANES) * LANES
    topk = min(SLC_TOPK, n_slc)

    r0 = pl.multiple_of(c * groups, groups)
    for pages, x_ref, w1_ref, h_ref in ((kpages, xk_ref, w1k_ref, hk_ref), (vpages, xv_ref, w1v_ref, hv_ref)):
        for k, page in enumerate(pages):
            x_ref[k * PAGE_SIZE:(k + 1) * PAGE_SIZE, :] = page[...].T
        h = jnp.zeros((groups, 4 * LANES), F32)
        for i in range(CMP_STRIDE // 2):
            pair = jnp.concatenate([x_ref[pl.ds(2 * i, groups, stride=CMP_STRIDE), :],
                                    x_ref[pl.ds(2 * i + 1, groups, stride=CMP_STRIDE), :]], axis=1)
            h = h + _dot(pair.astype(BF16), w1_ref[2 * i * KV_WIDTH:(2 * i + 2) * KV_WIDTH, :])
        h_ref[pl.ds(r0, groups), :] = h

    @pl.when(c == pl.num_programs(1) - 1)
    def _():
        hidk = _cmp_hidden(hk_ref[...], w1k_ref, pek_ref).astype(BF16)
        hidv = _cmp_hidden(hv_ref[...], w1v_ref, pev_ref).astype(BF16)
        q = q_ref[...] * SCALE
        rows = Q_PER_KV * TS
        t = _row_iota((rows, 1)) % TS
        qpos = P + t
        n = _lane_iota((rows, m_cmp))
        mask = (n * CMP_STRIDE + CMP_LEN - 1 <= qpos) & (n < n_cmp)
        ov = _overlap(m_cmp, n_slc_pad, transpose=False)
        jl = _lane_iota((TS, n_slc_pad))
        qblk = (P + _row_iota((TS, 1))) // SLC_BLOCK
        valid = (jl <= qblk) & (jl < n_slc)
        forced = ((jl == 0) | (jl == qblk) | (jl == qblk - 1)) & (jl < n_slc)
        first_lanes = _lane_iota((1, LANES)) < BLOCKS_PER_STEP
        for kv in range(NSA_KV_HEADS):
            kcmp = _dot(hidk[:, kv * LANES:(kv + 1) * LANES], w2k_ref[...]).astype(BF16)
            vcmp = _dot(hidv[:, kv * LANES:(kv + 1) * LANES], w2v_ref[...]).astype(BF16)
            qs = _stack_heads(q, kv, lambda hl: hl % 2).astype(BF16)
            p, l = _masked_softmax_parts(_dot_nt(qs, kcmp), mask)
            ocmp_ref[kv] = _dot(p.astype(BF16), vcmp) / l
            pn = p / l
            pg = pn[0:TS] + pn[TS:2 * TS] + pn[2 * TS:3 * TS] + pn[3 * TS:4 * TS]
            imp = sum(_dot(part, ov) for part in _split3(pg))
            score = jnp.where(forced, FORCED, jnp.where(valid, imp, NEG_INF))
            rank = jnp.zeros((TS, n_slc_pad), jnp.int32)
            for i in range(n_slc):
                si = score[:, i:i + 1]
                beats = (si > score) | ((si == score) & (i < jl))
                rank = rank + beats.astype(jnp.int32)
            sel = jnp.where((rank < topk) & valid, 1.0, 0.0)
            for ch in range(sel_ref.shape[1]):
                b0 = ch * BLOCKS_PER_STEP
                tile = sel[:, (b0 // LANES) * LANES:(b0 // LANES + 1) * LANES]
                if b0 % LANES:
                    tile = pltpu.roll(tile, LANES - b0 % LANES, 1)
                sel_ref[kv, ch] = jnp.where(first_lanes, tile, 0.0)


def _nsa_sample_cmp(page_table, q, ck_t, cv_t, cw):
    S, TS, _ = q.shape
    n_pages = page_table.shape[1]
    P = n_pages * PAGE_SIZE
    m_cmp = P // CMP_STRIDE
    steps = n_pages // PAGES_PER_STEP
    pages = _page_specs()
    wspecs = [pl.BlockSpec(w.shape, lambda b, c, pt, nd=w.ndim: (0,) * nd) for w in cw]
    rows_step = PAGES_PER_STEP * PAGE_SIZE
    grid_spec = pltpu.PrefetchScalarGridSpec(
        num_scalar_prefetch=1,
        grid=(S, steps),
        in_specs=[pl.BlockSpec((None, TS, NSA_WIDTH), lambda b, c, pt: (b, 0, 0))] + pages + pages + wspecs,
        out_specs=[pl.BlockSpec((None, NSA_KV_HEADS, Q_PER_KV * TS, LANES), lambda b, c, pt: (b, 0, 0, 0)),
                   pl.BlockSpec((None, NSA_KV_HEADS, steps + 1, TS, LANES), lambda b, c, pt: (b, 0, 0, 0, 0))],
        scratch_shapes=[pltpu.VMEM((m_cmp, 4 * LANES), F32), pltpu.VMEM((m_cmp, 4 * LANES), F32),
                        pltpu.VMEM((rows_step, KV_WIDTH), F32), pltpu.VMEM((rows_step, KV_WIDTH), F32)],
    )
    return pl.pallas_call(
        functools.partial(_nsa_sample_cmp_body, P=P, TS=TS),
        grid_spec=grid_spec,
        out_shape=[jax.ShapeDtypeStruct((S, NSA_KV_HEADS, Q_PER_KV * TS, LANES), F32),
                   jax.ShapeDtypeStruct((S, NSA_KV_HEADS, steps + 1, TS, LANES), F32)],
        compiler_params=_params(("parallel", "arbitrary")),
        name="nsa_sample_cmp",
    )(page_table, q, *([ck_t] * PAGES_PER_STEP), *([cv_t] * PAGES_PER_STEP), *cw)


def _nsa_sample_slc_body(pt_ref, q_ref, sel_ref, ocmp_ref, g_ref, ksn_ref, vsn_ref, kwn_ref, vwn_ref,
                         kwin_ref, vwin_ref, *refs, P, TS):
    n_pg = PAGES_PER_STEP
    kpages, vpages = refs[:n_pg], refs[n_pg:2 * n_pg]
    o_ref, kwout_ref, vwout_ref, q2_ref, m_ref, l_ref, acc_ref = refs[2 * n_pg:]
    c = pl.program_id(1)
    last = pl.num_programs(1) - 1
    rows = NSA_KV_HEADS * Q_PER_KV * TS
    nk = n_pg * PAGE_SIZE
    wb = kwin_ref.shape[1]
    lane = _lane_iota((1, LANES))
    t = _row_iota((rows, 1)) % TS

    def sel_rows(ch):
        return jnp.concatenate([sel_ref[kv, ch] for kv in range(NSA_KV_HEADS) for _ in range(Q_PER_KV)], axis=0)

    def update(s, pv_of):
        m_old = m_ref[...]
        m_new = jnp.maximum(m_old, jnp.max(s, axis=-1, keepdims=True))
        p = jnp.exp(s - m_new[:, 0:1])
        alpha = jnp.exp(m_old - m_new)
        l_ref[...] = alpha * l_ref[...] + jnp.sum(p, axis=-1, keepdims=True)
        acc_ref[...] = alpha * acc_ref[...] + pv_of(p.astype(BF16))
        m_ref[...] = m_new

    def result():
        return jnp.where(m_ref[...] > 0.5 * NEG_INF, acc_ref[...] / jnp.maximum(l_ref[...], 1e-30), 0.0)

    def reset():
        m_ref[...] = jnp.full(m_ref.shape, NEG_INF, F32)
        l_ref[...] = jnp.zeros(l_ref.shape, F32)
        acc_ref[...] = jnp.zeros(acc_ref.shape, F32)

    @pl.when(c == 0)
    def _():
        q = q_ref[...] * SCALE
        q2_ref[...] = jnp.concatenate([_stack_heads(q, kv, lambda hl, kv=kv: kv) for kv in range(NSA_KV_HEADS)],
                                      axis=0).astype(BF16)
        reset()

    q2 = q2_ref[...]
    k_t = jnp.concatenate([r[...] for r in kpages], axis=1).astype(BF16)
    v_t = jnp.concatenate([r[...] for r in vpages], axis=1).astype(BF16)
    sel_c = sel_rows(c)
    chosen = jnp.concatenate(
        [jnp.where(lane < SLC_BLOCK, jnp.broadcast_to(sel_c[:, 2 * i:2 * i + 1], (rows, LANES)),
                   jnp.broadcast_to(sel_c[:, 2 * i + 1:2 * i + 2], (rows, LANES)))
         for i in range(nk // LANES)], axis=1)
    kpos = c * nk + _lane_iota((1, nk))
    bias = jnp.where((chosen > 0.5) & (kpos <= P + t), 0.0, NEG_INF)
    update(_dot(q2, k_t) + bias, lambda p: _dot_nt(p, v_t))

    @pl.when(c == last)
    def _():
        pad_rows = lambda x: jnp.concatenate([x, jnp.zeros((LANES - TS, LANES), F32)], axis=0).astype(BF16)
        own = (lane >> 6) == (_row_iota((rows, 1)) // (Q_PER_KV * TS))
        jn = P // SLC_BLOCK
        sel_n = sel_rows(jn // BLOCKS_PER_STEP)[:, jn % BLOCKS_PER_STEP:jn % BLOCKS_PER_STEP + 1]
        bias_n = jnp.where((sel_n > 0.5) & (lane <= t) & (lane < TS), 0.0, NEG_INF)
        vn = pad_rows(vsn_ref[...])
        update(_dot_nt(q2, pad_rows(ksn_ref[...])) + bias_n, lambda p: _dot(p, vn))
        o_slc = result()
        reset()
        kw_t, vw_t = kwin_ref[...], vwin_ref[...]
        iw = _lane_iota((1, wb))
        rel = t + wb - iw
        bias_b = jnp.where((rel >= 0) & (rel < WINDOW), 0.0, NEG_INF)
        vw_tb = vw_t.astype(BF16)
        update(_dot(q2, kw_t.astype(BF16)) + bias_b, lambda p: _dot_nt(p, vw_tb))
        bias_w = jnp.where((lane <= t) & (lane < TS), 0.0, NEG_INF)
        vwn = pad_rows(vwn_ref[...])
        update(_dot_nt(q2, pad_rows(kwn_ref[...])) + bias_w, lambda p: _dot(p, vwn))
        o_win = result()
        gs = jax.nn.sigmoid(g_ref[...])
        gcol = lambda comp: jnp.concatenate(
            [gs[:, 3 * hq + comp:3 * hq + comp + 1] for hq in range(NSA_HEADS)], axis=0)
        ocmp = jnp.concatenate([ocmp_ref[kv] for kv in range(NSA_KV_HEADS)], axis=0)
        o = gcol(0) * ocmp + gcol(1) * o_slc + gcol(2) * o_win
        o = jnp.where(own, o, 0.0)
        for tile in range(NSA_HEADS // 2):
            a = o[(2 * tile) * TS:(2 * tile + 1) * TS]
            b = o[(2 * tile + 1) * TS:(2 * tile + 2) * TS]
            kv = (2 * tile) // Q_PER_KV
            a = a if kv == 0 else pltpu.roll(a, HEAD_DIM, 1)
            b = b if kv == 1 else pltpu.roll(b, HEAD_DIM, 1)
            o_ref[:, tile * LANES:(tile + 1) * LANES] = a + b
        kwout_ref[0:wb - TS, :] = kw_t.T[TS:wb, :]
        kwout_ref[wb - TS:wb, :] = kwn_ref[...]
        vwout_ref[0:wb - TS, :] = vw_t.T[TS:wb, :]
        vwout_ref[wb - TS:wb, :] = vwn_ref[...]


def _nsa_sample_slc(page_table, q, sel, ocmp, g, ksn, vsn, kwn, vwn, kwin_t, vwin_t, ck_t, cv_t):
    S, TS, _ = q.shape
    n_pages = page_table.shape[1]
    P = n_pages * PAGE_SIZE
    wb = kwin_t.shape[2]
    rows = NSA_KV_HEADS * Q_PER_KV * TS
    steps = n_pages // PAGES_PER_STEP
    pages = _page_specs()
    per_seq = lambda shape: pl.BlockSpec((None,) + shape, lambda b, c, pt, nd=len(shape): (b,) + (0,) * nd)
    grid_spec = pltpu.PrefetchScalarGridSpec(
        num_scalar_prefetch=1,
        grid=(S, steps),
        in_specs=[per_seq((TS, NSA_WIDTH)), per_seq(sel.shape[1:]), per_seq(ocmp.shape[1:]), per_seq((TS, LANES)),
                  per_seq((TS, KV_WIDTH)), per_seq((TS, KV_WIDTH)), per_seq((TS, KV_WIDTH)), per_seq((TS, KV_WIDTH)),
                  per_seq((KV_WIDTH, wb)), per_seq((KV_WIDTH, wb))] + pages + pages,
        out_specs=[per_seq((TS, NSA_WIDTH)), per_seq((wb, KV_WIDTH)), per_seq((wb, KV_WIDTH))],
        scratch_shapes=[pltpu.VMEM((rows, LANES), BF16)] + [pltpu.VMEM((rows, LANES), F32)] * 3,
    )
    return pl.pallas_call(
        functools.partial(_nsa_sample_slc_body, P=P, TS=TS),
        grid_spec=grid_spec,
        out_shape=[jax.ShapeDtypeStruct((S, TS, NSA_WIDTH), F32),
                   jax.ShapeDtypeStruct((S, wb, KV_WIDTH), F32), jax.ShapeDtypeStruct((S, wb, KV_WIDTH), F32)],
        compiler_params=_params(("parallel", "arbitrary")),
        name="nsa_sample_slc",
    )(page_table, q, sel, ocmp, g, ksn, vsn, kwn, vwn, kwin_t, vwin_t,
      *([ck_t] * PAGES_PER_STEP), *([cv_t] * PAGES_PER_STEP))


def _merge_body(x_ref, glu_ref, o_ref, nw_ref, wg_ref, wup_ref, wnsa_ref, wout_ref, out_ref):
    x = x_ref[...]
    gate = jax.nn.sigmoid(_dot(_rms(x, nw_ref[...]).astype(BF16), wg_ref[...]))
    s5_out = _dot(glu_ref[...].astype(BF16), wup_ref[...])
    nsa_out = _dot(o_ref[...].astype(BF16), wnsa_ref[...])
    merged = gate[:, :D_MODEL] * s5_out + gate[:, D_MODEL:] * nsa_out
    out_ref[...] = x + _dot(merged.astype(BF16), wout_ref[...])


def _merge(x, glu, o, nw, wg, wup, wnsa, wout, tt):
    B, T, _ = x.shape
    row = lambda w: pl.BlockSpec((None, tt, w), lambda b, i: (b, i, 0))
    return pl.pallas_call(
        _merge_body,
        grid=(B, T // tt),
        in_specs=[row(D_MODEL), row(S5_WIDTH), row(NSA_WIDTH), _const_spec(nw.shape), _const_spec(wg.shape),
                  _const_spec(wup.shape), _const_spec(wnsa.shape), _const_spec(wout.shape)],
        out_specs=row(D_MODEL),
        out_shape=jax.ShapeDtypeStruct((B, T, D_MODEL), F32),
        compiler_params=_params(("parallel", "parallel")),
        name="merge",
    )(x, glu, o, nw, wg, wup, wnsa, wout)


def _ffn_body(x_ref, nw_ref, w1_ref, w2_ref, out_ref):
    x = x_ref[...]
    a = jnp.maximum(_dot(_rms(x, nw_ref[...]).astype(BF16), w1_ref[...]), 0.0)
    out_ref[...] = x + _dot((a * a).astype(BF16), w2_ref[...])


def _ffn(x, nw, w1, w2, tt):
    B, T, _ = x.shape
    row = pl.BlockSpec((None, tt, D_MODEL), lambda b, i: (b, i, 0))
    return pl.pallas_call(
        _ffn_body,
        grid=(B, T // tt),
        in_specs=[row, _const_spec(nw.shape), _const_spec(w1.shape), _const_spec(w2.shape)],
        out_specs=row,
        out_shape=jax.ShapeDtypeStruct((B, T, D_MODEL), F32),
        compiler_params=_params(("parallel", "parallel")),
        name="ffn",
    )(x, nw, w1, w2)


def _rope_tables(pos):
    half = HEAD_DIM // 2
    inv = ROPE_THETA ** (-jnp.arange(half, dtype=F32) / half)
    ang = pos.astype(F32)[:, None] * inv[None, :]
    cos, sin = jnp.cos(ang), jnp.sin(ang)
    return (jnp.concatenate([cos, cos, cos, cos], axis=-1),
            jnp.concatenate([-sin, sin, -sin, sin], axis=-1))


def _block_diag(w):
    G, a, b = w.shape
    return jnp.einsum('gab,gk->gakb', w, jnp.eye(G, dtype=w.dtype)).reshape(G * a, G * b)


def _cmp_weights(pe, w1, w2):
    eye = jnp.eye(NSA_KV_HEADS, dtype=F32)
    half = CMP_LEN // 2
    big = lambda w: jnp.einsum('rdf,hk->rhdkf', w, eye).reshape(half * KV_WIDTH, NSA_KV_HEADS * CMP_HIDDEN)
    w1b = jnp.concatenate([big(w1[:half]), big(w1[half:])], axis=1).astype(BF16)
    w2d = jnp.concatenate([w2, w2], axis=1).astype(BF16)
    flat = lambda p: jnp.broadcast_to(p[:, None, :], (half, NSA_KV_HEADS, HEAD_DIM)).reshape(1, half * KV_WIDTH)
    pe8 = jnp.concatenate([flat(pe[:half]), flat(pe[half:]), jnp.zeros((SUBLANES - 2, half * KV_WIDTH), F32)], axis=0)
    return w1b, w2d, pe8


def kernel(x_prompt, x_sample, cache_k_cmp, cache_v_cmp, cache_k_slc, cache_v_slc, state_k_win, state_v_win, state_s5_re, state_s5_im, page_table, norm_mix_w, w_in, s5_lam_re, s5_lam_im, s5_log_dt, s5_b_re, s5_b_im, s5_c_re, s5_c_im, s5_d, s5_w_glu, s5_w_up, q_norm_w, k_norm_w, cmp_pe_k, cmp_wk1, cmp_wk2, cmp_pe_v, cmp_wv1, cmp_wv2, nsa_w_up, w_out, norm_ffn_w, w_ffn1, w_ffn2):
    B, T, _ = x_prompt.shape
    S, TS, _ = x_sample.shape
    n_pages = page_table.shape[1]
    P = n_pages * PAGE_SIZE
    n_pool = cache_k_cmp.shape[1]
    assert norm_mix_w.shape[0] == 1 and B % SUBLANES == 0 and S % SUBLANES == 0
    assert TS < CMP_STRIDE and P % SLC_BLOCK == 0 and n_pages % PAGES_PER_STEP == 0

    w = w_in[0]
    cols = lambda a, n: w[:, a:a + n]
    w_a = jnp.concatenate(
        [cols(0, 512), cols(512, 512), cols(1024, 128), cols(1280, 128), cols(1536, 128),
         cols(1152, 128), cols(1408, 128), cols(1664, 128), cols(1792, N_GATE),
         jnp.zeros((D_MODEL, W_A_COLS - 1792 - N_GATE), F32)], axis=1).astype(BF16)
    w_g = w[:, OFF_GMIX:].astype(BF16)
    nw_mix = norm_mix_w[0][None, :]
    qkw = jnp.concatenate([jnp.tile(q_norm_w[0], NSA_HEADS)]
                          + [jnp.tile(k_norm_w[0, i], NSA_KV_HEADS) for i in range(3)])[None, :]
    li = np.arange(LANES)
    seg = jnp.asarray((li[:, None] // HEAD_DIM) == (li[None, :] // HEAD_DIM), BF16)
    lre, lim = s5_lam_re[0].reshape(1, S5_LANES), s5_lam_im[0].reshape(1, S5_LANES)
    ldt = jnp.repeat(s5_log_dt[0], S5_STATE)[None, :]
    bre = _block_diag(jnp.swapaxes(s5_b_re[0], 1, 2))
    bim = _block_diag(jnp.swapaxes(s5_b_im[0], 1, 2))
    cre = _block_diag(jnp.swapaxes(s5_c_re[0], 1, 2)).astype(BF16)
    cim = _block_diag(jnp.swapaxes(s5_c_im[0], 1, 2)).astype(BF16)
    s5d = s5_d[0][None, :]
    wglu = s5_w_glu[0].astype(BF16)
    wup = s5_w_up[0].astype(BF16)
    cw = _cmp_weights(cmp_pe_k[0], cmp_wk1[0], cmp_wk2[0]) + _cmp_weights(cmp_pe_v[0], cmp_wv1[0], cmp_wv2[0])
    wnsa = nsa_w_up[0].astype(BF16)
    wout = w_out[0].astype(BF16)
    nw_ffn = norm_ffn_w[0][None, :]
    w1 = w_ffn1[0].astype(BF16)
    w2 = w_ffn2[0].astype(BF16)

    def trunk(x, glu, o, tt):
        return _ffn(_merge(x, glu, o, nw_mix, w_g, wup, wnsa, wout, tt), nw_ffn, w1, w2, tt)

    cos_p, sin_p = _rope_tables(jnp.arange(T, dtype=jnp.int32))
    u, q, kc, ks, kw, vc, vs, vw, g = _inproj(x_prompt, cos_p, sin_p, nw_mix, w_a, qkw, seg, 512)
    zeros_st = jnp.zeros((B, S5_LANES), F32)
    glu_tm, sre_p, sim_p = _s5(jnp.swapaxes(u, 0, 1), zeros_st, zeros_st, lre, lim, ldt, bre, bim, cre, cim, s5d, wglu, 64)
    o_p = _nsa_prompt(q, kc, vc, ks, vs, kw, vw, g, cw, 256)
    y_prompt = trunk(x_prompt, jnp.swapaxes(glu_tm, 0, 1), o_p, 512)

    n_s = S * TS
    cos_s, sin_s = _rope_tables(P + jnp.arange(TS, dtype=jnp.int32))
    cos_s, sin_s = jnp.tile(cos_s, (S, 1)), jnp.tile(sin_s, (S, 1))
    xs = x_sample.reshape(1, n_s, D_MODEL)
    us, qs, kcs, kss, kws, vcs, vss, vws, gs = _inproj(xs, cos_s, sin_s, nw_mix, w_a, qkw, seg, n_s)
    seq = lambda a: a.reshape(S, TS, a.shape[-1])
    u_tm = jnp.swapaxes(seq(us), 0, 1)
    glu_s_tm, sre_s, sim_s = _s5(u_tm, state_s5_re[0].reshape(S, S5_LANES), state_s5_im[0].reshape(S, S5_LANES),
                                 lre, lim, ldt, bre, bim, cre, cim, s5d, wglu, TS)
    pool = lambda c: jnp.transpose(c[0], (0, 2, 3, 1)).reshape(n_pool, KV_WIDTH, PAGE_SIZE)
    ocmp_s, sel_s = _nsa_sample_cmp(page_table, seq(qs), pool(cache_k_cmp), pool(cache_v_cmp), cw)
    win = lambda s: jnp.transpose(s[0], (0, 2, 3, 1)).reshape(S, KV_WIDTH, s.shape[2])
    o_s, kwin_s, vwin_s = _nsa_sample_slc(page_table, seq(qs), sel_s, ocmp_s, seq(gs), seq(kss), seq(vss), seq(kws),
                                          seq(vws), win(state_k_win), win(state_v_win),
                                          pool(cache_k_slc), pool(cache_v_slc))
    glu_s = jnp.swapaxes(glu_s_tm, 0, 1).reshape(1, n_s, S5_WIDTH)
    y_sample = trunk(xs, glu_s, o_s.reshape(1, n_s, NSA_WIDTH), n_s).reshape(S, TS, D_MODEL)

    heads_p = lambda a: a.reshape(1, B, a.shape[1], NSA_KV_HEADS, HEAD_DIM)
    heads_s = lambda a: a.reshape(1, S, -1, NSA_KV_HEADS, HEAD_DIM)
    keep = min(WINDOW, T)
    st_p = lambda a: a.reshape(1, B, S5_GROUPS, S5_STATE)
    st_s = lambda a: a.reshape(1, S, S5_GROUPS, S5_STATE)
    return (y_prompt, y_sample,
            heads_p(kc), heads_s(kcs), heads_p(vc), heads_s(vcs),
            heads_p(ks), heads_s(kss), heads_p(vs), heads_s(vss),
            heads_p(kw[:, T - keep:]), heads_s(kwin_s), heads_p(vw[:, T - keep:]), heads_s(vwin_s),
            st_p(sre_p), st_s(sre_s), st_p(sim_p), st_s(sim_s))
```

```python
import functools

import numpy as np
import jax
import jax.numpy as jnp
from jax import lax
from jax.experimental import pallas as pl
from jax.experimental.pallas import tpu as pltpu

F32 = jnp.float32
BF16 = jnp.bfloat16

D_MODEL = 1024
PAGE_SIZE = 128
S5_WIDTH = 512
S5_GROUPS = 32
S5_STATE = 64
S5_LANES = S5_GROUPS * S5_STATE
HEAD_DIM = 64
NSA_HEADS = 8
NSA_KV_HEADS = 2
Q_PER_KV = 4
NSA_WIDTH = 512
KV_WIDTH = 128
CMP_LEN = 32
CMP_STRIDE = 16
CMP_HIDDEN = 128
SLC_BLOCK = 64
SLC_TOPK = 16
WINDOW = 512
ROPE_THETA = 10000.0
EPS = 1e-6
NEG_INF = -1e30
FORCED = 1e9
SCALE = HEAD_DIM ** -0.5
N_GATE = 3 * NSA_HEADS
OFF_GMIX = 2 * 512 + 6 * KV_WIDTH + N_GATE
W_A_COLS = 1920

LANES = 128
SUBLANES = 8
VMEM_LIMIT = 56 * 1024 * 1024
BOUND_SLACK = 1.02
MAX_SHIFT_GAP = 60.0


def _params(sem):
    return pltpu.CompilerParams(dimension_semantics=sem, vmem_limit_bytes=VMEM_LIMIT)


def _const_spec(shape):
    nd = len(shape)
    return pl.BlockSpec(shape, lambda *_: (0,) * nd)


def _dot(a, b):
    return jnp.dot(a, b, preferred_element_type=F32)


def _dot_nt(a, b):
    return lax.dot_general(a, b, (((1,), (1,)), ((), ())), preferred_element_type=F32)


def _rms(x, w):
    r = lax.rsqrt(jnp.mean(x * x, axis=-1, keepdims=True) + EPS)
    return x * r * w


def _lane_iota(shape):
    return lax.broadcasted_iota(jnp.int32, shape, len(shape) - 1)


def _row_iota(shape):
    return lax.broadcasted_iota(jnp.int32, shape, len(shape) - 2)


def _split3(x):
    hi = x.astype(BF16)
    r1 = x - hi.astype(F32)
    mid = r1.astype(BF16)
    lo = (r1 - mid.astype(F32)).astype(BF16)
    return hi, mid, lo


def _inproj_body(x_ref, nw_ref, w_ref, qkw_ref, seg_ref, cos_ref, sin_ref,
                 u_ref, q_ref, kc_ref, ks_ref, kw_ref, vc_ref, vs_ref, vw_ref, g_ref):
    xn = _rms(x_ref[...], nw_ref[...]).astype(BF16)
    z = _dot(xn, w_ref[...])
    u_ref[...] = z[:, :512]
    cos = cos_ref[...]
    sin = sin_ref[...]
    seg = seg_ref[...]
    first_half = (_lane_iota((1, LANES)) % HEAD_DIM) < (HEAD_DIM // 2)

    def norm_rope(t, w):
        ss = _dot((t * t).astype(BF16), seg)
        t = t * lax.rsqrt(ss * (1.0 / HEAD_DIM) + EPS) * w
        rot = jnp.where(first_half, pltpu.roll(t, LANES - HEAD_DIM // 2, 1), pltpu.roll(t, HEAD_DIM // 2, 1))
        return t * cos + rot * sin

    for j in range(4):
        q_ref[:, j * LANES:(j + 1) * LANES] = norm_rope(
            z[:, 512 + j * LANES:512 + (j + 1) * LANES], qkw_ref[:, j * LANES:(j + 1) * LANES])
    for j, r in enumerate((kc_ref, ks_ref, kw_ref)):
        r[...] = norm_rope(z[:, 1024 + j * LANES:1024 + (j + 1) * LANES],
                           qkw_ref[:, 512 + j * LANES:512 + (j + 1) * LANES])
    vc_ref[...] = z[:, 1408:1536]
    vs_ref[...] = z[:, 1536:1664]
    vw_ref[...] = z[:, 1664:1792]
    g_ref[...] = z[:, 1792:1920]


def _inproj(x, cos_t, sin_t, nw, w_a, qkw, seg, tt):
    B, T, _ = x.shape
    row = lambda w: pl.BlockSpec((None, tt, w), lambda b, i: (b, i, 0))
    tab = pl.BlockSpec((tt, LANES), lambda b, i: (i, 0))
    kv = jax.ShapeDtypeStruct((B, T, KV_WIDTH), F32)
    out_shape = [jax.ShapeDtypeStruct((B, T, 512), F32), jax.ShapeDtypeStruct((B, T, 512), F32)] + [kv] * 7
    return pl.pallas_call(
        _inproj_body,
        grid=(B, T // tt),
        in_specs=[row(D_MODEL), _const_spec((1, D_MODEL)), _const_spec((D_MODEL, W_A_COLS)),
                  _const_spec((1, 896)), _const_spec((LANES, LANES)), tab, tab],
        out_specs=[row(512), row(512)] + [row(KV_WIDTH)] * 7,
        out_shape=out_shape,
        compiler_params=_params(("parallel", "parallel")),
        name="inproj",
    )(x, nw, w_a, qkw, seg, cos_t, sin_t)


def _s5_body(u_ref, sre0_ref, sim0_ref, lre_ref, lim_ref, ldt_ref, bre_ref, bim_ref, cre_ref, cim_ref,
             d_ref, wglu_ref, out_ref, sre_ref, sim_ref,
             bbar_ref, a_ref, st_ref, xr_ref, xi_ref, *, tc):
    c = pl.program_id(1)
    rows = tc * SUBLANES

    @pl.when(c == 0)
    def _():
        lr, li = lre_ref[...], lim_ref[...]
        dt = jnp.exp(ldt_ref[...])
        mag = jnp.exp(lr * dt)
        ab_re, ab_im = mag * jnp.cos(li * dt), mag * jnp.sin(li * dt)
        den = lr * lr + li * li
        f_re = ((ab_re - 1.0) * lr + ab_im * li) / den
        f_im = (ab_im * lr - (ab_re - 1.0) * li) / den
        bre, bim = bre_ref[...], bim_ref[...]
        bbar_ref[:, :S5_LANES] = (f_re * bre - f_im * bim).astype(BF16)
        bbar_ref[:, S5_LANES:] = (f_re * bim + f_im * bre).astype(BF16)
        a_ref[0] = jnp.broadcast_to(ab_re, (SUBLANES, S5_LANES))
        a_ref[1] = jnp.broadcast_to(ab_im, (SUBLANES, S5_LANES))
        st_ref[0] = sre0_ref[...]
        st_ref[1] = sim0_ref[...]

    u = u_ref[...].reshape(rows, S5_WIDTH)
    ub = u.astype(BF16)
    xr_ref[...] = _dot(ub, bbar_ref[:, :S5_LANES])
    xi_ref[...] = _dot(ub, bbar_ref[:, S5_LANES:])

    slab = 512
    for lc in range(S5_LANES // slab):
        sl = slice(lc * slab, (lc + 1) * slab)
        ar, ai = a_ref[0, :, sl], a_ref[1, :, sl]

        def step(t, carry, sl=sl, ar=ar, ai=ai):
            sr, si = carry
            r0 = pl.multiple_of(t * SUBLANES, SUBLANES)
            nr = ar * sr - ai * si + xr_ref[pl.ds(r0, SUBLANES), sl]
            ni = ar * si + ai * sr + xi_ref[pl.ds(r0, SUBLANES), sl]
            xr_ref[pl.ds(r0, SUBLANES), sl] = nr
            xi_ref[pl.ds(r0, SUBLANES), sl] = ni
            return nr, ni

        sr, si = lax.fori_loop(0, tc, step, (st_ref[0, :, sl], st_ref[1, :, sl]), unroll=min(tc, 8))
        st_ref[0, :, sl] = sr
        st_ref[1, :, sl] = si

    y = (_dot(xr_ref[...].astype(BF16), cre_ref[...]) - _dot(xi_ref[...].astype(BF16), cim_ref[...])
         + d_ref[...] * u)
    yg = jax.nn.gelu(y)
    glu = yg * jax.nn.sigmoid(_dot(yg.astype(BF16), wglu_ref[...]))
    out_ref[...] = glu.reshape(tc, SUBLANES, S5_WIDTH)

    @pl.when(c == pl.num_programs(1) - 1)
    def _():
        sre_ref[...] = st_ref[0]
        sim_ref[...] = st_ref[1]


def _s5(u_tm, sre0, sim0, lre, lim, ldt, bre, bim, cre, cim, d, wglu, tc):
    T, B, _ = u_tm.shape
    st_spec = pl.BlockSpec((SUBLANES, S5_LANES), lambda b, c: (b, 0))
    blk = pl.BlockSpec((tc, SUBLANES, S5_WIDTH), lambda b, c: (c, b, 0))
    st = jax.ShapeDtypeStruct((B, S5_LANES), F32)
    return pl.pallas_call(
        functools.partial(_s5_body, tc=tc),
        grid=(B // SUBLANES, T // tc),
        in_specs=[blk, st_spec, st_spec,
                  _const_spec((1, S5_LANES)), _const_spec((1, S5_LANES)), _const_spec((1, S5_LANES)),
                  _const_spec((S5_WIDTH, S5_LANES)), _const_spec((S5_WIDTH, S5_LANES)),
                  _const_spec((S5_LANES, S5_WIDTH)), _const_spec((S5_LANES, S5_WIDTH)),
                  _const_spec((1, S5_WIDTH)), _const_spec((S5_WIDTH, S5_WIDTH))],
        out_specs=[blk, st_spec, st_spec],
        out_shape=[jax.ShapeDtypeStruct((T, B, S5_WIDTH), F32), st, st],
        scratch_shapes=[pltpu.VMEM((S5_WIDTH, 2 * S5_LANES), BF16),
                        pltpu.VMEM((2, SUBLANES, S5_LANES), F32),
                        pltpu.VMEM((2, SUBLANES, S5_LANES), F32),
                        pltpu.VMEM((tc * SUBLANES, S5_LANES), F32),
                        pltpu.VMEM((tc * SUBLANES, S5_LANES), F32)],
        compiler_params=_params(("parallel", "arbitrary")),
        name="s5",
    )(u_tm, sre0, sim0, lre, lim, ldt, bre, bim, cre, cim, d, wglu)


def _head_lanes(x, half):
    return jnp.where((_lane_iota((1, LANES)) >> 6) == half, x, 0.0)


def _masked_softmax_parts(s, mask):
    s = jnp.where(mask, s, NEG_INF)
    m = jnp.max(s, axis=-1, keepdims=True)
    p = jnp.where(mask, jnp.exp(s - m), 0.0)
    l = jnp.maximum(jnp.sum(p, axis=-1, keepdims=True), 1e-30)
    return p, l


def _cmp_hidden(h, w1_ref, pe_ref):
    m = h.shape[0]
    pe = pe_ref[...].astype(BF16)
    bias = _dot(pe, w1_ref[:, :256])[0:1] + _dot(pe, w1_ref[:, 256:])[1:2]
    return jax.nn.gelu(h[:, :256] + pltpu.roll(h[:, 256:], m - 1, 0) + bias)


def _overlap(n_rows, n_cols, transpose):
    shape = (n_cols, n_rows) if transpose else (n_rows, n_cols)
    n = _row_iota(shape) if not transpose else _lane_iota(shape)
    j = _lane_iota(shape) if not transpose else _row_iota(shape)
    lo = jnp.maximum(n * CMP_STRIDE, j * SLC_BLOCK)
    hi = jnp.minimum(n * CMP_STRIDE + CMP_LEN, j * SLC_BLOCK + SLC_BLOCK)
    return (jnp.maximum(hi - lo, 0).astype(F32) * (1.0 / CMP_LEN)).astype(BF16)


def _ones_row_rows(n):
    return jnp.where(_row_iota((HEAD_DIM, n)) == 0, 1.0, 0.0)


def _shifted_update_t(m_ref, acc_ref, h, s_t, v_t):
    tk, tq = s_t.shape
    m_ref[h] = jnp.maximum(m_ref[h], jnp.max(s_t.reshape(tk // SUBLANES, SUBLANES, tq), axis=0))
    acc_ref[h] = acc_ref[h] + _dot(v_t, jnp.exp(s_t).astype(BF16))


def _online_update_t(m_ref, acc_ref, h, s_t, v_t):
    m_old = m_ref[h]
    m_new = jnp.maximum(m_old, jnp.max(s_t, axis=0, keepdims=True))
    p_t = jnp.exp(s_t - m_new[0:1])
    alpha = jnp.exp(m_old - m_new)
    acc_ref[h] = alpha[0:1] * acc_ref[h] + _dot(v_t, p_t.astype(BF16))
    m_ref[h] = m_new


def _softmax_result_t(m_ref, acc_ref, h):
    acc = acc_ref[h]
    out = acc[0:HEAD_DIM] / jnp.maximum(acc[HEAD_DIM:HEAD_DIM + 1], 1e-30)
    return jnp.where(jnp.max(m_ref[h], axis=0, keepdims=True) > 0.5 * NEG_INF, out, 0.0)


def _key_rows(k, kv):
    lane = _lane_iota((1, LANES))
    kk = jnp.where(lane < HEAD_DIM, jnp.where(kv == 0, k, pltpu.roll(k, HEAD_DIM, 1)), 0.0)
    norm2 = jnp.max(jnp.sum(kk * kk, axis=1, keepdims=True), axis=0, keepdims=True)
    return jnp.where(lane == HEAD_DIM, 1.0, kk).astype(BF16), jnp.sqrt(norm2)


def _nsa_prompt_body(q_ref, kc16_ref, vc16_ref, ks_ref, kw_ref, vs_ref, vw_ref, g_ref,
                     w1k_ref, w2k_ref, pek_ref, w1v_ref, w2v_ref, pev_ref,
                     o_ref,
                     ksa_ref, kwa_ref, vst_ref, vwt_ref, kcmp_ref, vcmpt_ref, knorm_ref,
                     qc_ref, qs_ref, qw_ref, selb_ref, ms_ref, accs_ref, mw_ref, accw_ref, *, T, TQ):
    kv = pl.program_id(1)
    qi = pl.program_id(2)
    m_cmp = T // CMP_STRIDE
    n_cmp = m_cmp - 1
    n_slc = T // SLC_BLOCK
    topk = min(SLC_TOPK, n_slc)
    n_kt = T // TQ
    lane = _lane_iota((1, LANES))

    @pl.when(qi == 0)
    def _():
        ksa_ref[...], ns = _key_rows(ks_ref[...], kv)
        kwa_ref[...], nw = _key_rows(kw_ref[...], kv)
        knorm_ref[0] = jnp.broadcast_to(ns, (SUBLANES, LANES))
        knorm_ref[1] = jnp.broadcast_to(nw, (SUBLANES, LANES))
        ones = _ones_row_rows(TQ).astype(BF16)
        for kt in range(n_kt):
            for src, dst in ((vs_ref, vst_ref), (vw_ref, vwt_ref)):
                vt = src[kt * TQ:(kt + 1) * TQ, :].T
                dst[kt, 0:HEAD_DIM, :] = jnp.where(kv == 0, vt[0:HEAD_DIM], vt[HEAD_DIM:]).astype(BF16)
                dst[kt, HEAD_DIM:, :] = ones

        def compressed(x16_ref, w1_ref, pe_ref, w2_ref):
            hid = _cmp_hidden(_dot(x16_ref[...].astype(BF16), w1_ref[...]), w1_ref, pe_ref)
            hid = jnp.where(kv == 0, hid[:, :LANES], hid[:, LANES:])
            return _dot(hid.astype(BF16), w2_ref[...])

        kcmp_ref[...] = compressed(kc16_ref, w1k_ref, pek_ref, w2k_ref).astype(BF16)
        vcmp_t = compressed(vc16_ref, w1v_ref, pev_ref, w2v_ref).T
        vcmpt_ref[0:HEAD_DIM, :] = vcmp_t[0:HEAD_DIM].astype(BF16)
        vcmpt_ref[HEAD_DIM:, :] = jnp.zeros((HEAD_DIM, m_cmp), BF16)

    r0 = pl.multiple_of(qi * TQ, TQ)
    for pp in range(2):
        qt = q_ref[pl.ds(r0, TQ), pp * LANES:(pp + 1) * LANES] * SCALE
        for e in range(2):
            qh = jnp.where(lane < HEAD_DIM, qt if e == 0 else pltpu.roll(qt, HEAD_DIM, 1), 0.0)
            qc_ref[2 * pp + e] = qh.astype(BF16)
            qnorm = jnp.sqrt(jnp.sum(qh * qh, axis=1, keepdims=True))
            for dst, which in ((qs_ref, 0), (qw_ref, 1)):
                bound = BOUND_SLACK * qnorm * knorm_ref[which][0:1, 0:1]
                dst[(2 * pp + e) * TQ:(2 * pp + e + 1) * TQ, :] = jnp.where(lane == HEAD_DIM, -bound, qh).astype(BF16)
    qpos = r0 + _lane_iota((1, TQ))

    n = _row_iota((m_cmp, TQ))
    mask_c = (n * CMP_STRIDE + CMP_LEN - 1 <= qpos) & (n < n_cmp)
    pg = jnp.zeros((m_cmp, TQ), F32)
    o_cmp = []
    for h in range(Q_PER_KV):
        s_t = jnp.where(mask_c, _dot_nt(kcmp_ref[...], qc_ref[h]), NEG_INF)
        p_t = jnp.where(mask_c, jnp.exp(s_t - jnp.max(s_t, axis=0, keepdims=True)), 0.0)
        inv_l = 1.0 / jnp.maximum(jnp.sum(p_t, axis=0, keepdims=True), 1e-30)
        pg = pg + p_t * inv_l
        o_cmp.append(_dot(vcmpt_ref[...], p_t.astype(BF16))[0:HEAD_DIM] * inv_l)

    ovt = _overlap(m_cmp, n_slc, transpose=True)
    imp = sum(_dot(ovt, part) for part in _split3(pg))
    j = _row_iota((n_slc, TQ))
    qblk = qpos // SLC_BLOCK
    valid = j <= qblk
    forced = (j == 0) | (j == qblk) | (j == qblk - 1)
    score = jnp.where(forced, FORCED, jnp.where(valid, imp, NEG_INF))
    rank = jnp.zeros((n_slc, TQ), jnp.int32)
    for i in range(n_slc):
        si = score[i:i + 1, :]
        beats = (si > score) | ((si == score) & (i < j))
        rank = rank + beats.astype(jnp.int32)
    def reset():
        for m_r, acc_r in ((ms_ref, accs_ref), (mw_ref, accw_ref)):
            m_r[...] = jnp.full(m_r.shape, NEG_INF, F32)
            acc_r[...] = jnp.zeros(acc_r.shape, F32)

    sel_bias = jnp.where((rank < topk) & valid, 0.0, NEG_INF)
    for jb in range(n_slc):
        selb_ref[jb] = jnp.broadcast_to(sel_bias[jb:jb + 1], (SUBLANES, TQ))
    not_after = _row_iota((TQ, 1)) <= _lane_iota((1, TQ))
    bias_diag = jnp.where(not_after, 0.0, NEG_INF)
    bias_far = jnp.where(not_after, NEG_INF, 0.0)
    blocks_per_tile = TQ // SLC_BLOCK
    win_tiles = WINDOW // TQ

    def both_branches(update):
        def slc_tile(kb, bias_extra):
            k0 = pl.multiple_of(kb * TQ, TQ)
            rows = []
            for jb in range(blocks_per_tile):
                rows += [selb_ref[kb * blocks_per_tile + jb]] * (SLC_BLOCK // SUBLANES)
            bias = jnp.concatenate(rows, axis=0)
            if bias_extra is not None:
                bias = bias + bias_extra
            s_all = _dot_nt(ksa_ref[pl.ds(k0, TQ), :], qs_ref[...])
            for h in range(Q_PER_KV):
                update(ms_ref, accs_ref, h, s_all[:, h * TQ:(h + 1) * TQ] + bias, vst_ref[kb])

        def slc_body(kb, carry):
            slc_tile(kb, None)
            return carry

        lax.fori_loop(0, qi, slc_body, 0)
        slc_tile(qi, bias_diag)

        for d in range(win_tiles + 1):
            kb = jnp.maximum(qi - d, 0)
            k0 = pl.multiple_of(kb * TQ, TQ)
            exists = jnp.where(qi >= d, 0.0, NEG_INF)
            bias = exists + (bias_diag if d == 0 else bias_far if d == win_tiles else 0.0)
            s_all = _dot_nt(kwa_ref[pl.ds(k0, TQ), :], qw_ref[...])
            for h in range(Q_PER_KV):
                update(mw_ref, accw_ref, h, s_all[:, h * TQ:(h + 1) * TQ] + bias, vwt_ref[kb])

    reset()
    both_branches(_shifted_update_t)

    worst = jnp.full((1, TQ), 0.0, F32)
    for m_r in (ms_ref, mw_ref):
        for h in range(Q_PER_KV):
            top = jnp.max(m_r[h], axis=0, keepdims=True)
            worst = jnp.minimum(worst, jnp.where(top > 0.5 * NEG_INF, top, 0.0))
    redo = jnp.min(worst) < -MAX_SHIFT_GAP

    @pl.when(redo)
    def _():
        reset()
        both_branches(_online_update_t)

    g = g_ref[pl.ds(r0, TQ), :]
    gs_t = jax.nn.sigmoid(jnp.where(kv == 0, g, pltpu.roll(g, LANES - 3 * Q_PER_KV, 1))).T
    for pp in range(2):
        halves = []
        for e in range(2):
            h = 2 * pp + e
            halves.append(gs_t[3 * h:3 * h + 1] * o_cmp[h]
                          + gs_t[3 * h + 1:3 * h + 2] * _softmax_result_t(ms_ref, accs_ref, h)
                          + gs_t[3 * h + 2:3 * h + 3] * _softmax_result_t(mw_ref, accw_ref, h))
        o_ref[pl.ds(r0, TQ), pp * LANES:(pp + 1) * LANES] = jnp.concatenate(halves, axis=0).T


def _nsa_prompt(q, kc, vc, ks, vs, kw, vw, g, cw, tq):
    B, T, _ = q.shape
    m_cmp = T // CMP_STRIDE
    kc16 = kc.reshape(B, m_cmp, CMP_STRIDE * KV_WIDTH)
    vc16 = vc.reshape(B, m_cmp, CMP_STRIDE * KV_WIDTH)
    full = lambda w: pl.BlockSpec((None, T, w), lambda b, h, i: (b, 0, 0))
    x16 = pl.BlockSpec((None, m_cmp, CMP_STRIDE * KV_WIDTH), lambda b, h, i: (b, 0, 0))
    qo = pl.BlockSpec((None, T, 2 * LANES), lambda b, h, i: (b, 0, h))
    wspecs = [_const_spec(w.shape) for w in cw]
    k_rows = pltpu.VMEM((T, LANES), BF16)
    v_t = pltpu.VMEM((T // tq, LANES, tq), BF16)
    q_rows = pltpu.VMEM((Q_PER_KV, tq, LANES), BF16)
    q_all = pltpu.VMEM((Q_PER_KV * tq, LANES), BF16)
    run_max = pltpu.VMEM((Q_PER_KV, SUBLANES, tq), F32)
    run_acc = pltpu.VMEM((Q_PER_KV, LANES, tq), F32)
    return pl.pallas_call(
        functools.partial(_nsa_prompt_body, T=T, TQ=tq),
        grid=(B, NSA_KV_HEADS, T // tq),
        in_specs=[qo, x16, x16, full(KV_WIDTH), full(KV_WIDTH), full(KV_WIDTH), full(KV_WIDTH), full(KV_WIDTH)] + wspecs,
        out_specs=qo,
        out_shape=jax.ShapeDtypeStruct((B, T, NSA_WIDTH), F32),
        scratch_shapes=[k_rows, k_rows, v_t, v_t,
                        pltpu.VMEM((m_cmp, LANES), BF16), pltpu.VMEM((LANES, m_cmp), BF16),
                        pltpu.VMEM((2, SUBLANES, LANES), F32),
                        q_rows, q_all, q_all, pltpu.VMEM((T // SLC_BLOCK, SUBLANES, tq), F32),
                        run_max, run_acc, run_max, run_acc],
        compiler_params=_params(("parallel", "arbitrary", "arbitrary")),
        name="nsa_prompt",
    )(q, kc16, vc16, ks, kw, vs, vw, g, *cw)


PAGES_PER_STEP = 16
BLOCKS_PER_STEP = PAGES_PER_STEP * PAGE_SIZE // SLC_BLOCK


def _page_specs():
    return [pl.BlockSpec((None, KV_WIDTH, PAGE_SIZE), lambda b, c, pt, k=k: (pt[b, c * PAGES_PER_STEP + k], 0, 0))
            for k in range(PAGES_PER_STEP)]


def _stack_heads(q, kv, lane_half):
    rows = []
    for hl in range(Q_PER_KV):
        tile = 2 * kv + hl // 2
        x = _head_lanes(q[:, tile * LANES:(tile + 1) * LANES], hl % 2)
        rows.append(x if lane_half(hl) == hl % 2 else pltpu.roll(x, HEAD_DIM, 1))
    return jnp.concatenate(rows, axis=0)


def _nsa_sample_cmp_body(pt_ref, q_ref, *refs, P, TS):
    n_pg = PAGES_PER_STEP
    kpages, vpages = refs[:n_pg], refs[n_pg:2 * n_pg]
    w1k_ref, w2k_ref, pek_ref, w1v_ref, w2v_ref, pev_ref = refs[2 * n_pg:2 * n_pg + 6]
    ocmp_ref, sel_ref, hk_ref, hv_ref, xk_ref, xv_ref = refs[2 * n_pg + 6:]
    c = pl.program_id(1)
    groups = n_pg * PAGE_SIZE // CMP_STRIDE
    m_cmp = P // CMP_STRIDE
    n_cmp = (P + TS - CMP_LEN) // CMP_STRIDE + 1
    n_slc = -(-(P + TS) // SLC_BLOCK)
    n_slc_pad = -(-n_slc // LANES) * LANES
    topk = min(SLC_TOPK, n_slc)

    r0 = pl.multiple_of(c * groups, groups)
    for pages, x_ref, w1_ref, h_ref in ((kpages, xk_ref, w1k_ref, hk_ref), (vpages, xv_ref, w1v_ref, hv_ref)):
        for k, page in enumerate(pages):
            x_ref[k * PAGE_SIZE:(k + 1) * PAGE_SIZE, :] = page[...].T
        h = jnp.zeros((groups, 4 * LANES), F32)
        for i in range(CMP_STRIDE // 2):
            pair = jnp.concatenate([x_ref[pl.ds(2 * i, groups, stride=CMP_STRIDE), :],
                                    x_ref[pl.ds(2 * i + 1, groups, stride=CMP_STRIDE), :]], axis=1)
            h = h + _dot(pair.astype(BF16), w1_ref[2 * i * KV_WIDTH:(2 * i + 2) * KV_WIDTH, :])
        h_ref[pl.ds(r0, groups), :] = h

    @pl.when(c == pl.num_programs(1) - 1)
    def _():
        hidk = _cmp_hidden(hk_ref[...], w1k_ref, pek_ref).astype(BF16)
        hidv = _cmp_hidden(hv_ref[...], w1v_ref, pev_ref).astype(BF16)
        q = q_ref[...] * SCALE
        rows = Q_PER_KV * TS
        t = _row_iota((rows, 1)) % TS
        qpos = P + t
        n = _lane_iota((rows, m_cmp))
        mask = (n * CMP_STRIDE + CMP_LEN - 1 <= qpos) & (n < n_cmp)
        ov = _overlap(m_cmp, n_slc_pad, transpose=False)
        jl = _lane_iota((TS, n_slc_pad))
        qblk = (P + _row_iota((TS, 1))) // SLC_BLOCK
        valid = (jl <= qblk) & (jl < n_slc)
        forced = ((jl == 0) | (jl == qblk) | (jl == qblk - 1)) & (jl < n_slc)
        first_lanes = _lane_iota((1, LANES)) < BLOCKS_PER_STEP
        for kv in range(NSA_KV_HEADS):
            kcmp = _dot(hidk[:, kv * LANES:(kv + 1) * LANES], w2k_ref[...]).astype(BF16)
            vcmp = _dot(hidv[:, kv * LANES:(kv + 1) * LANES], w2v_ref[...]).astype(BF16)
            qs = _stack_heads(q, kv, lambda hl: hl % 2).astype(BF16)
            p, l = _masked_softmax_parts(_dot_nt(qs, kcmp), mask)
            ocmp_ref[kv] = _dot(p.astype(BF16), vcmp) / l
            pn = p / l
            pg = pn[0:TS] + pn[TS:2 * TS] + pn[2 * TS:3 * TS] + pn[3 * TS:4 * TS]
            imp = sum(_dot(part, ov) for part in _split3(pg))
            score = jnp.where(forced, FORCED, jnp.where(valid, imp, NEG_INF))
            rank = jnp.zeros((TS, n_slc_pad), jnp.int32)
            for i in range(n_slc):
                si = score[:, i:i + 1]
                beats = (si > score) | ((si == score) & (i < jl))
                rank = rank + beats.astype(jnp.int32)
            sel = jnp.where((rank < topk) & valid, 1.0, 0.0)
            for ch in range(sel_ref.shape[1]):
                b0 = ch * BLOCKS_PER_STEP
                tile = sel[:, (b0 // LANES) * LANES:(b0 // LANES + 1) * LANES]
                if b0 % LANES:
                    tile = pltpu.roll(tile, LANES - b0 % LANES, 1)
                sel_ref[kv, ch] = jnp.where(first_lanes, tile, 0.0)


def _nsa_sample_cmp(page_table, q, ck_t, cv_t, cw):
    S, TS, _ = q.shape
    n_pages = page_table.shape[1]
    P = n_pages * PAGE_SIZE
    m_cmp = P // CMP_STRIDE
    steps = n_pages // PAGES_PER_STEP
    pages = _page_specs()
    wspecs = [pl.BlockSpec(w.shape, lambda b, c, pt, nd=w.ndim: (0,) * nd) for w in cw]
    rows_step = PAGES_PER_STEP * PAGE_SIZE
    grid_spec = pltpu.PrefetchScalarGridSpec(
        num_scalar_prefetch=1,
        grid=(S, steps),
        in_specs=[pl.BlockSpec((None, TS, NSA_WIDTH), lambda b, c, pt: (b, 0, 0))] + pages + pages + wspecs,
        out_specs=[pl.BlockSpec((None, NSA_KV_HEADS, Q_PER_KV * TS, LANES), lambda b, c, pt: (b, 0, 0, 0)),
                   pl.BlockSpec((None, NSA_KV_HEADS, steps + 1, TS, LANES), lambda b, c, pt: (b, 0, 0, 0, 0))],
        scratch_shapes=[pltpu.VMEM((m_cmp, 4 * LANES), F32), pltpu.VMEM((m_cmp, 4 * LANES), F32),
                        pltpu.VMEM((rows_step, KV_WIDTH), F32), pltpu.VMEM((rows_step, KV_WIDTH), F32)],
    )
    return pl.pallas_call(
        functools.partial(_nsa_sample_cmp_body, P=P, TS=TS),
        grid_spec=grid_spec,
        out_shape=[jax.ShapeDtypeStruct((S, NSA_KV_HEADS, Q_PER_KV * TS, LANES), F32),
                   jax.ShapeDtypeStruct((S, NSA_KV_HEADS, steps + 1, TS, LANES), F32)],
        compiler_params=_params(("parallel", "arbitrary")),
        name="nsa_sample_cmp",
    )(page_table, q, *([ck_t] * PAGES_PER_STEP), *([cv_t] * PAGES_PER_STEP), *cw)


def _nsa_sample_slc_body(pt_ref, q_ref, sel_ref, ocmp_ref, g_ref, ksn_ref, vsn_ref, kwn_ref, vwn_ref,
                         kwin_ref, vwin_ref, *refs, P, TS):
    n_pg = PAGES_PER_STEP
    kpages, vpages = refs[:n_pg], refs[n_pg:2 * n_pg]
    o_ref, kwout_ref, vwout_ref, q2_ref, m_ref, l_ref, acc_ref = refs[2 * n_pg:]
    c = pl.program_id(1)
    last = pl.num_programs(1) - 1
    rows = NSA_KV_HEADS * Q_PER_KV * TS
    nk = n_pg * PAGE_SIZE
    wb = kwin_ref.shape[1]
    lane = _lane_iota((1, LANES))
    t = _row_iota((rows, 1)) % TS

    def sel_rows(ch):
        return jnp.concatenate([sel_ref[kv, ch] for kv in range(NSA_KV_HEADS) for _ in range(Q_PER_KV)], axis=0)

    def update(s, pv_of):
        m_old = m_ref[...]
        m_new = jnp.maximum(m_old, jnp.max(s, axis=-1, keepdims=True))
        p = jnp.exp(s - m_new[:, 0:1])
        alpha = jnp.exp(m_old - m_new)
        l_ref[...] = alpha * l_ref[...] + jnp.sum(p, axis=-1, keepdims=True)
        acc_ref[...] = alpha * acc_ref[...] + pv_of(p.astype(BF16))
        m_ref[...] = m_new

    def result():
        return jnp.where(m_ref[...] > 0.5 * NEG_INF, acc_ref[...] / jnp.maximum(l_ref[...], 1e-30), 0.0)

    def reset():
        m_ref[...] = jnp.full(m_ref.shape, NEG_INF, F32)
        l_ref[...] = jnp.zeros(l_ref.shape, F32)
        acc_ref[...] = jnp.zeros(acc_ref.shape, F32)

    @pl.when(c == 0)
    def _():
        q = q_ref[...] * SCALE
        q2_ref[...] = jnp.concatenate([_stack_heads(q, kv, lambda hl, kv=kv: kv) for kv in range(NSA_KV_HEADS)],
                                      axis=0).astype(BF16)
        reset()

    q2 = q2_ref[...]
    k_t = jnp.concatenate([r[...] for r in kpages], axis=1).astype(BF16)
    v_t = jnp.concatenate([r[...] for r in vpages], axis=1).astype(BF16)
    sel_c = sel_rows(c)
    chosen = jnp.concatenate(
        [jnp.where(lane < SLC_BLOCK, jnp.broadcast_to(sel_c[:, 2 * i:2 * i + 1], (rows, LANES)),
                   jnp.broadcast_to(sel_c[:, 2 * i + 1:2 * i + 2], (rows, LANES)))
         for i in range(nk // LANES)], axis=1)
    kpos = c * nk + _lane_iota((1, nk))
    bias = jnp.where((chosen > 0.5) & (kpos <= P + t), 0.0, NEG_INF)
    update(_dot(q2, k_t) + bias, lambda p: _dot_nt(p, v_t))

    @pl.when(c == last)
    def _():
        pad_rows = lambda x: jnp.concatenate([x, jnp.zeros((LANES - TS, LANES), F32)], axis=0).astype(BF16)
        own = (lane >> 6) == (_row_iota((rows, 1)) // (Q_PER_KV * TS))
        jn = P // SLC_BLOCK
        sel_n = sel_rows(jn // BLOCKS_PER_STEP)[:, jn % BLOCKS_PER_STEP:jn % BLOCKS_PER_STEP + 1]
        bias_n = jnp.where((sel_n > 0.5) & (lane <= t) & (lane < TS), 0.0, NEG_INF)
        vn = pad_rows(vsn_ref[...])
        update(_dot_nt(q2, pad_rows(ksn_ref[...])) + bias_n, lambda p: _dot(p, vn))
        o_slc = result()
        reset()
        kw_t, vw_t = kwin_ref[...], vwin_ref[...]
        iw = _lane_iota((1, wb))
        rel = t + wb - iw
        bias_b = jnp.where((rel >= 0) & (rel < WINDOW), 0.0, NEG_INF)
        vw_tb = vw_t.astype(BF16)
        update(_dot(q2, kw_t.astype(BF16)) + bias_b, lambda p: _dot_nt(p, vw_tb))
        bias_w = jnp.where((lane <= t) & (lane < TS), 0.0, NEG_INF)
        vwn = pad_rows(vwn_ref[...])
        update(_dot_nt(q2, pad_rows(kwn_ref[...])) + bias_w, lambda p: _dot(p, vwn))
        o_win = result()
        gs = jax.nn.sigmoid(g_ref[...])
        gcol = lambda comp: jnp.concatenate(
            [gs[:, 3 * hq + comp:3 * hq + comp + 1] for hq in range(NSA_HEADS)], axis=0)
        ocmp = jnp.concatenate([ocmp_ref[kv] for kv in range(NSA_KV_HEADS)], axis=0)
        o = gcol(0) * ocmp + gcol(1) * o_slc + gcol(2) * o_win
        o = jnp.where(own, o, 0.0)
        for tile in range(NSA_HEADS // 2):
            a = o[(2 * tile) * TS:(2 * tile + 1) * TS]
            b = o[(2 * tile + 1) * TS:(2 * tile + 2) * TS]
            kv = (2 * tile) // Q_PER_KV
            a = a if kv == 0 else pltpu.roll(a, HEAD_DIM, 1)
            b = b if kv == 1 else pltpu.roll(b, HEAD_DIM, 1)
            o_ref[:, tile * LANES:(tile + 1) * LANES] = a + b
        kwout_ref[0:wb - TS, :] = kw_t.T[TS:wb, :]
        kwout_ref[wb - TS:wb, :] = kwn_ref[...]
        vwout_ref[0:wb - TS, :] = vw_t.T[TS:wb, :]
        vwout_ref[wb - TS:wb, :] = vwn_ref[...]


def _nsa_sample_slc(page_table, q, sel, ocmp, g, ksn, vsn, kwn, vwn, kwin_t, vwin_t, ck_t, cv_t):
    S, TS, _ = q.shape
    n_pages = page_table.shape[1]
    P = n_pages * PAGE_SIZE
    wb = kwin_t.shape[2]
    rows = NSA_KV_HEADS * Q_PER_KV * TS
    steps = n_pages // PAGES_PER_STEP
    pages = _page_specs()
    per_seq = lambda shape: pl.BlockSpec((None,) + shape, lambda b, c, pt, nd=len(shape): (b,) + (0,) * nd)
    grid_spec = pltpu.PrefetchScalarGridSpec(
        num_scalar_prefetch=1,
        grid=(S, steps),
        in_specs=[per_seq((TS, NSA_WIDTH)), per_seq(sel.shape[1:]), per_seq(ocmp.shape[1:]), per_seq((TS, LANES)),
                  per_seq((TS, KV_WIDTH)), per_seq((TS, KV_WIDTH)), per_seq((TS, KV_WIDTH)), per_seq((TS, KV_WIDTH)),
                  per_seq((KV_WIDTH, wb)), per_seq((KV_WIDTH, wb))] + pages + pages,
        out_specs=[per_seq((TS, NSA_WIDTH)), per_seq((wb, KV_WIDTH)), per_seq((wb, KV_WIDTH))],
        scratch_shapes=[pltpu.VMEM((rows, LANES), BF16)] + [pltpu.VMEM((rows, LANES), F32)] * 3,
    )
    return pl.pallas_call(
        functools.partial(_nsa_sample_slc_body, P=P, TS=TS),
        grid_spec=grid_spec,
        out_shape=[jax.ShapeDtypeStruct((S, TS, NSA_WIDTH), F32),
                   jax.ShapeDtypeStruct((S, wb, KV_WIDTH), F32), jax.ShapeDtypeStruct((S, wb, KV_WIDTH), F32)],
        compiler_params=_params(("parallel", "arbitrary")),
        name="nsa_sample_slc",
    )(page_table, q, sel, ocmp, g, ksn, vsn, kwn, vwn, kwin_t, vwin_t,
      *([ck_t] * PAGES_PER_STEP), *([cv_t] * PAGES_PER_STEP))


def _merge_body(x_ref, glu_ref, o_ref, nw_ref, wg_ref, wup_ref, wnsa_ref, wout_ref, out_ref):
    x = x_ref[...]
    gate = jax.nn.sigmoid(_dot(_rms(x, nw_ref[...]).astype(BF16), wg_ref[...]))
    s5_out = _dot(glu_ref[...].astype(BF16), wup_ref[...])
    nsa_out = _dot(o_ref[...].astype(BF16), wnsa_ref[...])
    merged = gate[:, :D_MODEL] * s5_out + gate[:, D_MODEL:] * nsa_out
    out_ref[...] = x + _dot(merged.astype(BF16), wout_ref[...])


def _merge(x, glu, o, nw, wg, wup, wnsa, wout, tt):
    B, T, _ = x.shape
    row = lambda w: pl.BlockSpec((None, tt, w), lambda b, i: (b, i, 0))
    return pl.pallas_call(
        _merge_body,
        grid=(B, T // tt),
        in_specs=[row(D_MODEL), row(S5_WIDTH), row(NSA_WIDTH), _const_spec(nw.shape), _const_spec(wg.shape),
                  _const_spec(wup.shape), _const_spec(wnsa.shape), _const_spec(wout.shape)],
        out_specs=row(D_MODEL),
        out_shape=jax.ShapeDtypeStruct((B, T, D_MODEL), F32),
        compiler_params=_params(("parallel", "parallel")),
        name="merge",
    )(x, glu, o, nw, wg, wup, wnsa, wout)


def _ffn_body(x_ref, nw_ref, w1_ref, w2_ref, out_ref):
    x = x_ref[...]
    a = jnp.maximum(_dot(_rms(x, nw_ref[...]).astype(BF16), w1_ref[...]), 0.0)
    out_ref[...] = x + _dot((a * a).astype(BF16), w2_ref[...])


def _ffn(x, nw, w1, w2, tt):
    B, T, _ = x.shape
    row = pl.BlockSpec((None, tt, D_MODEL), lambda b, i: (b, i, 0))
    return pl.pallas_call(
        _ffn_body,
        grid=(B, T // tt),
        in_specs=[row, _const_spec(nw.shape), _const_spec(w1.shape), _const_spec(w2.shape)],
        out_specs=row,
        out_shape=jax.ShapeDtypeStruct((B, T, D_MODEL), F32),
        compiler_params=_params(("parallel", "parallel")),
        name="ffn",
    )(x, nw, w1, w2)


def _rope_tables(pos):
    half = HEAD_DIM // 2
    inv = ROPE_THETA ** (-jnp.arange(half, dtype=F32) / half)
    ang = pos.astype(F32)[:, None] * inv[None, :]
    cos, sin = jnp.cos(ang), jnp.sin(ang)
    return (jnp.concatenate([cos, cos, cos, cos], axis=-1),
            jnp.concatenate([-sin, sin, -sin, sin], axis=-1))


def _block_diag(w):
    G, a, b = w.shape
    return jnp.einsum('gab,gk->gakb', w, jnp.eye(G, dtype=w.dtype)).reshape(G * a, G * b)


def _cmp_weights(pe, w1, w2):
    eye = jnp.eye(NSA_KV_HEADS, dtype=F32)
    half = CMP_LEN // 2
    big = lambda w: jnp.einsum('rdf,hk->rhdkf', w, eye).reshape(half * KV_WIDTH, NSA_KV_HEADS * CMP_HIDDEN)
    w1b = jnp.concatenate([big(w1[:half]), big(w1[half:])], axis=1).astype(BF16)
    w2d = jnp.concatenate([w2, w2], axis=1).astype(BF16)
    flat = lambda p: jnp.broadcast_to(p[:, None, :], (half, NSA_KV_HEADS, HEAD_DIM)).reshape(1, half * KV_WIDTH)
    pe8 = jnp.concatenate([flat(pe[:half]), flat(pe[half:]), jnp.zeros((SUBLANES - 2, half * KV_WIDTH), F32)], axis=0)
    return w1b, w2d, pe8


def kernel(x_prompt, x_sample, cache_k_cmp, cache_v_cmp, cache_k_slc, cache_v_slc, state_k_win, state_v_win, state_s5_re, state_s5_im, page_table, norm_mix_w, w_in, s5_lam_re, s5_lam_im, s5_log_dt, s5_b_re, s5_b_im, s5_c_re, s5_c_im, s5_d, s5_w_glu, s5_w_up, q_norm_w, k_norm_w, cmp_pe_k, cmp_wk1, cmp_wk2, cmp_pe_v, cmp_wv1, cmp_wv2, nsa_w_up, w_out, norm_ffn_w, w_ffn1, w_ffn2):
    B, T, _ = x_prompt.shape
    S, TS, _ = x_sample.shape
    n_pages = page_table.shape[1]
    P = n_pages * PAGE_SIZE
    n_pool = cache_k_cmp.shape[1]
    assert norm_mix_w.shape[0] == 1 and B % SUBLANES == 0 and S % SUBLANES == 0
    assert TS < CMP_STRIDE and P % SLC_BLOCK == 0 and n_pages % PAGES_PER_STEP == 0

    w = w_in[0]
    cols = lambda a, n: w[:, a:a + n]
    w_a = jnp.concatenate(
        [cols(0, 512), cols(512, 512), cols(1024, 128), cols(1280, 128), cols(1536, 128),
         cols(1152, 128), cols(1408, 128), cols(1664, 128), cols(1792, N_GATE),
         jnp.zeros((D_MODEL, W_A_COLS - 1792 - N_GATE), F32)], axis=1).astype(BF16)
    w_g = w[:, OFF_GMIX:].astype(BF16)
    nw_mix = norm_mix_w[0][None, :]
    qkw = jnp.concatenate([jnp.tile(q_norm_w[0], NSA_HEADS)]
                          + [jnp.tile(k_norm_w[0, i], NSA_KV_HEADS) for i in range(3)])[None, :]
    li = np.arange(LANES)
    seg = jnp.asarray((li[:, None] // HEAD_DIM) == (li[None, :] // HEAD_DIM), BF16)
    lre, lim = s5_lam_re[0].reshape(1, S5_LANES), s5_lam_im[0].reshape(1, S5_LANES)
    ldt = jnp.repeat(s5_log_dt[0], S5_STATE)[None, :]
    bre = _block_diag(jnp.swapaxes(s5_b_re[0], 1, 2))
    bim = _block_diag(jnp.swapaxes(s5_b_im[0], 1, 2))
    cre = _block_diag(jnp.swapaxes(s5_c_re[0], 1, 2)).astype(BF16)
    cim = _block_diag(jnp.swapaxes(s5_c_im[0], 1, 2)).astype(BF16)
    s5d = s5_d[0][None, :]
    wglu = s5_w_glu[0].astype(BF16)
    wup = s5_w_up[0].astype(BF16)
    cw = _cmp_weights(cmp_pe_k[0], cmp_wk1[0], cmp_wk2[0]) + _cmp_weights(cmp_pe_v[0], cmp_wv1[0], cmp_wv2[0])
    wnsa = nsa_w_up[0].astype(BF16)
    wout = w_out[0].astype(BF16)
    nw_ffn = norm_ffn_w[0][None, :]
    w1 = w_ffn1[0].astype(BF16)
    w2 = w_ffn2[0].astype(BF16)

    def trunk(x, glu, o, tt):
        return _ffn(_merge(x, glu, o, nw_mix, w_g, wup, wnsa, wout, tt), nw_ffn, w1, w2, tt)

    cos_p, sin_p = _rope_tables(jnp.arange(T, dtype=jnp.int32))
    u, q, kc, ks, kw, vc, vs, vw, g = _inproj(x_prompt, cos_p, sin_p, nw_mix, w_a, qkw, seg, 512)
    zeros_st = jnp.zeros((B, S5_LANES), F32)
    glu_tm, sre_p, sim_p = _s5(jnp.swapaxes(u, 0, 1), zeros_st, zeros_st, lre, lim, ldt, bre, bim, cre, cim, s5d, wglu, 64)
    o_p = _nsa_prompt(q, kc, vc, ks, vs, kw, vw, g, cw, 256)
    y_prompt = trunk(x_prompt, jnp.swapaxes(glu_tm, 0, 1), o_p, 512)

    n_s = S * TS
    cos_s, sin_s = _rope_tables(P + jnp.arange(TS, dtype=jnp.int32))
    cos_s, sin_s = jnp.tile(cos_s, (S, 1)), jnp.tile(sin_s, (S, 1))
    xs = x_sample.reshape(1, n_s, D_MODEL)
    us, qs, kcs, kss, kws, vcs, vss, vws, gs = _inproj(xs, cos_s, sin_s, nw_mix, w_a, qkw, seg, n_s)
    seq = lambda a: a.reshape(S, TS, a.shape[-1])
    u_tm = jnp.swapaxes(seq(us), 0, 1)
    glu_s_tm, sre_s, sim_s = _s5(u_tm, state_s5_re[0].reshape(S, S5_LANES), state_s5_im[0].reshape(S, S5_LANES),
                                 lre, lim, ldt, bre, bim, cre, cim, s5d, wglu, TS)
    pool = lambda c: jnp.transpose(c[0], (0, 2, 3, 1)).reshape(n_pool, KV_WIDTH, PAGE_SIZE)
    ocmp_s, sel_s = _nsa_sample_cmp(page_table, seq(qs), pool(cache_k_cmp), pool(cache_v_cmp), cw)
    win = lambda s: jnp.transpose(s[0], (0, 2, 3, 1)).reshape(S, KV_WIDTH, s.shape[2])
    o_s, kwin_s, vwin_s = _nsa_sample_slc(page_table, seq(qs), sel_s, ocmp_s, seq(gs), seq(kss), seq(vss), seq(kws),
                                          seq(vws), win(state_k_win), win(state_v_win),
                                          pool(cache_k_slc), pool(cache_v_slc))
    glu_s = jnp.swapaxes(glu_s_tm, 0, 1).reshape(1, n_s, S5_WIDTH)
    y_sample = trunk(xs, glu_s, o_s.reshape(1, n_s, NSA_WIDTH), n_s).reshape(S, TS, D_MODEL)

    heads_p = lambda a: a.reshape(1, B, a.shape[1], NSA_KV_HEADS, HEAD_DIM)
    heads_s = lambda a: a.reshape(1, S, -1, NSA_KV_HEADS, HEAD_DIM)
    keep = min(WINDOW, T)
    st_p = lambda a: a.reshape(1, B, S5_GROUPS, S5_STATE)
    st_s = lambda a: a.reshape(1, S, S5_GROUPS, S5_STATE)
    return (y_prompt, y_sample,
            heads_p(kc), heads_s(kcs), heads_p(vc), heads_s(vcs),
            heads_p(ks), heads_s(kss), heads_p(vs), heads_s(vss),
            heads_p(kw[:, T - keep:]), heads_s(kwin_s), heads_p(vw[:, T - keep:]), heads_s(vwin_s),
            st_p(sre_p), st_s(sre_s), st_p(sim_p), st_s(sim_s))
```

```python
import functools

import numpy as np
import jax
import jax.numpy as jnp
from jax import lax
from jax.experimental import pallas as pl
from jax.experimental.pallas import tpu as pltpu

F32 = jnp.float32
BF16 = jnp.bfloat16

D_MODEL = 1024
PAGE_SIZE = 128
S5_WIDTH = 512
S5_GROUPS = 32
S5_STATE = 64
S5_LANES = S5_GROUPS * S5_STATE
HEAD_DIM = 64
NSA_HEADS = 8
NSA_KV_HEADS = 2
Q_PER_KV = 4
NSA_WIDTH = 512
KV_WIDTH = 128
CMP_LEN = 32
CMP_STRIDE = 16
CMP_HIDDEN = 128
SLC_BLOCK = 64
SLC_TOPK = 16
WINDOW = 512
ROPE_THETA = 10000.0
EPS = 1e-6
NEG_INF = -1e30
FORCED = 1e9
SCALE = HEAD_DIM ** -0.5
N_GATE = 3 * NSA_HEADS
OFF_GMIX = 2 * 512 + 6 * KV_WIDTH + N_GATE
W_A_COLS = 1920

LANES = 128
SUBLANES = 8
VMEM_LIMIT = 56 * 1024 * 1024
BOUND_SLACK = 1.02
MAX_SHIFT_GAP = 60.0


def _params(sem):
    return pltpu.CompilerParams(dimension_semantics=sem, vmem_limit_bytes=VMEM_LIMIT)


def _const_spec(shape):
    nd = len(shape)
    return pl.BlockSpec(shape, lambda *_: (0,) * nd)


def _dot(a, b):
    return jnp.dot(a, b, preferred_element_type=F32)


def _dot_nt(a, b):
    return lax.dot_general(a, b, (((1,), (1,)), ((), ())), preferred_element_type=F32)


def _rms(x, w):
    r = lax.rsqrt(jnp.mean(x * x, axis=-1, keepdims=True) + EPS)
    return x * r * w


def _lane_iota(shape):
    return lax.broadcasted_iota(jnp.int32, shape, len(shape) - 1)


def _row_iota(shape):
    return lax.broadcasted_iota(jnp.int32, shape, len(shape) - 2)


def _split3(x):
    hi = x.astype(BF16)
    r1 = x - hi.astype(F32)
    mid = r1.astype(BF16)
    lo = (r1 - mid.astype(F32)).astype(BF16)
    return hi, mid, lo


def _inproj_body(x_ref, nw_ref, w_ref, qkw_ref, seg_ref, cos_ref, sin_ref,
                 u_ref, q_ref, kc_ref, ks_ref, kw_ref, vc_ref, vs_ref, vw_ref, g_ref):
    xn = _rms(x_ref[...], nw_ref[...]).astype(BF16)
    z = _dot(xn, w_ref[...])
    u_ref[...] = z[:, :512]
    cos = cos_ref[...]
    sin = sin_ref[...]
    seg = seg_ref[...]
    first_half = (_lane_iota((1, LANES)) % HEAD_DIM) < (HEAD_DIM // 2)

    def norm_rope(t, w):
        ss = _dot((t * t).astype(BF16), seg)
        t = t * lax.rsqrt(ss * (1.0 / HEAD_DIM) + EPS) * w
        rot = jnp.where(first_half, pltpu.roll(t, LANES - HEAD_DIM // 2, 1), pltpu.roll(t, HEAD_DIM // 2, 1))
        return t * cos + rot * sin

    for j in range(4):
        q_ref[:, j * LANES:(j + 1) * LANES] = norm_rope(
            z[:, 512 + j * LANES:512 + (j + 1) * LANES], qkw_ref[:, j * LANES:(j + 1) * LANES])
    for j, r in enumerate((kc_ref, ks_ref, kw_ref)):
        r[...] = norm_rope(z[:, 1024 + j * LANES:1024 + (j + 1) * LANES],
                           qkw_ref[:, 512 + j * LANES:512 + (j + 1) * LANES])
    vc_ref[...] = z[:, 1408:1536]
    vs_ref[...] = z[:, 1536:1664]
    vw_ref[...] = z[:, 1664:1792]
    g_ref[...] = z[:, 1792:1920]


def _inproj(x, cos_t, sin_t, nw, w_a, qkw, seg, tt):
    B, T, _ = x.shape
    row = lambda w: pl.BlockSpec((None, tt, w), lambda b, i: (b, i, 0))
    tab = pl.BlockSpec((tt, LANES), lambda b, i: (i, 0))
    kv = jax.ShapeDtypeStruct((B, T, KV_WIDTH), F32)
    out_shape = [jax.ShapeDtypeStruct((B, T, 512), F32), jax.ShapeDtypeStruct((B, T, 512), F32)] + [kv] * 7
    return pl.pallas_call(
        _inproj_body,
        grid=(B, T // tt),
        in_specs=[row(D_MODEL), _const_spec((1, D_MODEL)), _const_spec((D_MODEL, W_A_COLS)),
                  _const_spec((1, 896)), _const_spec((LANES, LANES)), tab, tab],
        out_specs=[row(512), row(512)] + [row(KV_WIDTH)] * 7,
        out_shape=out_shape,
        compiler_params=_params(("parallel", "parallel")),
        name="inproj",
    )(x, nw, w_a, qkw, seg, cos_t, sin_t)


def _s5_body(u_ref, sre0_ref, sim0_ref, lre_ref, lim_ref, ldt_ref, bre_ref, bim_ref, cre_ref, cim_ref,
             d_ref, wglu_ref, out_ref, sre_ref, sim_ref,
             bbar_ref, a_ref, st_ref, xr_ref, xi_ref, *, tc):
    c = pl.program_id(1)
    rows = tc * SUBLANES

    @pl.when(c == 0)
    def _():
        lr, li = lre_ref[...], lim_ref[...]
        dt = jnp.exp(ldt_ref[...])
        mag = jnp.exp(lr * dt)
        ab_re, ab_im = mag * jnp.cos(li * dt), mag * jnp.sin(li * dt)
        den = lr * lr + li * li
        f_re = ((ab_re - 1.0) * lr + ab_im * li) / den
        f_im = (ab_im * lr - (ab_re - 1.0) * li) / den
        bre, bim = bre_ref[...], bim_ref[...]
        bbar_ref[:, :S5_LANES] = (f_re * bre - f_im * bim).astype(BF16)
        bbar_ref[:, S5_LANES:] = (f_re * bim + f_im * bre).astype(BF16)
        a_ref[0] = jnp.broadcast_to(ab_re, (SUBLANES, S5_LANES))
        a_ref[1] = jnp.broadcast_to(ab_im, (SUBLANES, S5_LANES))
        st_ref[0] = sre0_ref[...]
        st_ref[1] = sim0_ref[...]

    u = u_ref[...].reshape(rows, S5_WIDTH)
    ub = u.astype(BF16)
    hw, hl = S5_WIDTH // 2, S5_LANES // 2
    for n in range(2):
        xr_ref[:, n * hl:(n + 1) * hl] = _dot(ub[:, n * hw:(n + 1) * hw], bbar_ref[n * hw:(n + 1) * hw, n * hl:(n + 1) * hl])
        xi_ref[:, n * hl:(n + 1) * hl] = _dot(ub[:, n * hw:(n + 1) * hw],
                                              bbar_ref[n * hw:(n + 1) * hw, S5_LANES + n * hl:S5_LANES + (n + 1) * hl])

    slab = 512
    for lc in range(S5_LANES // slab):
        sl = slice(lc * slab, (lc + 1) * slab)
        ar, ai = a_ref[0, :, sl], a_ref[1, :, sl]

        def step(t, carry, sl=sl, ar=ar, ai=ai):
            sr, si = carry
            r0 = pl.multiple_of(t * SUBLANES, SUBLANES)
            nr = ar * sr - ai * si + xr_ref[pl.ds(r0, SUBLANES), sl]
            ni = ar * si + ai * sr + xi_ref[pl.ds(r0, SUBLANES), sl]
            xr_ref[pl.ds(r0, SUBLANES), sl] = nr
            xi_ref[pl.ds(r0, SUBLANES), sl] = ni
            return nr, ni

        sr, si = lax.fori_loop(0, tc, step, (st_ref[0, :, sl], st_ref[1, :, sl]), unroll=min(tc, 8))
        st_ref[0, :, sl] = sr
        st_ref[1, :, sl] = si

    y = jnp.concatenate(
        [_dot(xr_ref[:, n * hl:(n + 1) * hl].astype(BF16), cre_ref[n * hl:(n + 1) * hl, n * hw:(n + 1) * hw])
         - _dot(xi_ref[:, n * hl:(n + 1) * hl].astype(BF16), cim_ref[n * hl:(n + 1) * hl, n * hw:(n + 1) * hw])
         for n in range(2)], axis=1) + d_ref[...] * u
    yg = jax.nn.gelu(y)
    glu = yg * jax.nn.sigmoid(_dot(yg.astype(BF16), wglu_ref[...]))
    out_ref[...] = glu.reshape(tc, SUBLANES, S5_WIDTH)

    @pl.when(c == pl.num_programs(1) - 1)
    def _():
        sre_ref[...] = st_ref[0]
        sim_ref[...] = st_ref[1]


def _s5(u_tm, sre0, sim0, lre, lim, ldt, bre, bim, cre, cim, d, wglu, tc):
    T, B, _ = u_tm.shape
    st_spec = pl.BlockSpec((SUBLANES, S5_LANES), lambda b, c: (b, 0))
    blk = pl.BlockSpec((tc, SUBLANES, S5_WIDTH), lambda b, c: (c, b, 0))
    st = jax.ShapeDtypeStruct((B, S5_LANES), F32)
    return pl.pallas_call(
        functools.partial(_s5_body, tc=tc),
        grid=(B // SUBLANES, T // tc),
        in_specs=[blk, st_spec, st_spec,
                  _const_spec((1, S5_LANES)), _const_spec((1, S5_LANES)), _const_spec((1, S5_LANES)),
                  _const_spec((S5_WIDTH, S5_LANES)), _const_spec((S5_WIDTH, S5_LANES)),
                  _const_spec((S5_LANES, S5_WIDTH)), _const_spec((S5_LANES, S5_WIDTH)),
                  _const_spec((1, S5_WIDTH)), _const_spec((S5_WIDTH, S5_WIDTH))],
        out_specs=[blk, st_spec, st_spec],
        out_shape=[jax.ShapeDtypeStruct((T, B, S5_WIDTH), F32), st, st],
        scratch_shapes=[pltpu.VMEM((S5_WIDTH, 2 * S5_LANES), BF16),
                        pltpu.VMEM((2, SUBLANES, S5_LANES), F32),
                        pltpu.VMEM((2, SUBLANES, S5_LANES), F32),
                        pltpu.VMEM((tc * SUBLANES, S5_LANES), F32),
                        pltpu.VMEM((tc * SUBLANES, S5_LANES), F32)],
        compiler_params=_params(("parallel", "arbitrary")),
        name="s5",
    )(u_tm, sre0, sim0, lre, lim, ldt, bre, bim, cre, cim, d, wglu)


def _head_lanes(x, half):
    return jnp.where((_lane_iota((1, LANES)) >> 6) == half, x, 0.0)


def _masked_softmax_parts(s, mask):
    s = jnp.where(mask, s, NEG_INF)
    m = jnp.max(s, axis=-1, keepdims=True)
    p = jnp.where(mask, jnp.exp(s - m), 0.0)
    l = jnp.maximum(jnp.sum(p, axis=-1, keepdims=True), 1e-30)
    return p, l


def _cmp_hidden(h, w1_ref, pe_ref):
    m = h.shape[0]
    pe = pe_ref[...].astype(BF16)
    bias = _dot(pe, w1_ref[:, :256])[0:1] + _dot(pe, w1_ref[:, 256:])[1:2]
    return jax.nn.gelu(h[:, :256] + pltpu.roll(h[:, 256:], m - 1, 0) + bias)


def _overlap(n_rows, n_cols, transpose):
    shape = (n_cols, n_rows) if transpose else (n_rows, n_cols)
    n = _row_iota(shape) if not transpose else _lane_iota(shape)
    j = _lane_iota(shape) if not transpose else _row_iota(shape)
    lo = jnp.maximum(n * CMP_STRIDE, j * SLC_BLOCK)
    hi = jnp.minimum(n * CMP_STRIDE + CMP_LEN, j * SLC_BLOCK + SLC_BLOCK)
    return (jnp.maximum(hi - lo, 0).astype(F32) * (1.0 / CMP_LEN)).astype(BF16)


def _ones_row_rows(n):
    return jnp.where(_row_iota((HEAD_DIM, n)) == 0, 1.0, 0.0)


def _shifted_update_t(m_ref, acc_ref, h, s_t, v_t):
    tk, tq = s_t.shape
    m_ref[h] = jnp.maximum(m_ref[h], jnp.max(s_t.reshape(tk // SUBLANES, SUBLANES, tq), axis=0))
    acc_ref[h] = acc_ref[h] + _dot(v_t, jnp.exp(s_t).astype(BF16))


def _online_update_t(m_ref, acc_ref, h, s_t, v_t):
    m_old = m_ref[h]
    m_new = jnp.maximum(m_old, jnp.max(s_t, axis=0, keepdims=True))
    p_t = jnp.exp(s_t - m_new[0:1])
    alpha = jnp.exp(m_old - m_new)
    acc_ref[h] = alpha[0:1] * acc_ref[h] + _dot(v_t, p_t.astype(BF16))
    m_ref[h] = m_new


def _softmax_result_t(m_ref, acc_ref, h):
    acc = acc_ref[h]
    out = acc[0:HEAD_DIM] / jnp.maximum(acc[HEAD_DIM:HEAD_DIM + 1], 1e-30)
    return jnp.where(jnp.max(m_ref[h], axis=0, keepdims=True) > 0.5 * NEG_INF, out, 0.0)


def _key_rows(k, kv):
    lane = _lane_iota((1, LANES))
    kk = jnp.where(lane < HEAD_DIM, jnp.where(kv == 0, k, pltpu.roll(k, HEAD_DIM, 1)), 0.0)
    norm2 = jnp.max(jnp.sum(kk * kk, axis=1, keepdims=True), axis=0, keepdims=True)
    return jnp.where(lane == HEAD_DIM, 1.0, kk).astype(BF16), jnp.sqrt(norm2)


def _nsa_prompt_body(q_ref, kc16_ref, vc16_ref, ks_ref, kw_ref, vs_ref, vw_ref, g_ref,
                     w1k_ref, w2k_ref, pek_ref, w1v_ref, w2v_ref, pev_ref,
                     o_ref,
                     ksa_ref, kwa_ref, vst_ref, vwt_ref, kcmp_ref, vcmpt_ref, knorm_ref,
                     qc_ref, qs_ref, qw_ref, selb_ref, ms_ref, accs_ref, mw_ref, accw_ref, *, T, TQ):
    kv = pl.program_id(1)
    qi = pl.program_id(2)
    m_cmp = T // CMP_STRIDE
    n_cmp = m_cmp - 1
    n_slc = T // SLC_BLOCK
    topk = min(SLC_TOPK, n_slc)
    n_kt = T // TQ
    lane = _lane_iota((1, LANES))

    @pl.when(qi == 0)
    def _():
        ksa_ref[...], ns = _key_rows(ks_ref[...], kv)
        kwa_ref[...], nw = _key_rows(kw_ref[...], kv)
        knorm_ref[0] = jnp.broadcast_to(ns, (SUBLANES, LANES))
        knorm_ref[1] = jnp.broadcast_to(nw, (SUBLANES, LANES))
        ones = _ones_row_rows(TQ).astype(BF16)
        for kt in range(n_kt):
            for src, dst in ((vs_ref, vst_ref), (vw_ref, vwt_ref)):
                vt = src[kt * TQ:(kt + 1) * TQ, :].T
                dst[kt, 0:HEAD_DIM, :] = jnp.where(kv == 0, vt[0:HEAD_DIM], vt[HEAD_DIM:]).astype(BF16)
                dst[kt, HEAD_DIM:, :] = ones

        def compressed(x16_ref, w1_ref, pe_ref, w2_ref):
            hid = _cmp_hidden(_dot(x16_ref[...].astype(BF16), w1_ref[...]), w1_ref, pe_ref)
            hid = jnp.where(kv == 0, hid[:, :LANES], hid[:, LANES:])
            return _dot(hid.astype(BF16), w2_ref[...])

        kcmp_ref[...] = compressed(kc16_ref, w1k_ref, pek_ref, w2k_ref).astype(BF16)
        vcmp_t = compressed(vc16_ref, w1v_ref, pev_ref, w2v_ref).T
        vcmpt_ref[0:HEAD_DIM, :] = vcmp_t[0:HEAD_DIM].astype(BF16)
        vcmpt_ref[HEAD_DIM:, :] = jnp.zeros((HEAD_DIM, m_cmp), BF16)

    r0 = pl.multiple_of(qi * TQ, TQ)
    for pp in range(2):
        qt = q_ref[pl.ds(r0, TQ), pp * LANES:(pp + 1) * LANES] * SCALE
        for e in range(2):
            qh = jnp.where(lane < HEAD_DIM, qt if e == 0 else pltpu.roll(qt, HEAD_DIM, 1), 0.0)
            qc_ref[2 * pp + e] = qh.astype(BF16)
            qnorm = jnp.sqrt(jnp.sum(qh * qh, axis=1, keepdims=True))
            for dst, which in ((qs_ref, 0), (qw_ref, 1)):
                bound = BOUND_SLACK * qnorm * knorm_ref[which][0:1, 0:1]
                dst[(2 * pp + e) * TQ:(2 * pp + e + 1) * TQ, :] = jnp.where(lane == HEAD_DIM, -bound, qh).astype(BF16)
    qpos = r0 + _lane_iota((1, TQ))

    n = _row_iota((m_cmp, TQ))
    mask_c = (n * CMP_STRIDE + CMP_LEN - 1 <= qpos) & (n < n_cmp)
    pg = jnp.zeros((m_cmp, TQ), F32)
    o_cmp = []
    for h in range(Q_PER_KV):
        s_t = jnp.where(mask_c, _dot_nt(kcmp_ref[...], qc_ref[h]), NEG_INF)
        p_t = jnp.where(mask_c, jnp.exp(s_t - jnp.max(s_t, axis=0, keepdims=True)), 0.0)
        inv_l = 1.0 / jnp.maximum(jnp.sum(p_t, axis=0, keepdims=True), 1e-30)
        pg = pg + p_t * inv_l
        o_cmp.append(_dot(vcmpt_ref[...], p_t.astype(BF16))[0:HEAD_DIM] * inv_l)

    ovt = _overlap(m_cmp, n_slc, transpose=True)
    imp = sum(_dot(ovt, part) for part in _split3(pg))
    j = _row_iota((n_slc, TQ))
    qblk = qpos // SLC_BLOCK
    valid = j <= qblk
    forced = (j == 0) | (j == qblk) | (j == qblk - 1)
    score = jnp.where(forced, FORCED, jnp.where(valid, imp, NEG_INF))
    rank = jnp.zeros((n_slc, TQ), jnp.int32)
    for i in range(n_slc):
        si = score[i:i + 1, :]
        beats = (si > score) | ((si == score) & (i < j))
        rank = rank + beats.astype(jnp.int32)
    def reset():
        for m_r, acc_r in ((ms_ref, accs_ref), (mw_ref, accw_ref)):
            m_r[...] = jnp.full(m_r.shape, NEG_INF, F32)
            acc_r[...] = jnp.zeros(acc_r.shape, F32)

    sel_bias = jnp.where((rank < topk) & valid, 0.0, NEG_INF)
    for jb in range(n_slc):
        selb_ref[jb] = jnp.broadcast_to(sel_bias[jb:jb + 1], (SUBLANES, TQ))
    not_after = _row_iota((TQ, 1)) <= _lane_iota((1, TQ))
    bias_diag = jnp.where(not_after, 0.0, NEG_INF)
    bias_far = jnp.where(not_after, NEG_INF, 0.0)
    blocks_per_tile = TQ // SLC_BLOCK
    win_tiles = WINDOW // TQ

    def both_branches(update):
        def slc_tile(kb, bias_extra):
            k0 = pl.multiple_of(kb * TQ, TQ)
            rows = []
            for jb in range(blocks_per_tile):
                rows += [selb_ref[kb * blocks_per_tile + jb]] * (SLC_BLOCK // SUBLANES)
            bias = jnp.concatenate(rows, axis=0)
            if bias_extra is not None:
                bias = bias + bias_extra
            s_all = _dot_nt(ksa_ref[pl.ds(k0, TQ), :], qs_ref[...])
            for h in range(Q_PER_KV):
                update(ms_ref, accs_ref, h, s_all[:, h * TQ:(h + 1) * TQ] + bias, vst_ref[kb])

        def slc_body(kb, carry):
            slc_tile(kb, None)
            return carry

        lax.fori_loop(0, qi, slc_body, 0)
        slc_tile(qi, bias_diag)

        for d in range(win_tiles + 1):
            kb = jnp.maximum(qi - d, 0)
            k0 = pl.multiple_of(kb * TQ, TQ)
            exists = jnp.where(qi >= d, 0.0, NEG_INF)
            bias = exists + (bias_diag if d == 0 else bias_far if d == win_tiles else 0.0)
            s_all = _dot_nt(kwa_ref[pl.ds(k0, TQ), :], qw_ref[...])
            for h in range(Q_PER_KV):
                update(mw_ref, accw_ref, h, s_all[:, h * TQ:(h + 1) * TQ] + bias, vwt_ref[kb])

    reset()
    both_branches(_shifted_update_t)

    worst = jnp.full((1, TQ), 0.0, F32)
    for m_r in (ms_ref, mw_ref):
        for h in range(Q_PER_KV):
            top = jnp.max(m_r[h], axis=0, keepdims=True)
            worst = jnp.minimum(worst, jnp.where(top > 0.5 * NEG_INF, top, 0.0))
    redo = jnp.min(worst) < -MAX_SHIFT_GAP

    @pl.when(redo)
    def _():
        reset()
        both_branches(_online_update_t)

    g = g_ref[pl.ds(r0, TQ), :]
    gs_t = jax.nn.sigmoid(jnp.where(kv == 0, g, pltpu.roll(g, LANES - 3 * Q_PER_KV, 1))).T
    for pp in range(2):
        halves = []
        for e in range(2):
            h = 2 * pp + e
            halves.append(gs_t[3 * h:3 * h + 1] * o_cmp[h]
                          + gs_t[3 * h + 1:3 * h + 2] * _softmax_result_t(ms_ref, accs_ref, h)
                          + gs_t[3 * h + 2:3 * h + 3] * _softmax_result_t(mw_ref, accw_ref, h))
        o_ref[pl.ds(r0, TQ), pp * LANES:(pp + 1) * LANES] = jnp.concatenate(halves, axis=0).T


def _nsa_prompt(q, kc, vc, ks, vs, kw, vw, g, cw, tq):
    B, T, _ = q.shape
    m_cmp = T // CMP_STRIDE
    kc16 = kc.reshape(B, m_cmp, CMP_STRIDE * KV_WIDTH)
    vc16 = vc.reshape(B, m_cmp, CMP_STRIDE * KV_WIDTH)
    full = lambda w: pl.BlockSpec((None, T, w), lambda b, h, i: (b, 0, 0))
    x16 = pl.BlockSpec((None, m_cmp, CMP_STRIDE * KV_WIDTH), lambda b, h, i: (b, 0, 0))
    qo = pl.BlockSpec((None, T, 2 * LANES), lambda b, h, i: (b, 0, h))
    wspecs = [_const_spec(w.shape) for w in cw]
    k_rows = pltpu.VMEM((T, LANES), BF16)
    v_t = pltpu.VMEM((T // tq, LANES, tq), BF16)
    q_rows = pltpu.VMEM((Q_PER_KV, tq, LANES), BF16)
    q_all = pltpu.VMEM((Q_PER_KV * tq, LANES), BF16)
    run_max = pltpu.VMEM((Q_PER_KV, SUBLANES, tq), F32)
    run_acc = pltpu.VMEM((Q_PER_KV, LANES, tq), F32)
    return pl.pallas_call(
        functools.partial(_nsa_prompt_body, T=T, TQ=tq),
        grid=(B, NSA_KV_HEADS, T // tq),
        in_specs=[qo, x16, x16, full(KV_WIDTH), full(KV_WIDTH), full(KV_WIDTH), full(KV_WIDTH), full(KV_WIDTH)] + wspecs,
        out_specs=qo,
        out_shape=jax.ShapeDtypeStruct((B, T, NSA_WIDTH), F32),
        scratch_shapes=[k_rows, k_rows, v_t, v_t,
                        pltpu.VMEM((m_cmp, LANES), BF16), pltpu.VMEM((LANES, m_cmp), BF16),
                        pltpu.VMEM((2, SUBLANES, LANES), F32),
                        q_rows, q_all, q_all, pltpu.VMEM((T // SLC_BLOCK, SUBLANES, tq), F32),
                        run_max, run_acc, run_max, run_acc],
        compiler_params=_params(("parallel", "arbitrary", "arbitrary")),
        name="nsa_prompt",
    )(q, kc16, vc16, ks, kw, vs, vw, g, *cw)


PAGES_PER_STEP = 64
BLOCKS_PER_STEP = PAGES_PER_STEP * PAGE_SIZE // SLC_BLOCK


def _page_specs():
    return [pl.BlockSpec((None, KV_WIDTH, PAGE_SIZE), lambda b, c, pt, k=k: (pt[b, c * PAGES_PER_STEP + k], 0, 0))
            for k in range(PAGES_PER_STEP)]


def _stack_heads(q, kv, lane_half):
    rows = []
    for hl in range(Q_PER_KV):
        tile = 2 * kv + hl // 2
        x = _head_lanes(q[:, tile * LANES:(tile + 1) * LANES], hl % 2)
        rows.append(x if lane_half(hl) == hl % 2 else pltpu.roll(x, HEAD_DIM, 1))
    return jnp.concatenate(rows, axis=0)


def _nsa_sample_cmp_body(pt_ref, q_ref, *refs, P, TS):
    n_pg = PAGES_PER_STEP
    kpages, vpages = refs[:n_pg], refs[n_pg:2 * n_pg]
    w1k_ref, w2k_ref, pek_ref, w1v_ref, w2v_ref, pev_ref = refs[2 * n_pg:2 * n_pg + 6]
    ocmp_ref, sel_ref, hk_ref, hv_ref, xk_ref, xv_ref = refs[2 * n_pg + 6:]
    c = pl.program_id(1)
    groups = n_pg * PAGE_SIZE // CMP_STRIDE
    m_cmp = P // CMP_STRIDE
    n_cmp = (P + TS - CMP_LEN) // CMP_STRIDE + 1
    n_slc = -(-(P + TS) // SLC_BLOCK)
    n_slc_pad = -(-n_slc // LANES) * LANES
    topk = min(SLC_TOPK, n_slc)

    gpp = PAGE_SIZE // CMP_STRIDE
    ri = _row_iota((PAGE_SIZE, PAGE_SIZE))
    pick = jnp.where(_lane_iota((PAGE_SIZE, PAGE_SIZE)) == CMP_STRIDE * (ri % gpp) + ri // gpp, 1.0, 0.0).astype(BF16)
    r0 = pl.multiple_of(c * groups, groups)
    for pages, x_ref, w1_ref, h_ref in ((kpages, xk_ref, w1k_ref, hk_ref), (vpages, xv_ref, w1v_ref, hv_ref)):
        for k, page in enumerate(pages):
            rows_by_r = _dot_nt(pick, page[...].astype(BF16))
            for r in range(CMP_STRIDE):
                x_ref[r, k * gpp:(k + 1) * gpp, :] = rows_by_r[r * gpp:(r + 1) * gpp, :]
        h = jnp.zeros((groups, 4 * LANES), F32)
        for i in range(CMP_STRIDE // 2):
            pair = jnp.concatenate([x_ref[2 * i], x_ref[2 * i + 1]], axis=1)
            h = h + _dot(pair.astype(BF16), w1_ref[2 * i * KV_WIDTH:(2 * i + 2) * KV_WIDTH, :])
        h_ref[pl.ds(r0, groups), :] = h

    @pl.when(c == pl.num_programs(1) - 1)
    def _():
        hidk = _cmp_hidden(hk_ref[...], w1k_ref, pek_ref).astype(BF16)
        hidv = _cmp_hidden(hv_ref[...], w1v_ref, pev_ref).astype(BF16)
        q = q_ref[...] * SCALE
        rows = Q_PER_KV * TS
        t = _row_iota((rows, 1)) % TS
        qpos = P + t
        n = _lane_iota((rows, m_cmp))
        mask = (n * CMP_STRIDE + CMP_LEN - 1 <= qpos) & (n < n_cmp)
        ov = _overlap(m_cmp, n_slc_pad, transpose=False)
        jl = _lane_iota((TS, n_slc_pad))
        qblk = (P + _row_iota((TS, 1))) // SLC_BLOCK
        valid = (jl <= qblk) & (jl < n_slc)
        forced = ((jl == 0) | (jl == qblk) | (jl == qblk - 1)) & (jl < n_slc)
        first_lanes = _lane_iota((1, LANES)) < BLOCKS_PER_STEP
        imps = []
        for kv in range(NSA_KV_HEADS):
            kcmp = _dot(hidk[:, kv * LANES:(kv + 1) * LANES], w2k_ref[...]).astype(BF16)
            vcmp = _dot(hidv[:, kv * LANES:(kv + 1) * LANES], w2v_ref[...]).astype(BF16)
            qs = _stack_heads(q, kv, lambda hl: hl % 2).astype(BF16)
            p, l = _masked_softmax_parts(_dot_nt(qs, kcmp), mask)
            ocmp_ref[kv] = _dot(p.astype(BF16), vcmp) / l
            pn = p / l
            pg = pn[0:TS] + pn[TS:2 * TS] + pn[2 * TS:3 * TS] + pn[3 * TS:4 * TS]
            imps.append(sum(_dot(part, ov) for part in _split3(pg)))
        tile2 = lambda a: jnp.concatenate([a] * NSA_KV_HEADS, axis=0)
        valid2, jl2 = tile2(valid), tile2(jl)
        score = jnp.where(tile2(forced), FORCED, jnp.where(valid2, jnp.concatenate(imps, axis=0), NEG_INF))
        rank = jnp.zeros(score.shape, jnp.int32)
        for i in range(n_slc):
            si = score[:, i:i + 1]
            beats = (si > score) | ((si == score) & (i < jl2))
            rank = rank + beats.astype(jnp.int32)
        sel = jnp.where((rank < topk) & valid2, 1.0, 0.0)
        for kv in range(NSA_KV_HEADS):
            for ch in range(sel_ref.shape[1]):
                b0 = ch * BLOCKS_PER_STEP
                tile = sel[kv * TS:(kv + 1) * TS, (b0 // LANES) * LANES:(b0 // LANES + 1) * LANES]
                if b0 % LANES:
                    tile = pltpu.roll(tile, LANES - b0 % LANES, 1)
                sel_ref[kv, ch] = jnp.where(first_lanes, tile, 0.0)


def _nsa_sample_cmp(page_table, q, ck_t, cv_t, cw):
    S, TS, _ = q.shape
    n_pages = page_table.shape[1]
    P = n_pages * PAGE_SIZE
    m_cmp = P // CMP_STRIDE
    steps = n_pages // PAGES_PER_STEP
    pages = _page_specs()
    wspecs = [pl.BlockSpec(w.shape, lambda b, c, pt, nd=w.ndim: (0,) * nd) for w in cw]
    rows_step = PAGES_PER_STEP * PAGE_SIZE
    grid_spec = pltpu.PrefetchScalarGridSpec(
        num_scalar_prefetch=1,
        grid=(S, steps),
        in_specs=[pl.BlockSpec((None, TS, NSA_WIDTH), lambda b, c, pt: (b, 0, 0))] + pages + pages + wspecs,
        out_specs=[pl.BlockSpec((None, NSA_KV_HEADS, Q_PER_KV * TS, LANES), lambda b, c, pt: (b, 0, 0, 0)),
                   pl.BlockSpec((None, NSA_KV_HEADS, steps + 1, TS, LANES), lambda b, c, pt: (b, 0, 0, 0, 0))],
        scratch_shapes=[pltpu.VMEM((m_cmp, 4 * LANES), F32), pltpu.VMEM((m_cmp, 4 * LANES), F32),
                        pltpu.VMEM((CMP_STRIDE, rows_step // CMP_STRIDE, KV_WIDTH), F32),
                        pltpu.VMEM((CMP_STRIDE, rows_step // CMP_STRIDE, KV_WIDTH), F32)],
    )
    return pl.pallas_call(
        functools.partial(_nsa_sample_cmp_body, P=P, TS=TS),
        grid_spec=grid_spec,
        out_shape=[jax.ShapeDtypeStruct((S, NSA_KV_HEADS, Q_PER_KV * TS, LANES), F32),
                   jax.ShapeDtypeStruct((S, NSA_KV_HEADS, steps + 1, TS, LANES), F32)],
        compiler_params=_params(("parallel", "arbitrary")),
        name="nsa_sample_cmp",
    )(page_table, q, *([ck_t] * PAGES_PER_STEP), *([cv_t] * PAGES_PER_STEP), *cw)


def _nsa_sample_slc_body(pt_ref, q_ref, sel_ref, ocmp_ref, g_ref, ksn_ref, vsn_ref, kwn_ref, vwn_ref,
                         kwin_ref, vwin_ref, *refs, P, TS):
    n_pg = PAGES_PER_STEP
    kpages, vpages = refs[:n_pg], refs[n_pg:2 * n_pg]
    o_ref, kwout_ref, vwout_ref, q2_ref, m_ref, l_ref, acc_ref = refs[2 * n_pg:]
    c = pl.program_id(1)
    last = pl.num_programs(1) - 1
    rows = NSA_KV_HEADS * Q_PER_KV * TS
    nk = n_pg * PAGE_SIZE
    wb = kwin_ref.shape[1]
    lane = _lane_iota((1, LANES))
    t = _row_iota((rows, 1)) % TS

    def sel_rows(ch):
        return jnp.concatenate([sel_ref[kv, ch] for kv in range(NSA_KV_HEADS) for _ in range(Q_PER_KV)], axis=0)

    def update(s, pv_of):
        m_old = m_ref[...]
        m_new = jnp.maximum(m_old, jnp.max(s, axis=-1, keepdims=True))
        p = jnp.exp(s - m_new[:, 0:1])
        alpha = jnp.exp(m_old - m_new)
        l_ref[...] = alpha * l_ref[...] + jnp.sum(p, axis=-1, keepdims=True)
        acc_ref[...] = alpha * acc_ref[...] + pv_of(p.astype(BF16))
        m_ref[...] = m_new

    def result():
        return jnp.where(m_ref[...] > 0.5 * NEG_INF, acc_ref[...] / jnp.maximum(l_ref[...], 1e-30), 0.0)

    def reset():
        m_ref[...] = jnp.full(m_ref.shape, NEG_INF, F32)
        l_ref[...] = jnp.zeros(l_ref.shape, F32)
        acc_ref[...] = jnp.zeros(acc_ref.shape, F32)

    @pl.when(c == 0)
    def _():
        q = q_ref[...] * SCALE
        q2_ref[...] = jnp.concatenate([_stack_heads(q, kv, lambda hl, kv=kv: kv) for kv in range(NSA_KV_HEADS)],
                                      axis=0).astype(BF16)
        reset()

    q2 = q2_ref[...]
    k_t = jnp.concatenate([r[...] for r in kpages], axis=1).astype(BF16)
    v_t = jnp.concatenate([r[...] for r in vpages], axis=1).astype(BF16)
    sel_c = sel_rows(c)
    chosen = jnp.concatenate(
        [jnp.where(lane < SLC_BLOCK, jnp.broadcast_to(sel_c[:, 2 * i:2 * i + 1], (rows, LANES)),
                   jnp.broadcast_to(sel_c[:, 2 * i + 1:2 * i + 2], (rows, LANES)))
         for i in range(nk // LANES)], axis=1)
    kpos = c * nk + _lane_iota((1, nk))
    bias = jnp.where((chosen > 0.5) & (kpos <= P + t), 0.0, NEG_INF)
    update(_dot(q2, k_t) + bias, lambda p: _dot_nt(p, v_t))

    @pl.when(c == last)
    def _():
        pad_rows = lambda x: jnp.concatenate([x, jnp.zeros((LANES - TS, LANES), F32)], axis=0).astype(BF16)
        own = (lane >> 6) == (_row_iota((rows, 1)) // (Q_PER_KV * TS))
        jn = P // SLC_BLOCK
        sel_n = sel_rows(jn // BLOCKS_PER_STEP)[:, jn % BLOCKS_PER_STEP:jn % BLOCKS_PER_STEP + 1]
        bias_n = jnp.where((sel_n > 0.5) & (lane <= t) & (lane < TS), 0.0, NEG_INF)
        vn = pad_rows(vsn_ref[...])
        update(_dot_nt(q2, pad_rows(ksn_ref[...])) + bias_n, lambda p: _dot(p, vn))
        o_slc = result()
        reset()
        kw_t, vw_t = kwin_ref[...], vwin_ref[...]
        iw = _lane_iota((1, wb))
        rel = t + wb - iw
        bias_b = jnp.where((rel >= 0) & (rel < WINDOW), 0.0, NEG_INF)
        vw_tb = vw_t.astype(BF16)
        update(_dot(q2, kw_t.astype(BF16)) + bias_b, lambda p: _dot_nt(p, vw_tb))
        bias_w = jnp.where((lane <= t) & (lane < TS), 0.0, NEG_INF)
        vwn = pad_rows(vwn_ref[...])
        update(_dot_nt(q2, pad_rows(kwn_ref[...])) + bias_w, lambda p: _dot(p, vwn))
        o_win = result()
        gs = jax.nn.sigmoid(g_ref[...])
        gcol = lambda comp: jnp.concatenate(
            [gs[:, 3 * hq + comp:3 * hq + comp + 1] for hq in range(NSA_HEADS)], axis=0)
        ocmp = jnp.concatenate([ocmp_ref[kv] for kv in range(NSA_KV_HEADS)], axis=0)
        o = gcol(0) * ocmp + gcol(1) * o_slc + gcol(2) * o_win
        o = jnp.where(own, o, 0.0)
        for tile in range(NSA_HEADS // 2):
            a = o[(2 * tile) * TS:(2 * tile + 1) * TS]
            b = o[(2 * tile + 1) * TS:(2 * tile + 2) * TS]
            kv = (2 * tile) // Q_PER_KV
            a = a if kv == 0 else pltpu.roll(a, HEAD_DIM, 1)
            b = b if kv == 1 else pltpu.roll(b, HEAD_DIM, 1)
            o_ref[:, tile * LANES:(tile + 1) * LANES] = a + b
        kwout_ref[0:wb - TS, :] = kw_t.T[TS:wb, :]
        kwout_ref[wb - TS:wb, :] = kwn_ref[...]
        vwout_ref[0:wb - TS, :] = vw_t.T[TS:wb, :]
        vwout_ref[wb - TS:wb, :] = vwn_ref[...]


def _nsa_sample_slc(page_table, q, sel, ocmp, g, ksn, vsn, kwn, vwn, kwin_t, vwin_t, ck_t, cv_t):
    S, TS, _ = q.shape
    n_pages = page_table.shape[1]
    P = n_pages * PAGE_SIZE
    wb = kwin_t.shape[2]
    rows = NSA_KV_HEADS * Q_PER_KV * TS
    steps = n_pages // PAGES_PER_STEP
    pages = _page_specs()
    per_seq = lambda shape: pl.BlockSpec((None,) + shape, lambda b, c, pt, nd=len(shape): (b,) + (0,) * nd)
    grid_spec = pltpu.PrefetchScalarGridSpec(
        num_scalar_prefetch=1,
        grid=(S, steps),
        in_specs=[per_seq((TS, NSA_WIDTH)), per_seq(sel.shape[1:]), per_seq(ocmp.shape[1:]), per_seq((TS, LANES)),
                  per_seq((TS, KV_WIDTH)), per_seq((TS, KV_WIDTH)), per_seq((TS, KV_WIDTH)), per_seq((TS, KV_WIDTH)),
                  per_seq((KV_WIDTH, wb)), per_seq((KV_WIDTH, wb))] + pages + pages,
        out_specs=[per_seq((TS, NSA_WIDTH)), per_seq((wb, KV_WIDTH)), per_seq((wb, KV_WIDTH))],
        scratch_shapes=[pltpu.VMEM((rows, LANES), BF16)] + [pltpu.VMEM((rows, LANES), F32)] * 3,
    )
    return pl.pallas_call(
        functools.partial(_nsa_sample_slc_body, P=P, TS=TS),
        grid_spec=grid_spec,
        out_shape=[jax.ShapeDtypeStruct((S, TS, NSA_WIDTH), F32),
                   jax.ShapeDtypeStruct((S, wb, KV_WIDTH), F32), jax.ShapeDtypeStruct((S, wb, KV_WIDTH), F32)],
        compiler_params=_params(("parallel", "arbitrary")),
        name="nsa_sample_slc",
    )(page_table, q, sel, ocmp, g, ksn, vsn, kwn, vwn, kwin_t, vwin_t,
      *([ck_t] * PAGES_PER_STEP), *([cv_t] * PAGES_PER_STEP))


def _merge_body(x_ref, glu_ref, o_ref, nw_ref, wg_ref, wup_ref, wnsa_ref, wout_ref, out_ref):
    x = x_ref[...]
    gate = jax.nn.sigmoid(_dot(_rms(x, nw_ref[...]).astype(BF16), wg_ref[...]))
    s5_out = _dot(glu_ref[...].astype(BF16), wup_ref[...])
    nsa_out = _dot(o_ref[...].astype(BF16), wnsa_ref[...])
    merged = gate[:, :D_MODEL] * s5_out + gate[:, D_MODEL:] * nsa_out
    out_ref[...] = x + _dot(merged.astype(BF16), wout_ref[...])


def _merge(x, glu, o, nw, wg, wup, wnsa, wout, tt):
    B, T, _ = x.shape
    row = lambda w: pl.BlockSpec((None, tt, w), lambda b, i: (b, i, 0))
    return pl.pallas_call(
        _merge_body,
        grid=(B, T // tt),
        in_specs=[row(D_MODEL), row(S5_WIDTH), row(NSA_WIDTH), _const_spec(nw.shape), _const_spec(wg.shape),
                  _const_spec(wup.shape), _const_spec(wnsa.shape), _const_spec(wout.shape)],
        out_specs=row(D_MODEL),
        out_shape=jax.ShapeDtypeStruct((B, T, D_MODEL), F32),
        compiler_params=_params(("parallel", "parallel")),
        name="merge",
    )(x, glu, o, nw, wg, wup, wnsa, wout)


def _ffn_body(x_ref, nw_ref, w1_ref, w2_ref, out_ref):
    x = x_ref[...]
    a = jnp.maximum(_dot(_rms(x, nw_ref[...]).astype(BF16), w1_ref[...]), 0.0)
    out_ref[...] = x + _dot((a * a).astype(BF16), w2_ref[...])


def _ffn(x, nw, w1, w2, tt):
    B, T, _ = x.shape
    row = pl.BlockSpec((None, tt, D_MODEL), lambda b, i: (b, i, 0))
    return pl.pallas_call(
        _ffn_body,
        grid=(B, T // tt),
        in_specs=[row, _const_spec(nw.shape), _const_spec(w1.shape), _const_spec(w2.shape)],
        out_specs=row,
        out_shape=jax.ShapeDtypeStruct((B, T, D_MODEL), F32),
        compiler_params=_params(("parallel", "parallel")),
        name="ffn",
    )(x, nw, w1, w2)


def _rope_tables(pos):
    half = HEAD_DIM // 2
    inv = ROPE_THETA ** (-jnp.arange(half, dtype=F32) / half)
    ang = pos.astype(F32)[:, None] * inv[None, :]
    cos, sin = jnp.cos(ang), jnp.sin(ang)
    return (jnp.concatenate([cos, cos, cos, cos], axis=-1),
            jnp.concatenate([-sin, sin, -sin, sin], axis=-1))


def _block_diag(w):
    G, a, b = w.shape
    return jnp.einsum('gab,gk->gakb', w, jnp.eye(G, dtype=w.dtype)).reshape(G * a, G * b)


def _cmp_weights(pe, w1, w2):
    eye = jnp.eye(NSA_KV_HEADS, dtype=F32)
    half = CMP_LEN // 2
    big = lambda w: jnp.einsum('rdf,hk->rhdkf', w, eye).reshape(half * KV_WIDTH, NSA_KV_HEADS * CMP_HIDDEN)
    w1b = jnp.concatenate([big(w1[:half]), big(w1[half:])], axis=1).astype(BF16)
    w2d = jnp.concatenate([w2, w2], axis=1).astype(BF16)
    flat = lambda p: jnp.broadcast_to(p[:, None, :], (half, NSA_KV_HEADS, HEAD_DIM)).reshape(1, half * KV_WIDTH)
    pe8 = jnp.concatenate([flat(pe[:half]), flat(pe[half:]), jnp.zeros((SUBLANES - 2, half * KV_WIDTH), F32)], axis=0)
    return w1b, w2d, pe8


def kernel(x_prompt, x_sample, cache_k_cmp, cache_v_cmp, cache_k_slc, cache_v_slc, state_k_win, state_v_win, state_s5_re, state_s5_im, page_table, norm_mix_w, w_in, s5_lam_re, s5_lam_im, s5_log_dt, s5_b_re, s5_b_im, s5_c_re, s5_c_im, s5_d, s5_w_glu, s5_w_up, q_norm_w, k_norm_w, cmp_pe_k, cmp_wk1, cmp_wk2, cmp_pe_v, cmp_wv1, cmp_wv2, nsa_w_up, w_out, norm_ffn_w, w_ffn1, w_ffn2):
    B, T, _ = x_prompt.shape
    S, TS, _ = x_sample.shape
    n_pages = page_table.shape[1]
    P = n_pages * PAGE_SIZE
    n_pool = cache_k_cmp.shape[1]
    assert norm_mix_w.shape[0] == 1 and B % SUBLANES == 0 and S % SUBLANES == 0
    assert TS < CMP_STRIDE and P % SLC_BLOCK == 0 and n_pages % PAGES_PER_STEP == 0

    w = w_in[0]
    cols = lambda a, n: w[:, a:a + n]
    w_a = jnp.concatenate(
        [cols(0, 512), cols(512, 512), cols(1024, 128), cols(1280, 128), cols(1536, 128),
         cols(1152, 128), cols(1408, 128), cols(1664, 128), cols(1792, N_GATE),
         jnp.zeros((D_MODEL, W_A_COLS - 1792 - N_GATE), F32)], axis=1).astype(BF16)
    w_g = w[:, OFF_GMIX:].astype(BF16)
    nw_mix = norm_mix_w[0][None, :]
    qkw = jnp.concatenate([jnp.tile(q_norm_w[0], NSA_HEADS)]
                          + [jnp.tile(k_norm_w[0, i], NSA_KV_HEADS) for i in range(3)])[None, :]
    li = np.arange(LANES)
    seg = jnp.asarray((li[:, None] // HEAD_DIM) == (li[None, :] // HEAD_DIM), BF16)
    lre, lim = s5_lam_re[0].reshape(1, S5_LANES), s5_lam_im[0].reshape(1, S5_LANES)
    ldt = jnp.repeat(s5_log_dt[0], S5_STATE)[None, :]
    bre = _block_diag(jnp.swapaxes(s5_b_re[0], 1, 2))
    bim = _block_diag(jnp.swapaxes(s5_b_im[0], 1, 2))
    cre = _block_diag(jnp.swapaxes(s5_c_re[0], 1, 2)).astype(BF16)
    cim = _block_diag(jnp.swapaxes(s5_c_im[0], 1, 2)).astype(BF16)
    s5d = s5_d[0][None, :]
    wglu = s5_w_glu[0].astype(BF16)
    wup = s5_w_up[0].astype(BF16)
    cw = _cmp_weights(cmp_pe_k[0], cmp_wk1[0], cmp_wk2[0]) + _cmp_weights(cmp_pe_v[0], cmp_wv1[0], cmp_wv2[0])
    wnsa = nsa_w_up[0].astype(BF16)
    wout = w_out[0].astype(BF16)
    nw_ffn = norm_ffn_w[0][None, :]
    w1 = w_ffn1[0].astype(BF16)
    w2 = w_ffn2[0].astype(BF16)

    def trunk(x, glu, o, tt):
        return _ffn(_merge(x, glu, o, nw_mix, w_g, wup, wnsa, wout, tt), nw_ffn, w1, w2, tt)

    cos_p, sin_p = _rope_tables(jnp.arange(T, dtype=jnp.int32))
    u, q, kc, ks, kw, vc, vs, vw, g = _inproj(x_prompt, cos_p, sin_p, nw_mix, w_a, qkw, seg, 512)
    zeros_st = jnp.zeros((B, S5_LANES), F32)
    glu_tm, sre_p, sim_p = _s5(jnp.swapaxes(u, 0, 1), zeros_st, zeros_st, lre, lim, ldt, bre, bim, cre, cim, s5d, wglu, 64)
    o_p = _nsa_prompt(q, kc, vc, ks, vs, kw, vw, g, cw, 256)
    y_prompt = trunk(x_prompt, jnp.swapaxes(glu_tm, 0, 1), o_p, 512)

    n_s = S * TS
    cos_s, sin_s = _rope_tables(P + jnp.arange(TS, dtype=jnp.int32))
    cos_s, sin_s = jnp.tile(cos_s, (S, 1)), jnp.tile(sin_s, (S, 1))
    xs = x_sample.reshape(1, n_s, D_MODEL)
    us, qs, kcs, kss, kws, vcs, vss, vws, gs = _inproj(xs, cos_s, sin_s, nw_mix, w_a, qkw, seg, n_s)
    seq = lambda a: a.reshape(S, TS, a.shape[-1])
    u_tm = jnp.swapaxes(seq(us), 0, 1)
    glu_s_tm, sre_s, sim_s = _s5(u_tm, state_s5_re[0].reshape(S, S5_LANES), state_s5_im[0].reshape(S, S5_LANES),
                                 lre, lim, ldt, bre, bim, cre, cim, s5d, wglu, TS)
    pool = lambda c: jnp.transpose(c[0], (0, 2, 3, 1)).reshape(n_pool, KV_WIDTH, PAGE_SIZE)
    ocmp_s, sel_s = _nsa_sample_cmp(page_table, seq(qs), pool(cache_k_cmp), pool(cache_v_cmp), cw)
    win = lambda s: jnp.transpose(s[0], (0, 2, 3, 1)).reshape(S, KV_WIDTH, s.shape[2])
    o_s, kwin_s, vwin_s = _nsa_sample_slc(page_table, seq(qs), sel_s, ocmp_s, seq(gs), seq(kss), seq(vss), seq(kws),
                                          seq(vws), win(state_k_win), win(state_v_win),
                                          pool(cache_k_slc), pool(cache_v_slc))
    glu_s = jnp.swapaxes(glu_s_tm, 0, 1).reshape(1, n_s, S5_WIDTH)
    y_sample = trunk(xs, glu_s, o_s.reshape(1, n_s, NSA_WIDTH), n_s).reshape(S, TS, D_MODEL)

    heads_p = lambda a: a.reshape(1, B, a.shape[1], NSA_KV_HEADS, HEAD_DIM)
    heads_s = lambda a: a.reshape(1, S, -1, NSA_KV_HEADS, HEAD_DIM)
    keep = min(WINDOW, T)
    st_p = lambda a: a.reshape(1, B, S5_GROUPS, S5_STATE)
    st_s = lambda a: a.reshape(1, S, S5_GROUPS, S5_STATE)
    return (y_prompt, y_sample,
            heads_p(kc), heads_s(kcs), heads_p(vc), heads_s(vcs),
            heads_p(ks), heads_s(kss), heads_p(vs), heads_s(vss),
            heads_p(kw[:, T - keep:]), heads_s(kwin_s), heads_p(vw[:, T - keep:]), heads_s(vwin_s),
            st_p(sre_p), st_s(sre_s), st_p(sim_p), st_s(sim_s))
```

```python
import functools

import numpy as np
import jax
import jax.numpy as jnp
from jax import lax
from jax.experimental import pallas as pl
from jax.experimental.pallas import tpu as pltpu

F32 = jnp.float32
BF16 = jnp.bfloat16

D_MODEL = 1024
PAGE_SIZE = 128
S5_WIDTH = 512
S5_GROUPS = 32
S5_STATE = 64
S5_LANES = S5_GROUPS * S5_STATE
HEAD_DIM = 64
NSA_HEADS = 8
NSA_KV_HEADS = 2
Q_PER_KV = 4
NSA_WIDTH = 512
KV_WIDTH = 128
CMP_LEN = 32
CMP_STRIDE = 16
CMP_HIDDEN = 128
SLC_BLOCK = 64
SLC_TOPK = 16
WINDOW = 512
ROPE_THETA = 10000.0
EPS = 1e-6
NEG_INF = -1e30
FORCED = 1e9
SCALE = HEAD_DIM ** -0.5
N_GATE = 3 * NSA_HEADS
OFF_GMIX = 2 * 512 + 6 * KV_WIDTH + N_GATE
W_A_COLS = 1920

LANES = 128
SUBLANES = 8
VMEM_LIMIT = 56 * 1024 * 1024
BOUND_SLACK = 1.02
MAX_SHIFT_GAP = 60.0


def _params(sem):
    return pltpu.CompilerParams(dimension_semantics=sem, vmem_limit_bytes=VMEM_LIMIT)


def _const_spec(shape):
    nd = len(shape)
    return pl.BlockSpec(shape, lambda *_: (0,) * nd)


def _dot(a, b):
    return jnp.dot(a, b, preferred_element_type=F32)


def _dot_nt(a, b):
    return lax.dot_general(a, b, (((1,), (1,)), ((), ())), preferred_element_type=F32)


def _rms(x, w):
    r = lax.rsqrt(jnp.mean(x * x, axis=-1, keepdims=True) + EPS)
    return x * r * w


def _lane_iota(shape):
    return lax.broadcasted_iota(jnp.int32, shape, len(shape) - 1)


def _row_iota(shape):
    return lax.broadcasted_iota(jnp.int32, shape, len(shape) - 2)


def _split3(x):
    hi = x.astype(BF16)
    r1 = x - hi.astype(F32)
    mid = r1.astype(BF16)
    lo = (r1 - mid.astype(F32)).astype(BF16)
    return hi, mid, lo


def _inproj_body(x_ref, nw_ref, w_ref, qkw_ref, seg_ref, cos_ref, sin_ref,
                 u_ref, q_ref, kc_ref, ks_ref, kw_ref, vc_ref, vs_ref, vw_ref, g_ref):
    xn = _rms(x_ref[...], nw_ref[...]).astype(BF16)
    z = _dot(xn, w_ref[...])
    u_ref[...] = z[:, :512]
    cos = cos_ref[...]
    sin = sin_ref[...]
    seg = seg_ref[...]
    first_half = (_lane_iota((1, LANES)) % HEAD_DIM) < (HEAD_DIM // 2)

    def norm_rope(t, w):
        ss = _dot((t * t).astype(BF16), seg)
        t = t * lax.rsqrt(ss * (1.0 / HEAD_DIM) + EPS) * w
        rot = jnp.where(first_half, pltpu.roll(t, LANES - HEAD_DIM // 2, 1), pltpu.roll(t, HEAD_DIM // 2, 1))
        return t * cos + rot * sin

    for j in range(4):
        q_ref[:, j * LANES:(j + 1) * LANES] = norm_rope(
            z[:, 512 + j * LANES:512 + (j + 1) * LANES], qkw_ref[:, j * LANES:(j + 1) * LANES])
    for j, r in enumerate((kc_ref, ks_ref, kw_ref)):
        r[...] = norm_rope(z[:, 1024 + j * LANES:1024 + (j + 1) * LANES],
                           qkw_ref[:, 512 + j * LANES:512 + (j + 1) * LANES])
    vc_ref[...] = z[:, 1408:1536]
    vs_ref[...] = z[:, 1536:1664]
    vw_ref[...] = z[:, 1664:1792]
    g_ref[...] = z[:, 1792:1920]


def _inproj(x, cos_t, sin_t, nw, w_a, qkw, seg, tt):
    B, T, _ = x.shape
    row = lambda w: pl.BlockSpec((None, tt, w), lambda b, i: (b, i, 0))
    tab = pl.BlockSpec((tt, LANES), lambda b, i: (i, 0))
    kv = jax.ShapeDtypeStruct((B, T, KV_WIDTH), F32)
    out_shape = [jax.ShapeDtypeStruct((B, T, 512), F32), jax.ShapeDtypeStruct((B, T, 512), F32)] + [kv] * 7
    return pl.pallas_call(
        _inproj_body,
        grid=(B, T // tt),
        in_specs=[row(D_MODEL), _const_spec((1, D_MODEL)), _const_spec((D_MODEL, W_A_COLS)),
                  _const_spec((1, 896)), _const_spec((LANES, LANES)), tab, tab],
        out_specs=[row(512), row(512)] + [row(KV_WIDTH)] * 7,
        out_shape=out_shape,
        compiler_params=_params(("parallel", "parallel")),
        name="inproj",
    )(x, nw, w_a, qkw, seg, cos_t, sin_t)


def _s5_body(u_ref, sre0_ref, sim0_ref, lre_ref, lim_ref, ldt_ref, bre_ref, bim_ref, cre_ref, cim_ref,
             d_ref, wglu_ref, out_ref, sre_ref, sim_ref,
             bbar_ref, a_ref, st_ref, xr_ref, xi_ref, *, tc):
    c = pl.program_id(1)
    rows = tc * SUBLANES

    @pl.when(c == 0)
    def _():
        lr, li = lre_ref[...], lim_ref[...]
        dt = jnp.exp(ldt_ref[...])
        mag = jnp.exp(lr * dt)
        ab_re, ab_im = mag * jnp.cos(li * dt), mag * jnp.sin(li * dt)
        den = lr * lr + li * li
        f_re = ((ab_re - 1.0) * lr + ab_im * li) / den
        f_im = (ab_im * lr - (ab_re - 1.0) * li) / den
        bre, bim = bre_ref[...], bim_ref[...]
        bbar_ref[:, :S5_LANES] = (f_re * bre - f_im * bim).astype(BF16)
        bbar_ref[:, S5_LANES:] = (f_re * bim + f_im * bre).astype(BF16)
        a_ref[0] = jnp.broadcast_to(ab_re, (SUBLANES, S5_LANES))
        a_ref[1] = jnp.broadcast_to(ab_im, (SUBLANES, S5_LANES))
        st_ref[0] = sre0_ref[...]
        st_ref[1] = sim0_ref[...]

    u = u_ref[...].reshape(rows, S5_WIDTH)
    ub = u.astype(BF16)
    hw, hl = S5_WIDTH // 2, S5_LANES // 2
    for n in range(2):
        xr_ref[:, n * hl:(n + 1) * hl] = _dot(ub[:, n * hw:(n + 1) * hw], bbar_ref[n * hw:(n + 1) * hw, n * hl:(n + 1) * hl])
        xi_ref[:, n * hl:(n + 1) * hl] = _dot(ub[:, n * hw:(n + 1) * hw],
                                              bbar_ref[n * hw:(n + 1) * hw, S5_LANES + n * hl:S5_LANES + (n + 1) * hl])

    slab = 512
    for lc in range(S5_LANES // slab):
        sl = slice(lc * slab, (lc + 1) * slab)
        ar, ai = a_ref[0, :, sl], a_ref[1, :, sl]

        def step(t, carry, sl=sl, ar=ar, ai=ai):
            sr, si = carry
            r0 = pl.multiple_of(t * SUBLANES, SUBLANES)
            nr = ar * sr - ai * si + xr_ref[pl.ds(r0, SUBLANES), sl]
            ni = ar * si + ai * sr + xi_ref[pl.ds(r0, SUBLANES), sl]
            xr_ref[pl.ds(r0, SUBLANES), sl] = nr
            xi_ref[pl.ds(r0, SUBLANES), sl] = ni
            return nr, ni

        sr, si = lax.fori_loop(0, tc, step, (st_ref[0, :, sl], st_ref[1, :, sl]), unroll=min(tc, 8))
        st_ref[0, :, sl] = sr
        st_ref[1, :, sl] = si

    y = jnp.concatenate(
        [_dot(xr_ref[:, n * hl:(n + 1) * hl].astype(BF16), cre_ref[n * hl:(n + 1) * hl, n * hw:(n + 1) * hw])
         - _dot(xi_ref[:, n * hl:(n + 1) * hl].astype(BF16), cim_ref[n * hl:(n + 1) * hl, n * hw:(n + 1) * hw])
         for n in range(2)], axis=1) + d_ref[...] * u
    yg = jax.nn.gelu(y)
    glu = yg * jax.nn.sigmoid(_dot(yg.astype(BF16), wglu_ref[...]))
    out_ref[...] = glu.reshape(tc, SUBLANES, S5_WIDTH)

    @pl.when(c == pl.num_programs(1) - 1)
    def _():
        sre_ref[...] = st_ref[0]
        sim_ref[...] = st_ref[1]


def _s5(u_tm, sre0, sim0, lre, lim, ldt, bre, bim, cre, cim, d, wglu, tc):
    T, B, _ = u_tm.shape
    st_spec = pl.BlockSpec((SUBLANES, S5_LANES), lambda b, c: (b, 0))
    blk = pl.BlockSpec((tc, SUBLANES, S5_WIDTH), lambda b, c: (c, b, 0))
    st = jax.ShapeDtypeStruct((B, S5_LANES), F32)
    return pl.pallas_call(
        functools.partial(_s5_body, tc=tc),
        grid=(B // SUBLANES, T // tc),
        in_specs=[blk, st_spec, st_spec,
                  _const_spec((1, S5_LANES)), _const_spec((1, S5_LANES)), _const_spec((1, S5_LANES)),
                  _const_spec((S5_WIDTH, S5_LANES)), _const_spec((S5_WIDTH, S5_LANES)),
                  _const_spec((S5_LANES, S5_WIDTH)), _const_spec((S5_LANES, S5_WIDTH)),
                  _const_spec((1, S5_WIDTH)), _const_spec((S5_WIDTH, S5_WIDTH))],
        out_specs=[blk, st_spec, st_spec],
        out_shape=[jax.ShapeDtypeStruct((T, B, S5_WIDTH), F32), st, st],
        scratch_shapes=[pltpu.VMEM((S5_WIDTH, 2 * S5_LANES), BF16),
                        pltpu.VMEM((2, SUBLANES, S5_LANES), F32),
                        pltpu.VMEM((2, SUBLANES, S5_LANES), F32),
                        pltpu.VMEM((tc * SUBLANES, S5_LANES), F32),
                        pltpu.VMEM((tc * SUBLANES, S5_LANES), F32)],
        compiler_params=_params(("parallel", "arbitrary")),
        name="s5",
    )(u_tm, sre0, sim0, lre, lim, ldt, bre, bim, cre, cim, d, wglu)


def _head_lanes(x, half):
    return jnp.where((_lane_iota((1, LANES)) >> 6) == half, x, 0.0)


def _masked_softmax_parts(s, mask):
    s = jnp.where(mask, s, NEG_INF)
    m = jnp.max(s, axis=-1, keepdims=True)
    p = jnp.where(mask, jnp.exp(s - m), 0.0)
    l = jnp.maximum(jnp.sum(p, axis=-1, keepdims=True), 1e-30)
    return p, l


def _cmp_hidden(h, w1_ref, pe_ref):
    m = h.shape[0]
    pe = pe_ref[...].astype(BF16)
    bias = _dot(pe, w1_ref[:, :256])[0:1] + _dot(pe, w1_ref[:, 256:])[1:2]
    return jax.nn.gelu(h[:, :256] + pltpu.roll(h[:, 256:], m - 1, 0) + bias)


def _overlap(n_rows, n_cols, transpose):
    shape = (n_cols, n_rows) if transpose else (n_rows, n_cols)
    n = _row_iota(shape) if not transpose else _lane_iota(shape)
    j = _lane_iota(shape) if not transpose else _row_iota(shape)
    lo = jnp.maximum(n * CMP_STRIDE, j * SLC_BLOCK)
    hi = jnp.minimum(n * CMP_STRIDE + CMP_LEN, j * SLC_BLOCK + SLC_BLOCK)
    return (jnp.maximum(hi - lo, 0).astype(F32) * (1.0 / CMP_LEN)).astype(BF16)


VT_ROWS = HEAD_DIM + 16


def _ones_row_rows(n):
    return jnp.where(_row_iota((VT_ROWS - HEAD_DIM, n)) == 0, 1.0, 0.0)


def _shifted_update_t(m_ref, acc_ref, h, s_t, v_t):
    tk, tq = s_t.shape
    m_ref[h] = jnp.maximum(m_ref[h], jnp.max(s_t.reshape(tk // SUBLANES, SUBLANES, tq), axis=0))
    acc_ref[h] = acc_ref[h] + _dot(v_t, jnp.exp(s_t).astype(BF16))


def _online_update_t(m_ref, acc_ref, h, s_t, v_t):
    m_old = m_ref[h]
    m_new = jnp.maximum(m_old, jnp.max(s_t, axis=0, keepdims=True))
    p_t = jnp.exp(s_t - m_new[0:1])
    alpha = jnp.exp(m_old - m_new)
    acc_ref[h] = alpha[0:1] * acc_ref[h] + _dot(v_t, p_t.astype(BF16))
    m_ref[h] = m_new


def _softmax_result_t(m_ref, acc_ref, h):
    acc = acc_ref[h]
    out = acc[0:HEAD_DIM] / jnp.maximum(acc[HEAD_DIM:HEAD_DIM + 1], 1e-30)
    return jnp.where(jnp.max(m_ref[h], axis=0, keepdims=True) > 0.5 * NEG_INF, out, 0.0)


def _key_rows(k, kv):
    lane = _lane_iota((1, LANES))
    kk = jnp.where(lane < HEAD_DIM, jnp.where(kv == 0, k, pltpu.roll(k, HEAD_DIM, 1)), 0.0)
    norm2 = jnp.max(jnp.sum(kk * kk, axis=1, keepdims=True), axis=0, keepdims=True)
    return jnp.where(lane == HEAD_DIM, 1.0, kk).astype(BF16), jnp.sqrt(norm2)


def _nsa_prompt_body(q_ref, kc16_ref, vc16_ref, ks_ref, kw_ref, vs_ref, vw_ref, g_ref,
                     w1k_ref, w2k_ref, pek_ref, w1v_ref, w2v_ref, pev_ref,
                     o_ref,
                     ksa_ref, kwa_ref, vst_ref, vwt_ref, kcmp_ref, vcmpt_ref, knorm_ref,
                     qc_ref, qs_ref, qw_ref, selb_ref, ms_ref, accs_ref, mw_ref, accw_ref, *, T, TQ):
    kv = pl.program_id(1)
    qi = pl.program_id(2)
    m_cmp = T // CMP_STRIDE
    n_cmp = m_cmp - 1
    n_slc = T // SLC_BLOCK
    topk = min(SLC_TOPK, n_slc)
    n_kt = T // TQ
    lane = _lane_iota((1, LANES))

    @pl.when(qi == 0)
    def _():
        ksa_ref[...], ns = _key_rows(ks_ref[...], kv)
        kwa_ref[...], nw = _key_rows(kw_ref[...], kv)
        knorm_ref[0] = jnp.broadcast_to(ns, (SUBLANES, LANES))
        knorm_ref[1] = jnp.broadcast_to(nw, (SUBLANES, LANES))
        ones = _ones_row_rows(TQ).astype(BF16)
        for kt in range(n_kt):
            for src, dst in ((vs_ref, vst_ref), (vw_ref, vwt_ref)):
                vt = src[kt * TQ:(kt + 1) * TQ, :].T
                dst[kt, 0:HEAD_DIM, :] = jnp.where(kv == 0, vt[0:HEAD_DIM], vt[HEAD_DIM:]).astype(BF16)
                dst[kt, HEAD_DIM:, :] = ones

        def compressed(x16_ref, w1_ref, pe_ref, w2_ref):
            hid = _cmp_hidden(_dot(x16_ref[...].astype(BF16), w1_ref[...]), w1_ref, pe_ref)
            hid = jnp.where(kv == 0, hid[:, :LANES], hid[:, LANES:])
            return _dot(hid.astype(BF16), w2_ref[...])

        kcmp_ref[...] = compressed(kc16_ref, w1k_ref, pek_ref, w2k_ref).astype(BF16)
        vcmp_t = compressed(vc16_ref, w1v_ref, pev_ref, w2v_ref).T
        vcmpt_ref[0:HEAD_DIM, :] = vcmp_t[0:HEAD_DIM].astype(BF16)
        vcmpt_ref[HEAD_DIM:, :] = jnp.zeros((HEAD_DIM, m_cmp), BF16)

    r0 = pl.multiple_of(qi * TQ, TQ)
    for pp in range(2):
        qt = q_ref[pl.ds(r0, TQ), pp * LANES:(pp + 1) * LANES] * SCALE
        for e in range(2):
            qh = jnp.where(lane < HEAD_DIM, qt if e == 0 else pltpu.roll(qt, HEAD_DIM, 1), 0.0)
            qc_ref[2 * pp + e] = qh.astype(BF16)
            qnorm = jnp.sqrt(jnp.sum(qh * qh, axis=1, keepdims=True))
            for dst, which in ((qs_ref, 0), (qw_ref, 1)):
                bound = BOUND_SLACK * qnorm * knorm_ref[which][0:1, 0:1]
                dst[(2 * pp + e) * TQ:(2 * pp + e + 1) * TQ, :] = jnp.where(lane == HEAD_DIM, -bound, qh).astype(BF16)
    qpos = r0 + _lane_iota((1, TQ))

    n = _row_iota((m_cmp, TQ))
    mask_c = (n * CMP_STRIDE + CMP_LEN - 1 <= qpos) & (n < n_cmp)
    pg = jnp.zeros((m_cmp, TQ), F32)
    o_cmp = []
    for h in range(Q_PER_KV):
        s_t = jnp.where(mask_c, _dot_nt(kcmp_ref[...], qc_ref[h]), NEG_INF)
        p_t = jnp.where(mask_c, jnp.exp(s_t - jnp.max(s_t, axis=0, keepdims=True)), 0.0)
        inv_l = 1.0 / jnp.maximum(jnp.sum(p_t, axis=0, keepdims=True), 1e-30)
        pg = pg + p_t * inv_l
        o_cmp.append(_dot(vcmpt_ref[...], p_t.astype(BF16))[0:HEAD_DIM] * inv_l)

    ovt = _overlap(m_cmp, n_slc, transpose=True)
    imp = sum(_dot(ovt, part) for part in _split3(pg))
    j = _row_iota((n_slc, TQ))
    qblk = qpos // SLC_BLOCK
    valid = j <= qblk
    forced = (j == 0) | (j == qblk) | (j == qblk - 1)
    score = jnp.where(forced, FORCED, jnp.where(valid, imp, NEG_INF))
    rank = jnp.zeros((n_slc, TQ), jnp.int32)
    for i in range(n_slc):
        si = score[i:i + 1, :]
        beats = (si > score) | ((si == score) & (i < j))
        rank = rank + beats.astype(jnp.int32)
    def reset():
        for m_r, acc_r in ((ms_ref, accs_ref), (mw_ref, accw_ref)):
            m_r[...] = jnp.full(m_r.shape, NEG_INF, F32)
            acc_r[...] = jnp.zeros(acc_r.shape, F32)

    sel_bias = jnp.where((rank < topk) & valid, 0.0, NEG_INF)
    for jb in range(n_slc):
        selb_ref[jb] = jnp.broadcast_to(sel_bias[jb:jb + 1], (SUBLANES, TQ))
    not_after = _row_iota((TQ, 1)) <= _lane_iota((1, TQ))
    bias_diag = jnp.where(not_after, 0.0, NEG_INF)
    bias_far = jnp.where(not_after, NEG_INF, 0.0)
    blocks_per_tile = TQ // SLC_BLOCK
    win_tiles = WINDOW // TQ

    def both_branches(update):
        def apply(m_r, acc_r, s_all, bias, v_t):
            for h in range(Q_PER_KV):
                update(m_r, acc_r, h, s_all[:, h * TQ:(h + 1) * TQ] + bias, v_t)

        def slc_scores(kb):
            return _dot_nt(ksa_ref[pl.ds(pl.multiple_of(kb * TQ, TQ), TQ), :], qs_ref[...])

        def win_scores(kb):
            return _dot_nt(kwa_ref[pl.ds(pl.multiple_of(kb * TQ, TQ), TQ), :], qw_ref[...])

        def slc_bias(kb):
            rows = []
            for jb in range(blocks_per_tile):
                rows += [selb_ref[kb * blocks_per_tile + jb]] * (SLC_BLOCK // SUBLANES)
            return jnp.concatenate(rows, axis=0)

        def slc_pair(p, carry):
            ka = 2 * p
            kb = jnp.minimum(ka + 1, n_kt - 1)
            live = jnp.where(ka + 1 < qi, 0.0, NEG_INF)
            s_a, s_b = slc_scores(ka), slc_scores(kb)
            apply(ms_ref, accs_ref, s_a, slc_bias(ka), vst_ref[ka])
            apply(ms_ref, accs_ref, s_b, slc_bias(kb) + live, vst_ref[kb])
            return carry

        lax.fori_loop(0, (qi + 1) // 2, slc_pair, 0)

        win_kb = [jnp.maximum(qi - d, 0) for d in range(win_tiles + 1)]
        s_own = slc_scores(qi)
        s_win = [win_scores(kb) for kb in win_kb]
        apply(ms_ref, accs_ref, s_own, slc_bias(qi) + bias_diag, vst_ref[qi])
        for d, kb in enumerate(win_kb):
            exists = jnp.where(qi >= d, 0.0, NEG_INF)
            bias = exists + (bias_diag if d == 0 else bias_far if d == win_tiles else 0.0)
            apply(mw_ref, accw_ref, s_win[d], bias, vwt_ref[kb])

    reset()
    both_branches(_shifted_update_t)

    worst = jnp.full((1, TQ), 0.0, F32)
    for m_r in (ms_ref, mw_ref):
        for h in range(Q_PER_KV):
            top = jnp.max(m_r[h], axis=0, keepdims=True)
            worst = jnp.minimum(worst, jnp.where(top > 0.5 * NEG_INF, top, 0.0))
    redo = jnp.min(worst) < -MAX_SHIFT_GAP

    @pl.when(redo)
    def _():
        reset()
        both_branches(_online_update_t)

    g = g_ref[pl.ds(r0, TQ), :]
    gs_t = jax.nn.sigmoid(jnp.where(kv == 0, g, pltpu.roll(g, LANES - 3 * Q_PER_KV, 1))).T
    for pp in range(2):
        halves = []
        for e in range(2):
            h = 2 * pp + e
            halves.append(gs_t[3 * h:3 * h + 1] * o_cmp[h]
                          + gs_t[3 * h + 1:3 * h + 2] * _softmax_result_t(ms_ref, accs_ref, h)
                          + gs_t[3 * h + 2:3 * h + 3] * _softmax_result_t(mw_ref, accw_ref, h))
        o_ref[pl.ds(r0, TQ), pp * LANES:(pp + 1) * LANES] = jnp.concatenate(halves, axis=0).T


def _nsa_prompt(q, kc, vc, ks, vs, kw, vw, g, cw, tq):
    B, T, _ = q.shape
    m_cmp = T // CMP_STRIDE
    kc16 = kc.reshape(B, m_cmp, CMP_STRIDE * KV_WIDTH)
    vc16 = vc.reshape(B, m_cmp, CMP_STRIDE * KV_WIDTH)
    full = lambda w: pl.BlockSpec((None, T, w), lambda b, h, i: (b, 0, 0))
    x16 = pl.BlockSpec((None, m_cmp, CMP_STRIDE * KV_WIDTH), lambda b, h, i: (b, 0, 0))
    qo = pl.BlockSpec((None, T, 2 * LANES), lambda b, h, i: (b, 0, h))
    wspecs = [_const_spec(w.shape) for w in cw]
    k_rows = pltpu.VMEM((T, LANES), BF16)
    v_t = pltpu.VMEM((T // tq, VT_ROWS, tq), BF16)
    q_rows = pltpu.VMEM((Q_PER_KV, tq, LANES), BF16)
    q_all = pltpu.VMEM((Q_PER_KV * tq, LANES), BF16)
    run_max = pltpu.VMEM((Q_PER_KV, SUBLANES, tq), F32)
    run_acc = pltpu.VMEM((Q_PER_KV, VT_ROWS, tq), F32)
    return pl.pallas_call(
        functools.partial(_nsa_prompt_body, T=T, TQ=tq),
        grid=(B, NSA_KV_HEADS, T // tq),
        in_specs=[qo, x16, x16, full(KV_WIDTH), full(KV_WIDTH), full(KV_WIDTH), full(KV_WIDTH), full(KV_WIDTH)] + wspecs,
        out_specs=qo,
        out_shape=jax.ShapeDtypeStruct((B, T, NSA_WIDTH), F32),
        scratch_shapes=[k_rows, k_rows, v_t, v_t,
                        pltpu.VMEM((m_cmp, LANES), BF16), pltpu.VMEM((LANES, m_cmp), BF16),
                        pltpu.VMEM((2, SUBLANES, LANES), F32),
                        q_rows, q_all, q_all, pltpu.VMEM((T // SLC_BLOCK, SUBLANES, tq), F32),
                        run_max, run_acc, run_max, run_acc],
        compiler_params=_params(("parallel", "arbitrary", "arbitrary")),
        name="nsa_prompt",
    )(q, kc16, vc16, ks, kw, vs, vw, g, *cw)


PAGES_PER_STEP = 64
BLOCKS_PER_STEP = PAGES_PER_STEP * PAGE_SIZE // SLC_BLOCK


def _page_specs():
    return [pl.BlockSpec((None, KV_WIDTH, PAGE_SIZE), lambda b, c, pt, k=k: (pt[b, c * PAGES_PER_STEP + k], 0, 0))
            for k in range(PAGES_PER_STEP)]


def _stack_heads(q, kv, lane_half):
    rows = []
    for hl in range(Q_PER_KV):
        tile = 2 * kv + hl // 2
        x = _head_lanes(q[:, tile * LANES:(tile + 1) * LANES], hl % 2)
        rows.append(x if lane_half(hl) == hl % 2 else pltpu.roll(x, HEAD_DIM, 1))
    return jnp.concatenate(rows, axis=0)


def _nsa_sample_cmp_body(pt_ref, q_ref, *refs, P, TS):
    n_pg = PAGES_PER_STEP
    kpages, vpages = refs[:n_pg], refs[n_pg:2 * n_pg]
    w1k_ref, w2k_ref, pek_ref, w1v_ref, w2v_ref, pev_ref = refs[2 * n_pg:2 * n_pg + 6]
    ocmp_ref, sel_ref, hk_ref, hv_ref, xk_ref, xv_ref = refs[2 * n_pg + 6:]
    c = pl.program_id(1)
    groups = n_pg * PAGE_SIZE // CMP_STRIDE
    m_cmp = P // CMP_STRIDE
    n_cmp = (P + TS - CMP_LEN) // CMP_STRIDE + 1
    n_slc = -(-(P + TS) // SLC_BLOCK)
    n_slc_pad = -(-n_slc // LANES) * LANES
    topk = min(SLC_TOPK, n_slc)

    gpp = PAGE_SIZE // CMP_STRIDE
    ri = _row_iota((PAGE_SIZE, PAGE_SIZE))
    li = _lane_iota((PAGE_SIZE, PAGE_SIZE))
    pick = jnp.where(ri == CMP_STRIDE * (li % gpp) + li // gpp, 1.0, 0.0).astype(BF16)
    r0 = pl.multiple_of(c * groups, groups)
    for pages, x_ref, w1_ref, h_ref in ((kpages, xk_ref, w1k_ref, hk_ref), (vpages, xv_ref, w1v_ref, hv_ref)):
        for k, page in enumerate(pages):
            rows_by_r = _dot(page[...].astype(BF16), pick).T
            for r in range(CMP_STRIDE):
                x_ref[r, k * gpp:(k + 1) * gpp, :] = rows_by_r[r * gpp:(r + 1) * gpp, :]
        h = jnp.zeros((groups, 4 * LANES), F32)
        for i in range(CMP_STRIDE // 2):
            pair = jnp.concatenate([x_ref[2 * i], x_ref[2 * i + 1]], axis=1)
            h = h + _dot(pair.astype(BF16), w1_ref[2 * i * KV_WIDTH:(2 * i + 2) * KV_WIDTH, :])
        h_ref[pl.ds(r0, groups), :] = h

    @pl.when(c == pl.num_programs(1) - 1)
    def _():
        hidk = _cmp_hidden(hk_ref[...], w1k_ref, pek_ref).astype(BF16)
        hidv = _cmp_hidden(hv_ref[...], w1v_ref, pev_ref).astype(BF16)
        q = q_ref[...] * SCALE
        rows = Q_PER_KV * TS
        t = _row_iota((rows, 1)) % TS
        qpos = P + t
        n = _lane_iota((rows, m_cmp))
        mask = (n * CMP_STRIDE + CMP_LEN - 1 <= qpos) & (n < n_cmp)
        ov = _overlap(m_cmp, n_slc_pad, transpose=False)
        jl = _lane_iota((TS, n_slc_pad))
        qblk = (P + _row_iota((TS, 1))) // SLC_BLOCK
        valid = (jl <= qblk) & (jl < n_slc)
        forced = ((jl == 0) | (jl == qblk) | (jl == qblk - 1)) & (jl < n_slc)
        first_lanes = _lane_iota((1, LANES)) < BLOCKS_PER_STEP
        imps = []
        for kv in range(NSA_KV_HEADS):
            kcmp = _dot(hidk[:, kv * LANES:(kv + 1) * LANES], w2k_ref[...]).astype(BF16)
            vcmp = _dot(hidv[:, kv * LANES:(kv + 1) * LANES], w2v_ref[...]).astype(BF16)
            qs = _stack_heads(q, kv, lambda hl: hl % 2).astype(BF16)
            p, l = _masked_softmax_parts(_dot_nt(qs, kcmp), mask)
            ocmp_ref[kv] = _dot(p.astype(BF16), vcmp) / l
            pn = p / l
            pg = pn[0:TS] + pn[TS:2 * TS] + pn[2 * TS:3 * TS] + pn[3 * TS:4 * TS]
            imps.append(sum(_dot(part, ov) for part in _split3(pg)))
        tile2 = lambda a: jnp.concatenate([a] * NSA_KV_HEADS, axis=0)
        valid2, jl2 = tile2(valid), tile2(jl)
        score = jnp.where(tile2(forced), FORCED, jnp.where(valid2, jnp.concatenate(imps, axis=0), NEG_INF))
        rank = jnp.zeros(score.shape, jnp.int32)
        for i in range(n_slc):
            si = score[:, i:i + 1]
            beats = (si > score) | ((si == score) & (i < jl2))
            rank = rank + beats.astype(jnp.int32)
        sel = jnp.where((rank < topk) & valid2, 1.0, 0.0)
        for kv in range(NSA_KV_HEADS):
            for ch in range(sel_ref.shape[1]):
                b0 = ch * BLOCKS_PER_STEP
                tile = sel[kv * TS:(kv + 1) * TS, (b0 // LANES) * LANES:(b0 // LANES + 1) * LANES]
                if b0 % LANES:
                    tile = pltpu.roll(tile, LANES - b0 % LANES, 1)
                sel_ref[kv, ch] = jnp.where(first_lanes, tile, 0.0)


def _nsa_sample_cmp(page_table, q, ck_t, cv_t, cw):
    S, TS, _ = q.shape
    n_pages = page_table.shape[1]
    P = n_pages * PAGE_SIZE
    m_cmp = P // CMP_STRIDE
    steps = n_pages // PAGES_PER_STEP
    pages = _page_specs()
    wspecs = [pl.BlockSpec(w.shape, lambda b, c, pt, nd=w.ndim: (0,) * nd) for w in cw]
    rows_step = PAGES_PER_STEP * PAGE_SIZE
    grid_spec = pltpu.PrefetchScalarGridSpec(
        num_scalar_prefetch=1,
        grid=(S, steps),
        in_specs=[pl.BlockSpec((None, TS, NSA_WIDTH), lambda b, c, pt: (b, 0, 0))] + pages + pages + wspecs,
        out_specs=[pl.BlockSpec((None, NSA_KV_HEADS, Q_PER_KV * TS, LANES), lambda b, c, pt: (b, 0, 0, 0)),
                   pl.BlockSpec((None, NSA_KV_HEADS, steps + 1, TS, LANES), lambda b, c, pt: (b, 0, 0, 0, 0))],
        scratch_shapes=[pltpu.VMEM((m_cmp, 4 * LANES), F32), pltpu.VMEM((m_cmp, 4 * LANES), F32),
                        pltpu.VMEM((CMP_STRIDE, rows_step // CMP_STRIDE, KV_WIDTH), F32),
                        pltpu.VMEM((CMP_STRIDE, rows_step // CMP_STRIDE, KV_WIDTH), F32)],
    )
    return pl.pallas_call(
        functools.partial(_nsa_sample_cmp_body, P=P, TS=TS),
        grid_spec=grid_spec,
        out_shape=[jax.ShapeDtypeStruct((S, NSA_KV_HEADS, Q_PER_KV * TS, LANES), F32),
                   jax.ShapeDtypeStruct((S, NSA_KV_HEADS, steps + 1, TS, LANES), F32)],
        compiler_params=_params(("parallel", "arbitrary")),
        name="nsa_sample_cmp",
    )(page_table, q, *([ck_t] * PAGES_PER_STEP), *([cv_t] * PAGES_PER_STEP), *cw)


def _nsa_sample_slc_body(pt_ref, q_ref, sel_ref, ocmp_ref, g_ref, ksn_ref, vsn_ref, kwn_ref, vwn_ref,
                         kwin_ref, vwin_ref, *refs, P, TS):
    n_pg = PAGES_PER_STEP
    kpages, vpages = refs[:n_pg], refs[n_pg:2 * n_pg]
    o_ref, kwout_ref, vwout_ref, q2_ref, m_ref, l_ref, acc_ref = refs[2 * n_pg:]
    c = pl.program_id(1)
    last = pl.num_programs(1) - 1
    rows = NSA_KV_HEADS * Q_PER_KV * TS
    nk = n_pg * PAGE_SIZE
    wb = kwin_ref.shape[1]
    lane = _lane_iota((1, LANES))
    t = _row_iota((rows, 1)) % TS

    def sel_rows(ch):
        return jnp.concatenate([sel_ref[kv, ch] for kv in range(NSA_KV_HEADS) for _ in range(Q_PER_KV)], axis=0)

    def update(s, pv_of):
        m_old = m_ref[...]
        m_new = jnp.maximum(m_old, jnp.max(s, axis=-1, keepdims=True))
        p = jnp.exp(s - m_new[:, 0:1])
        alpha = jnp.exp(m_old - m_new)
        l_ref[...] = alpha * l_ref[...] + jnp.sum(p, axis=-1, keepdims=True)
        acc_ref[...] = alpha * acc_ref[...] + pv_of(p.astype(BF16))
        m_ref[...] = m_new

    def result():
        return jnp.where(m_ref[...] > 0.5 * NEG_INF, acc_ref[...] / jnp.maximum(l_ref[...], 1e-30), 0.0)

    def reset():
        m_ref[...] = jnp.full(m_ref.shape, NEG_INF, F32)
        l_ref[...] = jnp.zeros(l_ref.shape, F32)
        acc_ref[...] = jnp.zeros(acc_ref.shape, F32)

    @pl.when(c == 0)
    def _():
        q = q_ref[...] * SCALE
        q2_ref[...] = jnp.concatenate([_stack_heads(q, kv, lambda hl, kv=kv: kv) for kv in range(NSA_KV_HEADS)],
                                      axis=0).astype(BF16)
        reset()

    q2 = q2_ref[...]
    k_t = jnp.concatenate([r[...] for r in kpages], axis=1).astype(BF16)
    v_t = jnp.concatenate([r[...] for r in vpages], axis=1).astype(BF16)
    sel_c = sel_rows(c)
    chosen = jnp.concatenate(
        [jnp.where(lane < SLC_BLOCK, jnp.broadcast_to(sel_c[:, 2 * i:2 * i + 1], (rows, LANES)),
                   jnp.broadcast_to(sel_c[:, 2 * i + 1:2 * i + 2], (rows, LANES)))
         for i in range(nk // LANES)], axis=1)
    kpos = c * nk + _lane_iota((1, nk))
    bias = jnp.where((chosen > 0.5) & (kpos <= P + t), 0.0, NEG_INF)
    update(_dot(q2, k_t) + bias, lambda p: _dot_nt(p, v_t))

    @pl.when(c == last)
    def _():
        pad_rows = lambda x: jnp.concatenate([x, jnp.zeros((LANES - TS, LANES), F32)], axis=0).astype(BF16)
        own = (lane >> 6) == (_row_iota((rows, 1)) // (Q_PER_KV * TS))
        jn = P // SLC_BLOCK
        sel_n = sel_rows(jn // BLOCKS_PER_STEP)[:, jn % BLOCKS_PER_STEP:jn % BLOCKS_PER_STEP + 1]
        bias_n = jnp.where((sel_n > 0.5) & (lane <= t) & (lane < TS), 0.0, NEG_INF)
        vn = pad_rows(vsn_ref[...])
        update(_dot_nt(q2, pad_rows(ksn_ref[...])) + bias_n, lambda p: _dot(p, vn))
        o_slc = result()
        reset()
        kw_t, vw_t = kwin_ref[...], vwin_ref[...]
        iw = _lane_iota((1, wb))
        rel = t + wb - iw
        bias_b = jnp.where((rel >= 0) & (rel < WINDOW), 0.0, NEG_INF)
        vw_tb = vw_t.astype(BF16)
        update(_dot(q2, kw_t.astype(BF16)) + bias_b, lambda p: _dot_nt(p, vw_tb))
        bias_w = jnp.where((lane <= t) & (lane < TS), 0.0, NEG_INF)
        vwn = pad_rows(vwn_ref[...])
        update(_dot_nt(q2, pad_rows(kwn_ref[...])) + bias_w, lambda p: _dot(p, vwn))
        o_win = result()
        gs = jax.nn.sigmoid(g_ref[...])
        gcol = lambda comp: jnp.concatenate(
            [gs[:, 3 * hq + comp:3 * hq + comp + 1] for hq in range(NSA_HEADS)], axis=0)
        ocmp = jnp.concatenate([ocmp_ref[kv] for kv in range(NSA_KV_HEADS)], axis=0)
        o = gcol(0) * ocmp + gcol(1) * o_slc + gcol(2) * o_win
        o = jnp.where(own, o, 0.0)
        for tile in range(NSA_HEADS // 2):
            a = o[(2 * tile) * TS:(2 * tile + 1) * TS]
            b = o[(2 * tile + 1) * TS:(2 * tile + 2) * TS]
            kv = (2 * tile) // Q_PER_KV
            a = a if kv == 0 else pltpu.roll(a, HEAD_DIM, 1)
            b = b if kv == 1 else pltpu.roll(b, HEAD_DIM, 1)
            o_ref[:, tile * LANES:(tile + 1) * LANES] = a + b
        kwout_ref[0:wb - TS, :] = kw_t.T[TS:wb, :]
        kwout_ref[wb - TS:wb, :] = kwn_ref[...]
        vwout_ref[0:wb - TS, :] = vw_t.T[TS:wb, :]
        vwout_ref[wb - TS:wb, :] = vwn_ref[...]


def _nsa_sample_slc(page_table, q, sel, ocmp, g, ksn, vsn, kwn, vwn, kwin_t, vwin_t, ck_t, cv_t):
    S, TS, _ = q.shape
    n_pages = page_table.shape[1]
    P = n_pages * PAGE_SIZE
    wb = kwin_t.shape[2]
    rows = NSA_KV_HEADS * Q_PER_KV * TS
    steps = n_pages // PAGES_PER_STEP
    pages = _page_specs()
    per_seq = lambda shape: pl.BlockSpec((None,) + shape, lambda b, c, pt, nd=len(shape): (b,) + (0,) * nd)
    grid_spec = pltpu.PrefetchScalarGridSpec(
        num_scalar_prefetch=1,
        grid=(S, steps),
        in_specs=[per_seq((TS, NSA_WIDTH)), per_seq(sel.shape[1:]), per_seq(ocmp.shape[1:]), per_seq((TS, LANES)),
                  per_seq((TS, KV_WIDTH)), per_seq((TS, KV_WIDTH)), per_seq((TS, KV_WIDTH)), per_seq((TS, KV_WIDTH)),
                  per_seq((KV_WIDTH, wb)), per_seq((KV_WIDTH, wb))] + pages + pages,
        out_specs=[per_seq((TS, NSA_WIDTH)), per_seq((wb, KV_WIDTH)), per_seq((wb, KV_WIDTH))],
        scratch_shapes=[pltpu.VMEM((rows, LANES), BF16)] + [pltpu.VMEM((rows, LANES), F32)] * 3,
    )
    return pl.pallas_call(
        functools.partial(_nsa_sample_slc_body, P=P, TS=TS),
        grid_spec=grid_spec,
        out_shape=[jax.ShapeDtypeStruct((S, TS, NSA_WIDTH), F32),
                   jax.ShapeDtypeStruct((S, wb, KV_WIDTH), F32), jax.ShapeDtypeStruct((S, wb, KV_WIDTH), F32)],
        compiler_params=_params(("parallel", "arbitrary")),
        name="nsa_sample_slc",
    )(page_table, q, sel, ocmp, g, ksn, vsn, kwn, vwn, kwin_t, vwin_t,
      *([ck_t] * PAGES_PER_STEP), *([cv_t] * PAGES_PER_STEP))


def _merge_body(x_ref, glu_ref, o_ref, nw_ref, wg_ref, wup_ref, wnsa_ref, wout_ref, out_ref):
    x = x_ref[...]
    gate = jax.nn.sigmoid(_dot(_rms(x, nw_ref[...]).astype(BF16), wg_ref[...]))
    s5_out = _dot(glu_ref[...].astype(BF16), wup_ref[...])
    nsa_out = _dot(o_ref[...].astype(BF16), wnsa_ref[...])
    merged = gate[:, :D_MODEL] * s5_out + gate[:, D_MODEL:] * nsa_out
    out_ref[...] = x + _dot(merged.astype(BF16), wout_ref[...])


def _merge(x, glu, o, nw, wg, wup, wnsa, wout, tt):
    B, T, _ = x.shape
    row = lambda w: pl.BlockSpec((None, tt, w), lambda b, i: (b, i, 0))
    return pl.pallas_call(
        _merge_body,
        grid=(B, T // tt),
        in_specs=[row(D_MODEL), row(S5_WIDTH), row(NSA_WIDTH), _const_spec(nw.shape), _const_spec(wg.shape),
                  _const_spec(wup.shape), _const_spec(wnsa.shape), _const_spec(wout.shape)],
        out_specs=row(D_MODEL),
        out_shape=jax.ShapeDtypeStruct((B, T, D_MODEL), F32),
        compiler_params=_params(("parallel", "parallel")),
        name="merge",
    )(x, glu, o, nw, wg, wup, wnsa, wout)


def _ffn_body(x_ref, nw_ref, w1_ref, w2_ref, out_ref):
    x = x_ref[...]
    a = jnp.maximum(_dot(_rms(x, nw_ref[...]).astype(BF16), w1_ref[...]), 0.0)
    out_ref[...] = x + _dot((a * a).astype(BF16), w2_ref[...])


def _ffn(x, nw, w1, w2, tt):
    B, T, _ = x.shape
    row = pl.BlockSpec((None, tt, D_MODEL), lambda b, i: (b, i, 0))
    return pl.pallas_call(
        _ffn_body,
        grid=(B, T // tt),
        in_specs=[row, _const_spec(nw.shape), _const_spec(w1.shape), _const_spec(w2.shape)],
        out_specs=row,
        out_shape=jax.ShapeDtypeStruct((B, T, D_MODEL), F32),
        compiler_params=_params(("parallel", "parallel")),
        name="ffn",
    )(x, nw, w1, w2)


def _rope_tables(pos):
    half = HEAD_DIM // 2
    inv = ROPE_THETA ** (-jnp.arange(half, dtype=F32) / half)
    ang = pos.astype(F32)[:, None] * inv[None, :]
    cos, sin = jnp.cos(ang), jnp.sin(ang)
    return (jnp.concatenate([cos, cos, cos, cos], axis=-1),
            jnp.concatenate([-sin, sin, -sin, sin], axis=-1))


def _block_diag(w):
    G, a, b = w.shape
    return jnp.einsum('gab,gk->gakb', w, jnp.eye(G, dtype=w.dtype)).reshape(G * a, G * b)


def _cmp_weights(pe, w1, w2):
    eye = jnp.eye(NSA_KV_HEADS, dtype=F32)
    half = CMP_LEN // 2
    big = lambda w: jnp.einsum('rdf,hk->rhdkf', w, eye).reshape(half * KV_WIDTH, NSA_KV_HEADS * CMP_HIDDEN)
    w1b = jnp.concatenate([big(w1[:half]), big(w1[half:])], axis=1).astype(BF16)
    w2d = jnp.concatenate([w2, w2], axis=1).astype(BF16)
    flat = lambda p: jnp.broadcast_to(p[:, None, :], (half, NSA_KV_HEADS, HEAD_DIM)).reshape(1, half * KV_WIDTH)
    pe8 = jnp.concatenate([flat(pe[:half]), flat(pe[half:]), jnp.zeros((SUBLANES - 2, half * KV_WIDTH), F32)], axis=0)
    return w1b, w2d, pe8


def kernel(x_prompt, x_sample, cache_k_cmp, cache_v_cmp, cache_k_slc, cache_v_slc, state_k_win, state_v_win, state_s5_re, state_s5_im, page_table, norm_mix_w, w_in, s5_lam_re, s5_lam_im, s5_log_dt, s5_b_re, s5_b_im, s5_c_re, s5_c_im, s5_d, s5_w_glu, s5_w_up, q_norm_w, k_norm_w, cmp_pe_k, cmp_wk1, cmp_wk2, cmp_pe_v, cmp_wv1, cmp_wv2, nsa_w_up, w_out, norm_ffn_w, w_ffn1, w_ffn2):
    B, T, _ = x_prompt.shape
    S, TS, _ = x_sample.shape
    n_pages = page_table.shape[1]
    P = n_pages * PAGE_SIZE
    n_pool = cache_k_cmp.shape[1]
    assert norm_mix_w.shape[0] == 1 and B % SUBLANES == 0 and S % SUBLANES == 0
    assert TS < CMP_STRIDE and P % SLC_BLOCK == 0 and n_pages % PAGES_PER_STEP == 0

    w = w_in[0]
    cols = lambda a, n: w[:, a:a + n]
    w_a = jnp.concatenate(
        [cols(0, 512), cols(512, 512), cols(1024, 128), cols(1280, 128), cols(1536, 128),
         cols(1152, 128), cols(1408, 128), cols(1664, 128), cols(1792, N_GATE),
         jnp.zeros((D_MODEL, W_A_COLS - 1792 - N_GATE), F32)], axis=1).astype(BF16)
    w_g = w[:, OFF_GMIX:].astype(BF16)
    nw_mix = norm_mix_w[0][None, :]
    qkw = jnp.concatenate([jnp.tile(q_norm_w[0], NSA_HEADS)]
                          + [jnp.tile(k_norm_w[0, i], NSA_KV_HEADS) for i in range(3)])[None, :]
    li = np.arange(LANES)
    seg = jnp.asarray((li[:, None] // HEAD_DIM) == (li[None, :] // HEAD_DIM), BF16)
    lre, lim = s5_lam_re[0].reshape(1, S5_LANES), s5_lam_im[0].reshape(1, S5_LANES)
    ldt = jnp.repeat(s5_log_dt[0], S5_STATE)[None, :]
    bre = _block_diag(jnp.swapaxes(s5_b_re[0], 1, 2))
    bim = _block_diag(jnp.swapaxes(s5_b_im[0], 1, 2))
    cre = _block_diag(jnp.swapaxes(s5_c_re[0], 1, 2)).astype(BF16)
    cim = _block_diag(jnp.swapaxes(s5_c_im[0], 1, 2)).astype(BF16)
    s5d = s5_d[0][None, :]
    wglu = s5_w_glu[0].astype(BF16)
    wup = s5_w_up[0].astype(BF16)
    cw = _cmp_weights(cmp_pe_k[0], cmp_wk1[0], cmp_wk2[0]) + _cmp_weights(cmp_pe_v[0], cmp_wv1[0], cmp_wv2[0])
    wnsa = nsa_w_up[0].astype(BF16)
    wout = w_out[0].astype(BF16)
    nw_ffn = norm_ffn_w[0][None, :]
    w1 = w_ffn1[0].astype(BF16)
    w2 = w_ffn2[0].astype(BF16)

    def trunk(x, glu, o, tt):
        return _ffn(_merge(x, glu, o, nw_mix, w_g, wup, wnsa, wout, tt), nw_ffn, w1, w2, tt)

    cos_p, sin_p = _rope_tables(jnp.arange(T, dtype=jnp.int32))
    u, q, kc, ks, kw, vc, vs, vw, g = _inproj(x_prompt, cos_p, sin_p, nw_mix, w_a, qkw, seg, 512)
    zeros_st = jnp.zeros((B, S5_LANES), F32)
    glu_tm, sre_p, sim_p = _s5(jnp.swapaxes(u, 0, 1), zeros_st, zeros_st, lre, lim, ldt, bre, bim, cre, cim, s5d, wglu, 64)
    o_p = _nsa_prompt(q, kc, vc, ks, vs, kw, vw, g, cw, 256)
    y_prompt = trunk(x_prompt, jnp.swapaxes(glu_tm, 0, 1), o_p, 512)

    n_s = S * TS
    cos_s, sin_s = _rope_tables(P + jnp.arange(TS, dtype=jnp.int32))
    cos_s, sin_s = jnp.tile(cos_s, (S, 1)), jnp.tile(sin_s, (S, 1))
    xs = x_sample.reshape(1, n_s, D_MODEL)
    us, qs, kcs, kss, kws, vcs, vss, vws, gs = _inproj(xs, cos_s, sin_s, nw_mix, w_a, qkw, seg, n_s)
    seq = lambda a: a.reshape(S, TS, a.shape[-1])
    u_tm = jnp.swapaxes(seq(us), 0, 1)
    glu_s_tm, sre_s, sim_s = _s5(u_tm, state_s5_re[0].reshape(S, S5_LANES), state_s5_im[0].reshape(S, S5_LANES),
                                 lre, lim, ldt, bre, bim, cre, cim, s5d, wglu, TS)
    pool = lambda c: jnp.transpose(c[0], (0, 2, 3, 1)).reshape(n_pool, KV_WIDTH, PAGE_SIZE)
    ocmp_s, sel_s = _nsa_sample_cmp(page_table, seq(qs), pool(cache_k_cmp), pool(cache_v_cmp), cw)
    win = lambda s: jnp.transpose(s[0], (0, 2, 3, 1)).reshape(S, KV_WIDTH, s.shape[2])
    o_s, kwin_s, vwin_s = _nsa_sample_slc(page_table, seq(qs), sel_s, ocmp_s, seq(gs), seq(kss), seq(vss), seq(kws),
                                          seq(vws), win(state_k_win), win(state_v_win),
                                          pool(cache_k_slc), pool(cache_v_slc))
    glu_s = jnp.swapaxes(glu_s_tm, 0, 1).reshape(1, n_s, S5_WIDTH)
    y_sample = trunk(xs, glu_s, o_s.reshape(1, n_s, NSA_WIDTH), n_s).reshape(S, TS, D_MODEL)

    heads_p = lambda a: a.reshape(1, B, a.shape[1], NSA_KV_HEADS, HEAD_DIM)
    heads_s = lambda a: a.reshape(1, S, -1, NSA_KV_HEADS, HEAD_DIM)
    keep = min(WINDOW, T)
    st_p = lambda a: a.reshape(1, B, S5_GROUPS, S5_STATE)
    st_s = lambda a: a.reshape(1, S, S5_GROUPS, S5_STATE)
    return (y_prompt, y_sample,
            heads_p(kc), heads_s(kcs), heads_p(vc), heads_s(vcs),
            heads_p(ks), heads_s(kss), heads_p(vs), heads_s(vss),
            heads_p(kw[:, T - keep:]), heads_s(kwin_s), heads_p(vw[:, T - keep:]), heads_s(vwin_s),
            st_p(sre_p), st_s(sre_s), st_p(sim_p), st_s(sim_s))
```

```python
import functools

import numpy as np
import jax
import jax.numpy as jnp
from jax import lax
from jax.experimental import pallas as pl
from jax.experimental.pallas import tpu as pltpu

F32 = jnp.float32
BF16 = jnp.bfloat16

D_MODEL = 1024
PAGE_SIZE = 128
S5_WIDTH = 512
S5_GROUPS = 32
S5_STATE = 64
S5_LANES = S5_GROUPS * S5_STATE
HEAD_DIM = 64
NSA_HEADS = 8
NSA_KV_HEADS = 2
Q_PER_KV = 4
NSA_WIDTH = 512
KV_WIDTH = 128
CMP_LEN = 32
CMP_STRIDE = 16
CMP_HIDDEN = 128
SLC_BLOCK = 64
SLC_TOPK = 16
WINDOW = 512
ROPE_THETA = 10000.0
EPS = 1e-6
NEG_INF = -1e30
FORCED = 1e9
SCALE = HEAD_DIM ** -0.5
N_GATE = 3 * NSA_HEADS
OFF_GMIX = 2 * 512 + 6 * KV_WIDTH + N_GATE
W_A_COLS = 1920

LANES = 128
SUBLANES = 8
VMEM_LIMIT = 56 * 1024 * 1024
BOUND_SLACK = 1.02
MAX_SHIFT_GAP = 60.0


def _params(sem):
    return pltpu.CompilerParams(dimension_semantics=sem, vmem_limit_bytes=VMEM_LIMIT)


def _const_spec(shape):
    nd = len(shape)
    return pl.BlockSpec(shape, lambda *_: (0,) * nd)


def _dot(a, b):
    return jnp.dot(a, b, preferred_element_type=F32)


def _dot_nt(a, b):
    return lax.dot_general(a, b, (((1,), (1,)), ((), ())), preferred_element_type=F32)


def _rms(x, w):
    r = lax.rsqrt(jnp.mean(x * x, axis=-1, keepdims=True) + EPS)
    return x * r * w


def _lane_iota(shape):
    return lax.broadcasted_iota(jnp.int32, shape, len(shape) - 1)


def _row_iota(shape):
    return lax.broadcasted_iota(jnp.int32, shape, len(shape) - 2)


def _split3(x):
    hi = x.astype(BF16)
    r1 = x - hi.astype(F32)
    mid = r1.astype(BF16)
    lo = (r1 - mid.astype(F32)).astype(BF16)
    return hi, mid, lo


def _inproj_body(x_ref, nw_ref, w_ref, qkw_ref, seg_ref, cos_ref, sin_ref,
                 u_ref, q_ref, kc_ref, ks_ref, kw_ref, vc_ref, vs_ref, vw_ref, g_ref):
    xn = _rms(x_ref[...], nw_ref[...]).astype(BF16)
    z = _dot(xn, w_ref[...])
    u_ref[...] = z[:, :512]
    cos = cos_ref[...]
    sin = sin_ref[...]
    seg = seg_ref[...]
    first_half = (_lane_iota((1, LANES)) % HEAD_DIM) < (HEAD_DIM // 2)

    def norm_rope(t, w):
        ss = _dot((t * t).astype(BF16), seg)
        t = t * lax.rsqrt(ss * (1.0 / HEAD_DIM) + EPS) * w
        rot = jnp.where(first_half, pltpu.roll(t, LANES - HEAD_DIM // 2, 1), pltpu.roll(t, HEAD_DIM // 2, 1))
        return t * cos + rot * sin

    for j in range(4):
        q_ref[:, j * LANES:(j + 1) * LANES] = norm_rope(
            z[:, 512 + j * LANES:512 + (j + 1) * LANES], qkw_ref[:, j * LANES:(j + 1) * LANES])
    for j, r in enumerate((kc_ref, ks_ref, kw_ref)):
        r[...] = norm_rope(z[:, 1024 + j * LANES:1024 + (j + 1) * LANES],
                           qkw_ref[:, 512 + j * LANES:512 + (j + 1) * LANES])
    vc_ref[...] = z[:, 1408:1536]
    vs_ref[...] = z[:, 1536:1664]
    vw_ref[...] = z[:, 1664:1792]
    g_ref[...] = z[:, 1792:1920]


def _inproj(x, cos_t, sin_t, nw, w_a, qkw, seg, tt):
    B, T, _ = x.shape
    row = lambda w: pl.BlockSpec((None, tt, w), lambda b, i: (b, i, 0))
    tab = pl.BlockSpec((tt, LANES), lambda b, i: (i, 0))
    kv = jax.ShapeDtypeStruct((B, T, KV_WIDTH), F32)
    out_shape = [jax.ShapeDtypeStruct((B, T, 512), F32), jax.ShapeDtypeStruct((B, T, 512), F32)] + [kv] * 7
    return pl.pallas_call(
        _inproj_body,
        grid=(B, T // tt),
        in_specs=[row(D_MODEL), _const_spec((1, D_MODEL)), _const_spec((D_MODEL, W_A_COLS)),
                  _const_spec((1, 896)), _const_spec((LANES, LANES)), tab, tab],
        out_specs=[row(512), row(512)] + [row(KV_WIDTH)] * 7,
        out_shape=out_shape,
        compiler_params=_params(("parallel", "parallel")),
        name="inproj",
    )(x, nw, w_a, qkw, seg, cos_t, sin_t)


def _s5_body(u_ref, sre0_ref, sim0_ref, lre_ref, lim_ref, ldt_ref, bre_ref, bim_ref, cre_ref, cim_ref,
             d_ref, wglu_ref, out_ref, sre_ref, sim_ref,
             bbar_ref, a_ref, st_ref, xr_ref, xi_ref, *, tc):
    c = pl.program_id(1)
    rows = tc * SUBLANES

    @pl.when(c == 0)
    def _():
        lr, li = lre_ref[...], lim_ref[...]
        dt = jnp.exp(ldt_ref[...])
        mag = jnp.exp(lr * dt)
        ab_re, ab_im = mag * jnp.cos(li * dt), mag * jnp.sin(li * dt)
        den = lr * lr + li * li
        f_re = ((ab_re - 1.0) * lr + ab_im * li) / den
        f_im = (ab_im * lr - (ab_re - 1.0) * li) / den
        bre, bim = bre_ref[...], bim_ref[...]
        bbar_ref[:, :S5_LANES] = (f_re * bre - f_im * bim).astype(BF16)
        bbar_ref[:, S5_LANES:] = (f_re * bim + f_im * bre).astype(BF16)
        a_ref[0] = jnp.broadcast_to(ab_re, (SUBLANES, S5_LANES))
        a_ref[1] = jnp.broadcast_to(ab_im, (SUBLANES, S5_LANES))
        st_ref[0] = sre0_ref[...]
        st_ref[1] = sim0_ref[...]

    u = u_ref[...].reshape(rows, S5_WIDTH)
    ub = u.astype(BF16)
    hw, hl = S5_WIDTH // 2, S5_LANES // 2
    for n in range(2):
        xr_ref[:, n * hl:(n + 1) * hl] = _dot(ub[:, n * hw:(n + 1) * hw], bbar_ref[n * hw:(n + 1) * hw, n * hl:(n + 1) * hl])
        xi_ref[:, n * hl:(n + 1) * hl] = _dot(ub[:, n * hw:(n + 1) * hw],
                                              bbar_ref[n * hw:(n + 1) * hw, S5_LANES + n * hl:S5_LANES + (n + 1) * hl])

    slab = 512
    for lc in range(S5_LANES // slab):
        sl = slice(lc * slab, (lc + 1) * slab)
        ar, ai = a_ref[0, :, sl], a_ref[1, :, sl]

        def step(t, carry, sl=sl, ar=ar, ai=ai):
            sr, si = carry
            r0 = pl.multiple_of(t * SUBLANES, SUBLANES)
            nr = ar * sr - ai * si + xr_ref[pl.ds(r0, SUBLANES), sl]
            ni = ar * si + ai * sr + xi_ref[pl.ds(r0, SUBLANES), sl]
            xr_ref[pl.ds(r0, SUBLANES), sl] = nr
            xi_ref[pl.ds(r0, SUBLANES), sl] = ni
            return nr, ni

        sr, si = lax.fori_loop(0, tc, step, (st_ref[0, :, sl], st_ref[1, :, sl]), unroll=min(tc, 8))
        st_ref[0, :, sl] = sr
        st_ref[1, :, sl] = si

    y = jnp.concatenate(
        [_dot(xr_ref[:, n * hl:(n + 1) * hl].astype(BF16), cre_ref[n * hl:(n + 1) * hl, n * hw:(n + 1) * hw])
         - _dot(xi_ref[:, n * hl:(n + 1) * hl].astype(BF16), cim_ref[n * hl:(n + 1) * hl, n * hw:(n + 1) * hw])
         for n in range(2)], axis=1) + d_ref[...] * u
    yg = jax.nn.gelu(y)
    glu = yg * jax.nn.sigmoid(_dot(yg.astype(BF16), wglu_ref[...]))
    out_ref[...] = glu.reshape(tc, SUBLANES, S5_WIDTH)

    @pl.when(c == pl.num_programs(1) - 1)
    def _():
        sre_ref[...] = st_ref[0]
        sim_ref[...] = st_ref[1]


def _s5(u_tm, sre0, sim0, lre, lim, ldt, bre, bim, cre, cim, d, wglu, tc):
    T, B, _ = u_tm.shape
    st_spec = pl.BlockSpec((SUBLANES, S5_LANES), lambda b, c: (b, 0))
    blk = pl.BlockSpec((tc, SUBLANES, S5_WIDTH), lambda b, c: (c, b, 0))
    st = jax.ShapeDtypeStruct((B, S5_LANES), F32)
    return pl.pallas_call(
        functools.partial(_s5_body, tc=tc),
        grid=(B // SUBLANES, T // tc),
        in_specs=[blk, st_spec, st_spec,
                  _const_spec((1, S5_LANES)), _const_spec((1, S5_LANES)), _const_spec((1, S5_LANES)),
                  _const_spec((S5_WIDTH, S5_LANES)), _const_spec((S5_WIDTH, S5_LANES)),
                  _const_spec((S5_LANES, S5_WIDTH)), _const_spec((S5_LANES, S5_WIDTH)),
                  _const_spec((1, S5_WIDTH)), _const_spec((S5_WIDTH, S5_WIDTH))],
        out_specs=[blk, st_spec, st_spec],
        out_shape=[jax.ShapeDtypeStruct((T, B, S5_WIDTH), F32), st, st],
        scratch_shapes=[pltpu.VMEM((S5_WIDTH, 2 * S5_LANES), BF16),
                        pltpu.VMEM((2, SUBLANES, S5_LANES), F32),
                        pltpu.VMEM((2, SUBLANES, S5_LANES), F32),
                        pltpu.VMEM((tc * SUBLANES, S5_LANES), F32),
                        pltpu.VMEM((tc * SUBLANES, S5_LANES), F32)],
        compiler_params=_params(("parallel", "arbitrary")),
        name="s5",
    )(u_tm, sre0, sim0, lre, lim, ldt, bre, bim, cre, cim, d, wglu)


def _head_lanes(x, half):
    return jnp.where((_lane_iota((1, LANES)) >> 6) == half, x, 0.0)


def _masked_softmax_parts(s, mask):
    s = jnp.where(mask, s, NEG_INF)
    m = jnp.max(s, axis=-1, keepdims=True)
    p = jnp.where(mask, jnp.exp(s - m), 0.0)
    l = jnp.maximum(jnp.sum(p, axis=-1, keepdims=True), 1e-30)
    return p, l


def _cmp_hidden(h, w1_ref, pe_ref):
    m = h.shape[0]
    pe = pe_ref[...].astype(BF16)
    bias = _dot(pe, w1_ref[:, :256])[0:1] + _dot(pe, w1_ref[:, 256:])[1:2]
    return jax.nn.gelu(h[:, :256] + pltpu.roll(h[:, 256:], m - 1, 0) + bias)


def _overlap(n_rows, n_cols, transpose):
    shape = (n_cols, n_rows) if transpose else (n_rows, n_cols)
    n = _row_iota(shape) if not transpose else _lane_iota(shape)
    j = _lane_iota(shape) if not transpose else _row_iota(shape)
    lo = jnp.maximum(n * CMP_STRIDE, j * SLC_BLOCK)
    hi = jnp.minimum(n * CMP_STRIDE + CMP_LEN, j * SLC_BLOCK + SLC_BLOCK)
    return (jnp.maximum(hi - lo, 0).astype(F32) * (1.0 / CMP_LEN)).astype(BF16)


VT_ROWS = HEAD_DIM + 16


def _ones_row_rows(n):
    return jnp.where(_row_iota((VT_ROWS - HEAD_DIM, n)) == 0, 1.0, 0.0)


def _shifted_update_t(m_ref, acc_ref, h, s_t, v_t):
    tk, tq = s_t.shape
    m_ref[h] = jnp.maximum(m_ref[h], jnp.max(s_t.reshape(tk // SUBLANES, SUBLANES, tq), axis=0))
    acc_ref[h] = acc_ref[h] + _dot(v_t, jnp.exp(s_t).astype(BF16))


def _online_update_t(m_ref, acc_ref, h, s_t, v_t):
    m_old = m_ref[h]
    m_new = jnp.maximum(m_old, jnp.max(s_t, axis=0, keepdims=True))
    p_t = jnp.exp(s_t - m_new[0:1])
    alpha = jnp.exp(m_old - m_new)
    acc_ref[h] = alpha[0:1] * acc_ref[h] + _dot(v_t, p_t.astype(BF16))
    m_ref[h] = m_new


def _softmax_result_t(m_ref, acc_ref, h):
    acc = acc_ref[h]
    out = acc[0:HEAD_DIM] / jnp.maximum(acc[HEAD_DIM:HEAD_DIM + 1], 1e-30)
    return jnp.where(jnp.max(m_ref[h], axis=0, keepdims=True) > 0.5 * NEG_INF, out, 0.0)


def _key_rows(k, kv):
    lane = _lane_iota((1, LANES))
    kk = jnp.where(lane < HEAD_DIM, jnp.where(kv == 0, k, pltpu.roll(k, HEAD_DIM, 1)), 0.0)
    norm2 = jnp.max(jnp.sum(kk * kk, axis=1, keepdims=True), axis=0, keepdims=True)
    return jnp.where(lane == HEAD_DIM, 1.0, kk).astype(BF16), jnp.sqrt(norm2)


def _nsa_prompt_body(q_ref, kc16_ref, vc16_ref, ks_ref, kw_ref, vs_ref, vw_ref, g_ref,
                     w1k_ref, w2k_ref, pek_ref, w1v_ref, w2v_ref, pev_ref,
                     o_ref,
                     ksa_ref, kwa_ref, vst_ref, vwt_ref, kcmp_ref, vcmpt_ref, knorm_ref,
                     qc_ref, qs_ref, qw_ref, selb_ref, ms_ref, accs_ref, mw_ref, accw_ref, *, T, TQ):
    kv = pl.program_id(1)
    qi = pl.program_id(2)
    m_cmp = T // CMP_STRIDE
    n_cmp = m_cmp - 1
    n_slc = T // SLC_BLOCK
    topk = min(SLC_TOPK, n_slc)
    n_kt = T // TQ
    lane = _lane_iota((1, LANES))

    @pl.when(qi == 0)
    def _():
        ksa_ref[...], ns = _key_rows(ks_ref[...], kv)
        kwa_ref[...], nw = _key_rows(kw_ref[...], kv)
        knorm_ref[0] = jnp.broadcast_to(ns, (SUBLANES, LANES))
        knorm_ref[1] = jnp.broadcast_to(nw, (SUBLANES, LANES))
        ones = _ones_row_rows(TQ).astype(BF16)
        for kt in range(n_kt):
            for src, dst in ((vs_ref, vst_ref), (vw_ref, vwt_ref)):
                vt = src[kt * TQ:(kt + 1) * TQ, :].T
                dst[kt, 0:HEAD_DIM, :] = jnp.where(kv == 0, vt[0:HEAD_DIM], vt[HEAD_DIM:]).astype(BF16)
                dst[kt, HEAD_DIM:, :] = ones

        def compressed(x16_ref, w1_ref, pe_ref, w2_ref):
            hid = _cmp_hidden(_dot(x16_ref[...].astype(BF16), w1_ref[...]), w1_ref, pe_ref)
            hid = jnp.where(kv == 0, hid[:, :LANES], hid[:, LANES:])
            return _dot(hid.astype(BF16), w2_ref[...])

        kcmp_ref[...] = compressed(kc16_ref, w1k_ref, pek_ref, w2k_ref).astype(BF16)
        vcmp_t = compressed(vc16_ref, w1v_ref, pev_ref, w2v_ref).T
        vcmpt_ref[0:HEAD_DIM, :] = vcmp_t[0:HEAD_DIM].astype(BF16)
        vcmpt_ref[HEAD_DIM:, :] = jnp.zeros((HEAD_DIM, m_cmp), BF16)

    r0 = pl.multiple_of(qi * TQ, TQ)
    for pp in range(2):
        qt = q_ref[pl.ds(r0, TQ), pp * LANES:(pp + 1) * LANES] * SCALE
        for e in range(2):
            qh = jnp.where(lane < HEAD_DIM, qt if e == 0 else pltpu.roll(qt, HEAD_DIM, 1), 0.0)
            qc_ref[2 * pp + e] = qh.astype(BF16)
            qnorm = jnp.sqrt(jnp.sum(qh * qh, axis=1, keepdims=True))
            for dst, which in ((qs_ref, 0), (qw_ref, 1)):
                bound = BOUND_SLACK * qnorm * knorm_ref[which][0:1, 0:1]
                dst[(2 * pp + e) * TQ:(2 * pp + e + 1) * TQ, :] = jnp.where(lane == HEAD_DIM, -bound, qh).astype(BF16)
    qpos = r0 + _lane_iota((1, TQ))

    n = _row_iota((m_cmp, TQ))
    mask_c = (n * CMP_STRIDE + CMP_LEN - 1 <= qpos) & (n < n_cmp)
    pg = jnp.zeros((m_cmp, TQ), F32)
    o_cmp = []
    for h in range(Q_PER_KV):
        s_t = jnp.where(mask_c, _dot_nt(kcmp_ref[...], qc_ref[h]), NEG_INF)
        p_t = jnp.where(mask_c, jnp.exp(s_t - jnp.max(s_t, axis=0, keepdims=True)), 0.0)
        inv_l = 1.0 / jnp.maximum(jnp.sum(p_t, axis=0, keepdims=True), 1e-30)
        pg = pg + p_t * inv_l
        o_cmp.append(_dot(vcmpt_ref[...], p_t.astype(BF16))[0:HEAD_DIM] * inv_l)

    ovt = _overlap(m_cmp, n_slc, transpose=True)
    imp = sum(_dot(ovt, part) for part in _split3(pg))
    j = _row_iota((n_slc, TQ))
    qblk = qpos // SLC_BLOCK
    valid = j <= qblk
    forced = (j == 0) | (j == qblk) | (j == qblk - 1)
    score = jnp.where(forced, FORCED, jnp.where(valid, imp, NEG_INF))
    rank = jnp.zeros((n_slc, TQ), jnp.int32)
    for i in range(n_slc):
        si = score[i:i + 1, :]
        beats = (si > score) | ((si == score) & (i < j))
        rank = rank + beats.astype(jnp.int32)
    def reset():
        for m_r, acc_r in ((ms_ref, accs_ref), (mw_ref, accw_ref)):
            m_r[...] = jnp.full(m_r.shape, NEG_INF, F32)
            acc_r[...] = jnp.zeros(acc_r.shape, F32)

    sel_bias = jnp.where((rank < topk) & valid, 0.0, NEG_INF)
    for jb in range(n_slc):
        selb_ref[jb] = jnp.broadcast_to(sel_bias[jb:jb + 1], (SUBLANES, TQ))
    not_after = _row_iota((TQ, 1)) <= _lane_iota((1, TQ))
    bias_diag = jnp.where(not_after, 0.0, NEG_INF)
    bias_far = jnp.where(not_after, NEG_INF, 0.0)
    blocks_per_tile = TQ // SLC_BLOCK
    win_tiles = WINDOW // TQ

    def both_branches(update):
        def apply(m_r, acc_r, s_all, bias, v_t):
            for h in range(Q_PER_KV):
                update(m_r, acc_r, h, s_all[:, h * TQ:(h + 1) * TQ] + bias, v_t)

        def slc_scores(kb):
            return _dot_nt(ksa_ref[pl.ds(pl.multiple_of(kb * TQ, TQ), TQ), :], qs_ref[...])

        def win_scores(kb):
            return _dot_nt(kwa_ref[pl.ds(pl.multiple_of(kb * TQ, TQ), TQ), :], qw_ref[...])

        def slc_bias(kb):
            rows = []
            for jb in range(blocks_per_tile):
                rows += [selb_ref[kb * blocks_per_tile + jb]] * (SLC_BLOCK // SUBLANES)
            return jnp.concatenate(rows, axis=0)

        def slc_pair(p, carry):
            ka = 2 * p
            kb = jnp.minimum(ka + 1, n_kt - 1)
            live = jnp.where(ka + 1 < qi, 0.0, NEG_INF)
            s_a, s_b = slc_scores(ka), slc_scores(kb)
            apply(ms_ref, accs_ref, s_a, slc_bias(ka), vst_ref[ka])
            apply(ms_ref, accs_ref, s_b, slc_bias(kb) + live, vst_ref[kb])
            return carry

        lax.fori_loop(0, (qi + 1) // 2, slc_pair, 0)

        win_kb = [jnp.maximum(qi - d, 0) for d in range(win_tiles + 1)]
        s_own = slc_scores(qi)
        s_win = [win_scores(kb) for kb in win_kb]
        apply(ms_ref, accs_ref, s_own, slc_bias(qi) + bias_diag, vst_ref[qi])
        for d, kb in enumerate(win_kb):
            exists = jnp.where(qi >= d, 0.0, NEG_INF)
            bias = exists + (bias_diag if d == 0 else bias_far if d == win_tiles else 0.0)
            apply(mw_ref, accw_ref, s_win[d], bias, vwt_ref[kb])

    reset()
    both_branches(_shifted_update_t)

    worst = jnp.full((1, TQ), 0.0, F32)
    for m_r in (ms_ref, mw_ref):
        for h in range(Q_PER_KV):
            top = jnp.max(m_r[h], axis=0, keepdims=True)
            worst = jnp.minimum(worst, jnp.where(top > 0.5 * NEG_INF, top, 0.0))
    redo = jnp.min(worst) < -MAX_SHIFT_GAP

    @pl.when(redo)
    def _():
        reset()
        both_branches(_online_update_t)

    g = g_ref[pl.ds(r0, TQ), :]
    gs_t = jax.nn.sigmoid(jnp.where(kv == 0, g, pltpu.roll(g, LANES - 3 * Q_PER_KV, 1))).T
    for pp in range(2):
        halves = []
        for e in range(2):
            h = 2 * pp + e
            halves.append(gs_t[3 * h:3 * h + 1] * o_cmp[h]
                          + gs_t[3 * h + 1:3 * h + 2] * _softmax_result_t(ms_ref, accs_ref, h)
                          + gs_t[3 * h + 2:3 * h + 3] * _softmax_result_t(mw_ref, accw_ref, h))
        o_ref[pl.ds(r0, TQ), pp * LANES:(pp + 1) * LANES] = jnp.concatenate(halves, axis=0).T


def _nsa_prompt(q, kc, vc, ks, vs, kw, vw, g, cw, tq):
    B, T, _ = q.shape
    m_cmp = T // CMP_STRIDE
    kc16 = kc.reshape(B, m_cmp, CMP_STRIDE * KV_WIDTH)
    vc16 = vc.reshape(B, m_cmp, CMP_STRIDE * KV_WIDTH)
    full = lambda w: pl.BlockSpec((None, T, w), lambda b, h, i: (b, 0, 0))
    x16 = pl.BlockSpec((None, m_cmp, CMP_STRIDE * KV_WIDTH), lambda b, h, i: (b, 0, 0))
    qo = pl.BlockSpec((None, T, 2 * LANES), lambda b, h, i: (b, 0, h))
    wspecs = [_const_spec(w.shape) for w in cw]
    k_rows = pltpu.VMEM((T, LANES), BF16)
    v_t = pltpu.VMEM((T // tq, VT_ROWS, tq), BF16)
    q_rows = pltpu.VMEM((Q_PER_KV, tq, LANES), BF16)
    q_all = pltpu.VMEM((Q_PER_KV * tq, LANES), BF16)
    run_max = pltpu.VMEM((Q_PER_KV, SUBLANES, tq), F32)
    run_acc = pltpu.VMEM((Q_PER_KV, VT_ROWS, tq), F32)
    return pl.pallas_call(
        functools.partial(_nsa_prompt_body, T=T, TQ=tq),
        grid=(B, NSA_KV_HEADS, T // tq),
        in_specs=[qo, x16, x16, full(KV_WIDTH), full(KV_WIDTH), full(KV_WIDTH), full(KV_WIDTH), full(KV_WIDTH)] + wspecs,
        out_specs=qo,
        out_shape=jax.ShapeDtypeStruct((B, T, NSA_WIDTH), F32),
        scratch_shapes=[k_rows, k_rows, v_t, v_t,
                        pltpu.VMEM((m_cmp, LANES), BF16), pltpu.VMEM((LANES, m_cmp), BF16),
                        pltpu.VMEM((2, SUBLANES, LANES), F32),
                        q_rows, q_all, q_all, pltpu.VMEM((T // SLC_BLOCK, SUBLANES, tq), F32),
                        run_max, run_acc, run_max, run_acc],
        compiler_params=_params(("parallel", "arbitrary", "arbitrary")),
        name="nsa_prompt",
    )(q, kc16, vc16, ks, kw, vs, vw, g, *cw)


def _stack_heads(q, kv, lane_half):
    rows = []
    for hl in range(Q_PER_KV):
        tile = 2 * kv + hl // 2
        x = _head_lanes(q[:, tile * LANES:(tile + 1) * LANES], hl % 2)
        rows.append(x if lane_half(hl) == hl % 2 else pltpu.roll(x, HEAD_DIM, 1))
    return jnp.concatenate(rows, axis=0)


class _PageFetch:
    def __init__(self, pt_ref, caches, bufs, sem_ref, n_pages):
        self.pt_ref, self.caches, self.bufs, self.sem_ref, self.n_pages = pt_ref, caches, bufs, sem_ref, n_pages
        self.b = pl.program_id(0)
        self.last = pl.num_programs(0) - 1
        self.slot = self.b % 2
        self.nxt = jnp.where(self.b == self.last, 0, self.b + 1)

    def _copy(self, which, seq, k, slot):
        return pltpu.make_async_copy(self.caches[which].at[self.pt_ref[seq, k]], self.bufs[which].at[slot, k],
                                     self.sem_ref.at[slot, which])

    def begin(self):
        @pl.when(self.b == 0)
        def _():
            for k in range(self.n_pages):
                for which in range(len(self.caches)):
                    self._copy(which, 0, k, 0).start()

        for k in range(self.n_pages):
            for which in range(len(self.caches)):
                self._copy(which, self.b, k, self.slot).wait()

    def prefetch(self, k):
        for which in range(len(self.caches)):
            self._copy(which, self.nxt, k, 1 - self.slot).start()

    def page(self, which, k):
        return self.bufs[which][self.slot, k]

    def end(self):
        @pl.when(self.b == self.last)
        def _():
            for k in range(self.n_pages):
                for which in range(len(self.caches)):
                    self._copy(which, self.nxt, k, 1 - self.slot).wait()


def _nsa_sample_cmp_body(pt_ref, q_ref, ck_ref, cv_ref, w1k_ref, w2k_ref, pek_ref, w1v_ref, w2v_ref, pev_ref,
                         ocmp_ref, sel_ref, xk_ref, xv_ref, kbuf_ref, vbuf_ref, sem_ref, *, P, TS):
    n_pages = P // PAGE_SIZE
    m_cmp = P // CMP_STRIDE
    n_cmp = (P + TS - CMP_LEN) // CMP_STRIDE + 1
    n_slc = -(-(P + TS) // SLC_BLOCK)
    n_slc_pad = -(-n_slc // LANES) * LANES
    topk = min(SLC_TOPK, n_slc)
    fetch = _PageFetch(pt_ref, (ck_ref, cv_ref), (kbuf_ref, vbuf_ref), sem_ref, n_pages)
    fetch.begin()

    gpp = PAGE_SIZE // CMP_STRIDE
    ri = _row_iota((PAGE_SIZE, PAGE_SIZE))
    li = _lane_iota((PAGE_SIZE, PAGE_SIZE))
    pick = jnp.where(ri == CMP_STRIDE * (li % gpp) + li // gpp, 1.0, 0.0).astype(BF16)
    for k in range(n_pages):
        fetch.prefetch(k)
        for which, x_ref in ((0, xk_ref), (1, xv_ref)):
            rows_by_r = _dot(fetch.page(which, k).astype(BF16), pick).T
            for r in range(CMP_STRIDE):
                x_ref[r, k * gpp:(k + 1) * gpp, :] = rows_by_r[r * gpp:(r + 1) * gpp, :]
    hid = []
    for x_ref, w1_ref, pe_ref in ((xk_ref, w1k_ref, pek_ref), (xv_ref, w1v_ref, pev_ref)):
        h = jnp.zeros((m_cmp, 4 * LANES), F32)
        for i in range(CMP_STRIDE // 2):
            pair = jnp.concatenate([x_ref[2 * i], x_ref[2 * i + 1]], axis=1)
            h = h + _dot(pair.astype(BF16), w1_ref[2 * i * KV_WIDTH:(2 * i + 2) * KV_WIDTH, :])
        hid.append(_cmp_hidden(h, w1_ref, pe_ref).astype(BF16))
    hidk, hidv = hid

    q = q_ref[...] * SCALE
    rows = Q_PER_KV * TS
    t = _row_iota((rows, 1)) % TS
    qpos = P + t
    n = _lane_iota((rows, m_cmp))
    mask = (n * CMP_STRIDE + CMP_LEN - 1 <= qpos) & (n < n_cmp)
    ov = _overlap(m_cmp, n_slc_pad, transpose=False)
    jl = _lane_iota((TS, n_slc_pad))
    qblk = (P + _row_iota((TS, 1))) // SLC_BLOCK
    valid = (jl <= qblk) & (jl < n_slc)
    forced = ((jl == 0) | (jl == qblk) | (jl == qblk - 1)) & (jl < n_slc)
    imps = []
    for kv in range(NSA_KV_HEADS):
        kcmp = _dot(hidk[:, kv * LANES:(kv + 1) * LANES], w2k_ref[...]).astype(BF16)
        vcmp = _dot(hidv[:, kv * LANES:(kv + 1) * LANES], w2v_ref[...]).astype(BF16)
        qs = _stack_heads(q, kv, lambda hl: hl % 2).astype(BF16)
        p, l = _masked_softmax_parts(_dot_nt(qs, kcmp), mask)
        ocmp_ref[kv] = _dot(p.astype(BF16), vcmp) / l
        pn = p / l
        pg = pn[0:TS] + pn[TS:2 * TS] + pn[2 * TS:3 * TS] + pn[3 * TS:4 * TS]
        imps.append(sum(_dot(part, ov) for part in _split3(pg)))
    tile2 = lambda a: jnp.concatenate([a] * NSA_KV_HEADS, axis=0)
    valid2, jl2 = tile2(valid), tile2(jl)
    score = jnp.where(tile2(forced), FORCED, jnp.where(valid2, jnp.concatenate(imps, axis=0), NEG_INF))
    rank = jnp.zeros(score.shape, jnp.int32)
    for i in range(n_slc):
        si = score[:, i:i + 1]
        beats = (si > score) | ((si == score) & (i < jl2))
        rank = rank + beats.astype(jnp.int32)
    sel = jnp.where((rank < topk) & valid2, 1.0, 0.0)
    for kv in range(NSA_KV_HEADS):
        for ch in range(n_slc_pad // LANES):
            sel_ref[kv, ch] = sel[kv * TS:(kv + 1) * TS, ch * LANES:(ch + 1) * LANES]
    fetch.end()


def _page_scratch(n_pages, n_caches):
    return ([pltpu.VMEM((2, n_pages, KV_WIDTH, PAGE_SIZE), F32)] * n_caches
            + [pltpu.SemaphoreType.DMA((2, n_caches))])


def _nsa_sample_cmp(page_table, q, ck_t, cv_t, cw):
    S, TS, _ = q.shape
    n_pages = page_table.shape[1]
    P = n_pages * PAGE_SIZE
    m_cmp = P // CMP_STRIDE
    n_slc_pad = -(-(-(-(P + TS) // SLC_BLOCK)) // LANES) * LANES
    hbm = pl.BlockSpec(memory_space=pl.ANY)
    wspecs = [pl.BlockSpec(w.shape, lambda b, pt, nd=w.ndim: (0,) * nd) for w in cw]
    grid_spec = pltpu.PrefetchScalarGridSpec(
        num_scalar_prefetch=1,
        grid=(S,),
        in_specs=[pl.BlockSpec((None, TS, NSA_WIDTH), lambda b, pt: (b, 0, 0)), hbm, hbm] + wspecs,
        out_specs=[pl.BlockSpec((None, NSA_KV_HEADS, Q_PER_KV * TS, LANES), lambda b, pt: (b, 0, 0, 0)),
                   pl.BlockSpec((None, NSA_KV_HEADS, n_slc_pad // LANES, TS, LANES), lambda b, pt: (b, 0, 0, 0, 0))],
        scratch_shapes=[pltpu.VMEM((CMP_STRIDE, m_cmp, KV_WIDTH), F32), pltpu.VMEM((CMP_STRIDE, m_cmp, KV_WIDTH), F32)]
        + _page_scratch(n_pages, 2),
    )
    return pl.pallas_call(
        functools.partial(_nsa_sample_cmp_body, P=P, TS=TS),
        grid_spec=grid_spec,
        out_shape=[jax.ShapeDtypeStruct((S, NSA_KV_HEADS, Q_PER_KV * TS, LANES), F32),
                   jax.ShapeDtypeStruct((S, NSA_KV_HEADS, n_slc_pad // LANES, TS, LANES), F32)],
        compiler_params=_params(("arbitrary",)),
        name="nsa_sample_cmp",
    )(page_table, q, ck_t, cv_t, *cw)


def _nsa_sample_slc_body(pt_ref, q_ref, sel_ref, ocmp_ref, g_ref, ksn_ref, vsn_ref, kwn_ref, vwn_ref,
                         kwin_ref, vwin_ref, ck_ref, cv_ref, o_ref, kwout_ref, vwout_ref,
                         kt_ref, vt_ref, m_ref, l_ref, acc_ref, kbuf_ref, vbuf_ref, sem_ref, *, P, TS):
    n_pages = P // PAGE_SIZE
    rows = NSA_KV_HEADS * Q_PER_KV * TS
    wb = kwin_ref.shape[1]
    lane = _lane_iota((1, LANES))
    t = _row_iota((rows, 1)) % TS

    fetch = _PageFetch(pt_ref, (ck_ref, cv_ref), (kbuf_ref, vbuf_ref), sem_ref, n_pages)
    fetch.begin()

    def sel_rows(ch):
        return jnp.concatenate([sel_ref[kv, ch] for kv in range(NSA_KV_HEADS) for _ in range(Q_PER_KV)], axis=0)

    def update(s, pv_of):
        m_old = m_ref[...]
        m_new = jnp.maximum(m_old, jnp.max(s, axis=-1, keepdims=True))
        p = jnp.exp(s - m_new[:, 0:1])
        alpha = jnp.exp(m_old - m_new)
        l_ref[...] = alpha * l_ref[...] + jnp.sum(p, axis=-1, keepdims=True)
        acc_ref[...] = alpha * acc_ref[...] + pv_of(p.astype(BF16))
        m_ref[...] = m_new

    def result():
        return jnp.where(m_ref[...] > 0.5 * NEG_INF, acc_ref[...] / jnp.maximum(l_ref[...], 1e-30), 0.0)

    def reset():
        m_ref[...] = jnp.full(m_ref.shape, NEG_INF, F32)
        l_ref[...] = jnp.zeros(l_ref.shape, F32)
        acc_ref[...] = jnp.zeros(acc_ref.shape, F32)

    q = q_ref[...] * SCALE
    q2 = jnp.concatenate([_stack_heads(q, kv, lambda hl, kv=kv: kv) for kv in range(NSA_KV_HEADS)],
                         axis=0).astype(BF16)
    reset()

    for k in range(n_pages):
        fetch.prefetch(k)
        kt_ref[:, k * PAGE_SIZE:(k + 1) * PAGE_SIZE] = fetch.page(0, k).astype(BF16)
        vt_ref[:, k * PAGE_SIZE:(k + 1) * PAGE_SIZE] = fetch.page(1, k).astype(BF16)
    chosen = []
    for i in range(P // LANES):
        b0 = i * (LANES // SLC_BLOCK)
        sel_c = sel_rows(b0 // LANES)
        chosen.append(jnp.where(lane < SLC_BLOCK,
                                jnp.broadcast_to(sel_c[:, b0 % LANES:b0 % LANES + 1], (rows, LANES)),
                                jnp.broadcast_to(sel_c[:, b0 % LANES + 1:b0 % LANES + 2], (rows, LANES))))
    kpos = _lane_iota((1, P))
    bias = jnp.where((jnp.concatenate(chosen, axis=1) > 0.5) & (kpos <= P + t), 0.0, NEG_INF)
    update(_dot(q2, kt_ref[...]) + bias, lambda p: _dot_nt(p, vt_ref[...]))

    pad_rows = lambda x: jnp.concatenate([x, jnp.zeros((LANES - TS, LANES), F32)], axis=0).astype(BF16)
    own = (lane >> 6) == (_row_iota((rows, 1)) // (Q_PER_KV * TS))
    jn = P // SLC_BLOCK
    sel_n = sel_rows(jn // LANES)[:, jn % LANES:jn % LANES + 1]
    bias_n = jnp.where((sel_n > 0.5) & (lane <= t) & (lane < TS), 0.0, NEG_INF)
    vn = pad_rows(vsn_ref[...])
    update(_dot_nt(q2, pad_rows(ksn_ref[...])) + bias_n, lambda p: _dot(p, vn))
    o_slc = result()
    reset()
    kw_t, vw_t = kwin_ref[...], vwin_ref[...]
    iw = _lane_iota((1, wb))
    rel = t + wb - iw
    bias_b = jnp.where((rel >= 0) & (rel < WINDOW), 0.0, NEG_INF)
    vw_tb = vw_t.astype(BF16)
    update(_dot(q2, kw_t.astype(BF16)) + bias_b, lambda p: _dot_nt(p, vw_tb))
    bias_w = jnp.where((lane <= t) & (lane < TS), 0.0, NEG_INF)
    vwn = pad_rows(vwn_ref[...])
    update(_dot_nt(q2, pad_rows(kwn_ref[...])) + bias_w, lambda p: _dot(p, vwn))
    o_win = result()
    gs = jax.nn.sigmoid(g_ref[...])
    gcol = lambda comp: jnp.concatenate(
        [gs[:, 3 * hq + comp:3 * hq + comp + 1] for hq in range(NSA_HEADS)], axis=0)
    ocmp = jnp.concatenate([ocmp_ref[kv] for kv in range(NSA_KV_HEADS)], axis=0)
    o = gcol(0) * ocmp + gcol(1) * o_slc + gcol(2) * o_win
    o = jnp.where(own, o, 0.0)
    for tile in range(NSA_HEADS // 2):
        a = o[(2 * tile) * TS:(2 * tile + 1) * TS]
        b = o[(2 * tile + 1) * TS:(2 * tile + 2) * TS]
        kv = (2 * tile) // Q_PER_KV
        a = a if kv == 0 else pltpu.roll(a, HEAD_DIM, 1)
        b = b if kv == 1 else pltpu.roll(b, HEAD_DIM, 1)
        o_ref[:, tile * LANES:(tile + 1) * LANES] = a + b
    kwout_ref[0:wb - TS, :] = kw_t.T[TS:wb, :]
    kwout_ref[wb - TS:wb, :] = kwn_ref[...]
    vwout_ref[0:wb - TS, :] = vw_t.T[TS:wb, :]
    vwout_ref[wb - TS:wb, :] = vwn_ref[...]
    fetch.end()


def _nsa_sample_slc(page_table, q, sel, ocmp, g, ksn, vsn, kwn, vwn, kwin_t, vwin_t, ck_t, cv_t):
    S, TS, _ = q.shape
    n_pages = page_table.shape[1]
    P = n_pages * PAGE_SIZE
    wb = kwin_t.shape[2]
    rows = NSA_KV_HEADS * Q_PER_KV * TS
    hbm = pl.BlockSpec(memory_space=pl.ANY)
    per_seq = lambda shape: pl.BlockSpec((None,) + shape, lambda b, pt, nd=len(shape): (b,) + (0,) * nd)
    grid_spec = pltpu.PrefetchScalarGridSpec(
        num_scalar_prefetch=1,
        grid=(S,),
        in_specs=[per_seq((TS, NSA_WIDTH)), per_seq(sel.shape[1:]), per_seq(ocmp.shape[1:]), per_seq((TS, LANES)),
                  per_seq((TS, KV_WIDTH)), per_seq((TS, KV_WIDTH)), per_seq((TS, KV_WIDTH)), per_seq((TS, KV_WIDTH)),
                  per_seq((KV_WIDTH, wb)), per_seq((KV_WIDTH, wb)), hbm, hbm],
        out_specs=[per_seq((TS, NSA_WIDTH)), per_seq((wb, KV_WIDTH)), per_seq((wb, KV_WIDTH))],
        scratch_shapes=[pltpu.VMEM((KV_WIDTH, P), BF16), pltpu.VMEM((KV_WIDTH, P), BF16)]
        + [pltpu.VMEM((rows, LANES), F32)] * 3 + _page_scratch(n_pages, 2),
    )
    return pl.pallas_call(
        functools.partial(_nsa_sample_slc_body, P=P, TS=TS),
        grid_spec=grid_spec,
        out_shape=[jax.ShapeDtypeStruct((S, TS, NSA_WIDTH), F32),
                   jax.ShapeDtypeStruct((S, wb, KV_WIDTH), F32), jax.ShapeDtypeStruct((S, wb, KV_WIDTH), F32)],
        compiler_params=_params(("arbitrary",)),
        name="nsa_sample_slc",
    )(page_table, q, sel, ocmp, g, ksn, vsn, kwn, vwn, kwin_t, vwin_t, ck_t, cv_t)


def _merge_body(x_ref, glu_ref, o_ref, nw_ref, wg_ref, wup_ref, wnsa_ref, wout_ref, out_ref):
    x = x_ref[...]
    gate = jax.nn.sigmoid(_dot(_rms(x, nw_ref[...]).astype(BF16), wg_ref[...]))
    s5_out = _dot(glu_ref[...].astype(BF16), wup_ref[...])
    nsa_out = _dot(o_ref[...].astype(BF16), wnsa_ref[...])
    merged = gate[:, :D_MODEL] * s5_out + gate[:, D_MODEL:] * nsa_out
    out_ref[...] = x + _dot(merged.astype(BF16), wout_ref[...])


def _merge(x, glu, o, nw, wg, wup, wnsa, wout, tt):
    B, T, _ = x.shape
    row = lambda w: pl.BlockSpec((None, tt, w), lambda b, i: (b, i, 0))
    return pl.pallas_call(
        _merge_body,
        grid=(B, T // tt),
        in_specs=[row(D_MODEL), row(S5_WIDTH), row(NSA_WIDTH), _const_spec(nw.shape), _const_spec(wg.shape),
                  _const_spec(wup.shape), _const_spec(wnsa.shape), _const_spec(wout.shape)],
        out_specs=row(D_MODEL),
        out_shape=jax.ShapeDtypeStruct((B, T, D_MODEL), F32),
        compiler_params=_params(("parallel", "parallel")),
        name="merge",
    )(x, glu, o, nw, wg, wup, wnsa, wout)


def _ffn_body(x_ref, nw_ref, w1_ref, w2_ref, out_ref):
    x = x_ref[...]
    a = jnp.maximum(_dot(_rms(x, nw_ref[...]).astype(BF16), w1_ref[...]), 0.0)
    out_ref[...] = x + _dot((a * a).astype(BF16), w2_ref[...])


def _ffn(x, nw, w1, w2, tt):
    B, T, _ = x.shape
    row = pl.BlockSpec((None, tt, D_MODEL), lambda b, i: (b, i, 0))
    return pl.pallas_call(
        _ffn_body,
        grid=(B, T // tt),
        in_specs=[row, _const_spec(nw.shape), _const_spec(w1.shape), _const_spec(w2.shape)],
        out_specs=row,
        out_shape=jax.ShapeDtypeStruct((B, T, D_MODEL), F32),
        compiler_params=_params(("parallel", "parallel")),
        name="ffn",
    )(x, nw, w1, w2)


def _rope_tables(pos):
    half = HEAD_DIM // 2
    inv = ROPE_THETA ** (-jnp.arange(half, dtype=F32) / half)
    ang = pos.astype(F32)[:, None] * inv[None, :]
    cos, sin = jnp.cos(ang), jnp.sin(ang)
    return (jnp.concatenate([cos, cos, cos, cos], axis=-1),
            jnp.concatenate([-sin, sin, -sin, sin], axis=-1))


def _block_diag(w):
    G, a, b = w.shape
    return jnp.einsum('gab,gk->gakb', w, jnp.eye(G, dtype=w.dtype)).reshape(G * a, G * b)


def _cmp_weights(pe, w1, w2):
    eye = jnp.eye(NSA_KV_HEADS, dtype=F32)
    half = CMP_LEN // 2
    big = lambda w: jnp.einsum('rdf,hk->rhdkf', w, eye).reshape(half * KV_WIDTH, NSA_KV_HEADS * CMP_HIDDEN)
    w1b = jnp.concatenate([big(w1[:half]), big(w1[half:])], axis=1).astype(BF16)
    w2d = jnp.concatenate([w2, w2], axis=1).astype(BF16)
    flat = lambda p: jnp.broadcast_to(p[:, None, :], (half, NSA_KV_HEADS, HEAD_DIM)).reshape(1, half * KV_WIDTH)
    pe8 = jnp.concatenate([flat(pe[:half]), flat(pe[half:]), jnp.zeros((SUBLANES - 2, half * KV_WIDTH), F32)], axis=0)
    return w1b, w2d, pe8


def kernel(x_prompt, x_sample, cache_k_cmp, cache_v_cmp, cache_k_slc, cache_v_slc, state_k_win, state_v_win, state_s5_re, state_s5_im, page_table, norm_mix_w, w_in, s5_lam_re, s5_lam_im, s5_log_dt, s5_b_re, s5_b_im, s5_c_re, s5_c_im, s5_d, s5_w_glu, s5_w_up, q_norm_w, k_norm_w, cmp_pe_k, cmp_wk1, cmp_wk2, cmp_pe_v, cmp_wv1, cmp_wv2, nsa_w_up, w_out, norm_ffn_w, w_ffn1, w_ffn2):
    B, T, _ = x_prompt.shape
    S, TS, _ = x_sample.shape
    n_pages = page_table.shape[1]
    P = n_pages * PAGE_SIZE
    n_pool = cache_k_cmp.shape[1]
    assert norm_mix_w.shape[0] == 1 and B % SUBLANES == 0 and S % SUBLANES == 0
    assert TS < CMP_STRIDE and P % SLC_BLOCK == 0

    w = w_in[0]
    cols = lambda a, n: w[:, a:a + n]
    w_a = jnp.concatenate(
        [cols(0, 512), cols(512, 512), cols(1024, 128), cols(1280, 128), cols(1536, 128),
         cols(1152, 128), cols(1408, 128), cols(1664, 128), cols(1792, N_GATE),
         jnp.zeros((D_MODEL, W_A_COLS - 1792 - N_GATE), F32)], axis=1).astype(BF16)
    w_g = w[:, OFF_GMIX:].astype(BF16)
    nw_mix = norm_mix_w[0][None, :]
    qkw = jnp.concatenate([jnp.tile(q_norm_w[0], NSA_HEADS)]
                          + [jnp.tile(k_norm_w[0, i], NSA_KV_HEADS) for i in range(3)])[None, :]
    li = np.arange(LANES)
    seg = jnp.asarray((li[:, None] // HEAD_DIM) == (li[None, :] // HEAD_DIM), BF16)
    lre, lim = s5_lam_re[0].reshape(1, S5_LANES), s5_lam_im[0].reshape(1, S5_LANES)
    ldt = jnp.repeat(s5_log_dt[0], S5_STATE)[None, :]
    bre = _block_diag(jnp.swapaxes(s5_b_re[0], 1, 2))
    bim = _block_diag(jnp.swapaxes(s5_b_im[0], 1, 2))
    cre = _block_diag(jnp.swapaxes(s5_c_re[0], 1, 2)).astype(BF16)
    cim = _block_diag(jnp.swapaxes(s5_c_im[0], 1, 2)).astype(BF16)
    s5d = s5_d[0][None, :]
    wglu = s5_w_glu[0].astype(BF16)
    wup = s5_w_up[0].astype(BF16)
    cw = _cmp_weights(cmp_pe_k[0], cmp_wk1[0], cmp_wk2[0]) + _cmp_weights(cmp_pe_v[0], cmp_wv1[0], cmp_wv2[0])
    wnsa = nsa_w_up[0].astype(BF16)
    wout = w_out[0].astype(BF16)
    nw_ffn = norm_ffn_w[0][None, :]
    w1 = w_ffn1[0].astype(BF16)
    w2 = w_ffn2[0].astype(BF16)

    def trunk(x, glu, o, tt):
        return _ffn(_merge(x, glu, o, nw_mix, w_g, wup, wnsa, wout, tt), nw_ffn, w1, w2, tt)

    cos_p, sin_p = _rope_tables(jnp.arange(T, dtype=jnp.int32))
    u, q, kc, ks, kw, vc, vs, vw, g = _inproj(x_prompt, cos_p, sin_p, nw_mix, w_a, qkw, seg, 512)
    zeros_st = jnp.zeros((B, S5_LANES), F32)
    glu_tm, sre_p, sim_p = _s5(jnp.swapaxes(u, 0, 1), zeros_st, zeros_st, lre, lim, ldt, bre, bim, cre, cim, s5d, wglu, 64)
    o_p = _nsa_prompt(q, kc, vc, ks, vs, kw, vw, g, cw, 256)
    y_prompt = trunk(x_prompt, jnp.swapaxes(glu_tm, 0, 1), o_p, 512)

    n_s = S * TS
    cos_s, sin_s = _rope_tables(P + jnp.arange(TS, dtype=jnp.int32))
    cos_s, sin_s = jnp.tile(cos_s, (S, 1)), jnp.tile(sin_s, (S, 1))
    xs = x_sample.reshape(1, n_s, D_MODEL)
    us, qs, kcs, kss, kws, vcs, vss, vws, gs = _inproj(xs, cos_s, sin_s, nw_mix, w_a, qkw, seg, n_s)
    seq = lambda a: a.reshape(S, TS, a.shape[-1])
    u_tm = jnp.swapaxes(seq(us), 0, 1)
    glu_s_tm, sre_s, sim_s = _s5(u_tm, state_s5_re[0].reshape(S, S5_LANES), state_s5_im[0].reshape(S, S5_LANES),
                                 lre, lim, ldt, bre, bim, cre, cim, s5d, wglu, TS)
    pool = lambda c: jnp.transpose(c[0], (0, 2, 3, 1)).reshape(n_pool, KV_WIDTH, PAGE_SIZE)
    ocmp_s, sel_s = _nsa_sample_cmp(page_table, seq(qs), pool(cache_k_cmp), pool(cache_v_cmp), cw)
    win = lambda s: jnp.transpose(s[0], (0, 2, 3, 1)).reshape(S, KV_WIDTH, s.shape[2])
    o_s, kwin_s, vwin_s = _nsa_sample_slc(page_table, seq(qs), sel_s, ocmp_s, seq(gs), seq(kss), seq(vss), seq(kws),
                                          seq(vws), win(state_k_win), win(state_v_win),
                                          pool(cache_k_slc), pool(cache_v_slc))
    glu_s = jnp.swapaxes(glu_s_tm, 0, 1).reshape(1, n_s, S5_WIDTH)
    y_sample = trunk(xs, glu_s, o_s.reshape(1, n_s, NSA_WIDTH), n_s).reshape(S, TS, D_MODEL)

    heads_p = lambda a: a.reshape(1, B, a.shape[1], NSA_KV_HEADS, HEAD_DIM)
    heads_s = lambda a: a.reshape(1, S, -1, NSA_KV_HEADS, HEAD_DIM)
    keep = min(WINDOW, T)
    st_p = lambda a: a.reshape(1, B, S5_GROUPS, S5_STATE)
    st_s = lambda a: a.reshape(1, S, S5_GROUPS, S5_STATE)
    return (y_prompt, y_sample,
            heads_p(kc), heads_s(kcs), heads_p(vc), heads_s(vcs),
            heads_p(ks), heads_s(kss), heads_p(vs), heads_s(vss),
            heads_p(kw[:, T - keep:]), heads_s(kwin_s), heads_p(vw[:, T - keep:]), heads_s(vwin_s),
            st_p(sre_p), st_s(sre_s), st_p(sim_p), st_s(sim_s))
```

```python
import functools

import numpy as np
import jax
import jax.numpy as jnp
from jax import lax
from jax.experimental import pallas as pl
from jax.experimental.pallas import tpu as pltpu

F32 = jnp.float32
BF16 = jnp.bfloat16

D_MODEL = 1024
PAGE_SIZE = 128
S5_WIDTH = 512
S5_GROUPS = 32
S5_STATE = 64
S5_LANES = S5_GROUPS * S5_STATE
HEAD_DIM = 64
NSA_HEADS = 8
NSA_KV_HEADS = 2
Q_PER_KV = 4
NSA_WIDTH = 512
KV_WIDTH = 128
CMP_LEN = 32
CMP_STRIDE = 16
CMP_HIDDEN = 128
SLC_BLOCK = 64
SLC_TOPK = 16
WINDOW = 512
ROPE_THETA = 10000.0
EPS = 1e-6
NEG_INF = -1e30
FORCED = 1e9
SCALE = HEAD_DIM ** -0.5
N_GATE = 3 * NSA_HEADS
OFF_GMIX = 2 * 512 + 6 * KV_WIDTH + N_GATE
W_A_COLS = 1920

LANES = 128
SUBLANES = 8
VMEM_LIMIT = 56 * 1024 * 1024
TOKEN_TILE = 512
S5_CHUNK = 64
NSA_TILE = 256
BOUND_SLACK = 1.02
MAX_SHIFT_GAP = 60.0


def _params(sem):
    return pltpu.CompilerParams(dimension_semantics=sem, vmem_limit_bytes=VMEM_LIMIT)


def _const_spec(shape):
    nd = len(shape)
    return pl.BlockSpec(shape, lambda *_: (0,) * nd)


def _dot(a, b):
    return jnp.dot(a, b, preferred_element_type=F32)


def _dot_nt(a, b):
    return lax.dot_general(a, b, (((1,), (1,)), ((), ())), preferred_element_type=F32)


def _rms(x, w):
    r = lax.rsqrt(jnp.mean(x * x, axis=-1, keepdims=True) + EPS)
    return x * r * w


def _lane_iota(shape):
    return lax.broadcasted_iota(jnp.int32, shape, len(shape) - 1)


def _row_iota(shape):
    return lax.broadcasted_iota(jnp.int32, shape, len(shape) - 2)


def _split3(x):
    hi = x.astype(BF16)
    r1 = x - hi.astype(F32)
    mid = r1.astype(BF16)
    lo = (r1 - mid.astype(F32)).astype(BF16)
    return hi, mid, lo


def _inproj_body(x_ref, nw_ref, w_ref, qkw_ref, seg_ref, cos_ref, sin_ref,
                 u_ref, q_ref, kc_ref, ks_ref, kw_ref, vc_ref, vs_ref, vw_ref, g_ref):
    xn = _rms(x_ref[...], nw_ref[...]).astype(BF16)
    z = _dot(xn, w_ref[...])
    u_ref[...] = z[:, :512]
    cos = cos_ref[...]
    sin = sin_ref[...]
    seg = seg_ref[...]
    first_half = (_lane_iota((1, LANES)) % HEAD_DIM) < (HEAD_DIM // 2)

    def norm_rope(t, w):
        width = t.shape[1]
        ss = _dot((t * t).astype(BF16), seg[:width, :width])
        t = t * lax.rsqrt(ss * (1.0 / HEAD_DIM) + EPS) * w
        out = []
        for j in range(width // LANES):
            tj = t[:, j * LANES:(j + 1) * LANES]
            rot = jnp.where(first_half, pltpu.roll(tj, LANES - HEAD_DIM // 2, 1), pltpu.roll(tj, HEAD_DIM // 2, 1))
            out.append(tj * cos + rot * sin)
        return out

    qk = [norm_rope(z[:, 512 + j * 256:512 + (j + 1) * 256], qkw_ref[:, j * 256:(j + 1) * 256]) for j in range(3)]
    qk.append(norm_rope(z[:, 1280:1408], qkw_ref[:, 768:896]))
    tiles = [t for pair in qk for t in pair]
    for j in range(4):
        q_ref[:, j * LANES:(j + 1) * LANES] = tiles[j]
    for j, r in enumerate((kc_ref, ks_ref, kw_ref)):
        r[...] = tiles[4 + j]
    vc_ref[...] = z[:, 1408:1536]
    vs_ref[...] = z[:, 1536:1664]
    vw_ref[...] = z[:, 1664:1792]
    g_ref[...] = z[:, 1792:1920]


def _inproj(x, cos_t, sin_t, nw, w_a, qkw, seg, tt):
    B, T, _ = x.shape
    row = lambda w: pl.BlockSpec((None, tt, w), lambda b, i: (b, i, 0))
    tab = pl.BlockSpec((tt, LANES), lambda b, i: (i, 0))
    kv = jax.ShapeDtypeStruct((B, T, KV_WIDTH), F32)
    out_shape = [jax.ShapeDtypeStruct((B, T, 512), F32), jax.ShapeDtypeStruct((B, T, 512), F32)] + [kv] * 7
    return pl.pallas_call(
        _inproj_body,
        grid=(B, T // tt),
        in_specs=[row(D_MODEL), _const_spec((1, D_MODEL)), _const_spec((D_MODEL, W_A_COLS)),
                  _const_spec((1, 896)), _const_spec((2 * LANES, 2 * LANES)), tab, tab],
        out_specs=[row(512), row(512)] + [row(KV_WIDTH)] * 7,
        out_shape=out_shape,
        compiler_params=_params(("parallel", "parallel")),
        name="inproj",
    )(x, nw, w_a, qkw, seg, cos_t, sin_t)


def _s5_body(u_ref, sre0_ref, sim0_ref, lre_ref, lim_ref, ldt_ref, bre_ref, bim_ref, cre_ref, cim_ref,
             d_ref, wglu_ref, out_ref, sre_ref, sim_ref,
             bbar_ref, a_ref, st_ref, xr_ref, xi_ref, *, tc):
    c = pl.program_id(1)
    rows = tc * SUBLANES

    @pl.when(c == 0)
    def _():
        lr, li = lre_ref[...], lim_ref[...]
        dt = jnp.exp(ldt_ref[...])
        mag = jnp.exp(lr * dt)
        ab_re, ab_im = mag * jnp.cos(li * dt), mag * jnp.sin(li * dt)
        den = lr * lr + li * li
        f_re = ((ab_re - 1.0) * lr + ab_im * li) / den
        f_im = (ab_im * lr - (ab_re - 1.0) * li) / den
        bre, bim = bre_ref[...], bim_ref[...]
        bbar_ref[:, :S5_LANES] = (f_re * bre - f_im * bim).astype(BF16)
        bbar_ref[:, S5_LANES:] = (f_re * bim + f_im * bre).astype(BF16)
        a_ref[0] = jnp.broadcast_to(ab_re, (SUBLANES, S5_LANES))
        a_ref[1] = jnp.broadcast_to(ab_im, (SUBLANES, S5_LANES))
        st_ref[0] = sre0_ref[...]
        st_ref[1] = sim0_ref[...]

    u = u_ref[...].reshape(rows, S5_WIDTH)
    ub = u.astype(BF16)
    hw, hl = S5_WIDTH // 2, S5_LANES // 2
    for n in range(2):
        xr_ref[:, n * hl:(n + 1) * hl] = _dot(ub[:, n * hw:(n + 1) * hw], bbar_ref[n * hw:(n + 1) * hw, n * hl:(n + 1) * hl])
        xi_ref[:, n * hl:(n + 1) * hl] = _dot(ub[:, n * hw:(n + 1) * hw],
                                              bbar_ref[n * hw:(n + 1) * hw, S5_LANES + n * hl:S5_LANES + (n + 1) * hl])

    slab = 512
    for lc in range(S5_LANES // slab):
        sl = slice(lc * slab, (lc + 1) * slab)
        ar, ai = a_ref[0, :, sl], a_ref[1, :, sl]

        def step(t, carry, sl=sl, ar=ar, ai=ai):
            sr, si = carry
            r0 = pl.multiple_of(t * SUBLANES, SUBLANES)
            nr = ar * sr - ai * si + xr_ref[pl.ds(r0, SUBLANES), sl]
            ni = ar * si + ai * sr + xi_ref[pl.ds(r0, SUBLANES), sl]
            xr_ref[pl.ds(r0, SUBLANES), sl] = nr
            xi_ref[pl.ds(r0, SUBLANES), sl] = ni
            return nr, ni

        sr, si = lax.fori_loop(0, tc, step, (st_ref[0, :, sl], st_ref[1, :, sl]), unroll=min(tc, 8))
        st_ref[0, :, sl] = sr
        st_ref[1, :, sl] = si

    y = jnp.concatenate(
        [_dot(xr_ref[:, n * hl:(n + 1) * hl].astype(BF16), cre_ref[n * hl:(n + 1) * hl, n * hw:(n + 1) * hw])
         - _dot(xi_ref[:, n * hl:(n + 1) * hl].astype(BF16), cim_ref[n * hl:(n + 1) * hl, n * hw:(n + 1) * hw])
         for n in range(2)], axis=1) + d_ref[...] * u
    yg = jax.nn.gelu(y)
    glu = yg * jax.nn.sigmoid(_dot(yg.astype(BF16), wglu_ref[...]))
    out_ref[...] = glu.reshape(tc, SUBLANES, S5_WIDTH)

    @pl.when(c == pl.num_programs(1) - 1)
    def _():
        sre_ref[...] = st_ref[0]
        sim_ref[...] = st_ref[1]


def _s5(u_tm, sre0, sim0, lre, lim, ldt, bre, bim, cre, cim, d, wglu, tc):
    T, B, _ = u_tm.shape
    st_spec = pl.BlockSpec((SUBLANES, S5_LANES), lambda b, c: (b, 0))
    blk = pl.BlockSpec((tc, SUBLANES, S5_WIDTH), lambda b, c: (c, b, 0))
    st = jax.ShapeDtypeStruct((B, S5_LANES), F32)
    return pl.pallas_call(
        functools.partial(_s5_body, tc=tc),
        grid=(B // SUBLANES, T // tc),
        in_specs=[blk, st_spec, st_spec,
                  _const_spec((1, S5_LANES)), _const_spec((1, S5_LANES)), _const_spec((1, S5_LANES)),
                  _const_spec((S5_WIDTH, S5_LANES)), _const_spec((S5_WIDTH, S5_LANES)),
                  _const_spec((S5_LANES, S5_WIDTH)), _const_spec((S5_LANES, S5_WIDTH)),
                  _const_spec((1, S5_WIDTH)), _const_spec((S5_WIDTH, S5_WIDTH))],
        out_specs=[blk, st_spec, st_spec],
        out_shape=[jax.ShapeDtypeStruct((T, B, S5_WIDTH), F32), st, st],
        scratch_shapes=[pltpu.VMEM((S5_WIDTH, 2 * S5_LANES), BF16),
                        pltpu.VMEM((2, SUBLANES, S5_LANES), F32),
                        pltpu.VMEM((2, SUBLANES, S5_LANES), F32),
                        pltpu.VMEM((tc * SUBLANES, S5_LANES), F32),
                        pltpu.VMEM((tc * SUBLANES, S5_LANES), F32)],
        compiler_params=_params(("parallel", "arbitrary")),
        name="s5",
    )(u_tm, sre0, sim0, lre, lim, ldt, bre, bim, cre, cim, d, wglu)


def _head_lanes(x, half):
    return jnp.where((_lane_iota((1, LANES)) >> 6) == half, x, 0.0)


def _masked_softmax_parts(s, mask):
    s = jnp.where(mask, s, NEG_INF)
    m = jnp.max(s, axis=-1, keepdims=True)
    p = jnp.where(mask, jnp.exp(s - m), 0.0)
    l = jnp.maximum(jnp.sum(p, axis=-1, keepdims=True), 1e-30)
    return p, l


def _cmp_hidden(h, w1_ref, pe_ref):
    m = h.shape[0]
    pe = pe_ref[...].astype(BF16)
    bias = _dot(pe, w1_ref[:, :256])[0:1] + _dot(pe, w1_ref[:, 256:])[1:2]
    return jax.nn.gelu(h[:, :256] + pltpu.roll(h[:, 256:], m - 1, 0) + bias)


def _overlap(n_rows, n_cols, transpose):
    shape = (n_cols, n_rows) if transpose else (n_rows, n_cols)
    n = _row_iota(shape) if not transpose else _lane_iota(shape)
    j = _lane_iota(shape) if not transpose else _row_iota(shape)
    lo = jnp.maximum(n * CMP_STRIDE, j * SLC_BLOCK)
    hi = jnp.minimum(n * CMP_STRIDE + CMP_LEN, j * SLC_BLOCK + SLC_BLOCK)
    return (jnp.maximum(hi - lo, 0).astype(F32) * (1.0 / CMP_LEN)).astype(BF16)


VT_ROWS = HEAD_DIM + 16


def _ones_row_rows(n):
    return jnp.where(_row_iota((VT_ROWS - HEAD_DIM, n)) == 0, 1.0, 0.0)


def _shifted_update_t(m_ref, acc_ref, h, s_t, v_t):
    tk, tq = s_t.shape
    m_ref[h] = jnp.maximum(m_ref[h], jnp.max(s_t.reshape(tk // SUBLANES, SUBLANES, tq), axis=0))
    acc_ref[h] = acc_ref[h] + _dot(v_t, jnp.exp(s_t).astype(BF16))


def _online_update_t(m_ref, acc_ref, h, s_t, v_t):
    m_old = m_ref[h]
    m_new = jnp.maximum(m_old, jnp.max(s_t, axis=0, keepdims=True))
    p_t = jnp.exp(s_t - m_new[0:1])
    alpha = jnp.exp(m_old - m_new)
    acc_ref[h] = alpha[0:1] * acc_ref[h] + _dot(v_t, p_t.astype(BF16))
    m_ref[h] = m_new


def _softmax_result_t(m_ref, acc_ref, h):
    acc = acc_ref[h]
    out = acc[0:HEAD_DIM] / jnp.maximum(acc[HEAD_DIM:HEAD_DIM + 1], 1e-30)
    return jnp.where(jnp.max(m_ref[h], axis=0, keepdims=True) > 0.5 * NEG_INF, out, 0.0)


def _key_rows(k, kv):
    lane = _lane_iota((1, LANES))
    kk = jnp.where(lane < HEAD_DIM, jnp.where(kv == 0, k, pltpu.roll(k, HEAD_DIM, 1)), 0.0)
    norm2 = jnp.max(jnp.sum(kk * kk, axis=1, keepdims=True), axis=0, keepdims=True)
    return jnp.where(lane == HEAD_DIM, 1.0, kk).astype(BF16), jnp.sqrt(norm2)


def _nsa_prompt_body(q_ref, kc16_ref, vc16_ref, ks_ref, kw_ref, vs_ref, vw_ref, g_ref,
                     w1k_ref, w2k_ref, pek_ref, w1v_ref, w2v_ref, pev_ref,
                     o_ref,
                     ksa_ref, kwa_ref, vst_ref, vwt_ref, kcmp_ref, vcmpt_ref, knorm_ref,
                     qc_ref, qs_ref, qw_ref, selb_ref, ms_ref, accs_ref, mw_ref, accw_ref, *, T, TQ):
    kv = pl.program_id(1)
    m_cmp = T // CMP_STRIDE
    n_kt = T // TQ

    def prepare():
        ksa_ref[...], ns = _key_rows(ks_ref[...], kv)
        kwa_ref[...], nw = _key_rows(kw_ref[...], kv)
        knorm_ref[0] = jnp.broadcast_to(ns, (SUBLANES, LANES))
        knorm_ref[1] = jnp.broadcast_to(nw, (SUBLANES, LANES))
        ones = _ones_row_rows(TQ).astype(BF16)
        for kt in range(n_kt):
            for src, dst in ((vs_ref, vst_ref), (vw_ref, vwt_ref)):
                vt = src[kt * TQ:(kt + 1) * TQ, :].T
                dst[kt, 0:HEAD_DIM, :] = jnp.where(kv == 0, vt[0:HEAD_DIM], vt[HEAD_DIM:]).astype(BF16)
                dst[kt, HEAD_DIM:, :] = ones

        def compressed(x16_ref, w1_ref, pe_ref, w2_ref):
            hid = _cmp_hidden(_dot(x16_ref[...].astype(BF16), w1_ref[...]), w1_ref, pe_ref)
            hid = jnp.where(kv == 0, hid[:, :LANES], hid[:, LANES:])
            return _dot(hid.astype(BF16), w2_ref[...])

        kcmp_ref[...] = compressed(kc16_ref, w1k_ref, pek_ref, w2k_ref).astype(BF16)
        vcmp_t = compressed(vc16_ref, w1v_ref, pev_ref, w2v_ref).T
        vcmpt_ref[0:HEAD_DIM, :] = vcmp_t[0:HEAD_DIM].astype(BF16)
        vcmpt_ref[HEAD_DIM:, :] = jnp.zeros((HEAD_DIM, m_cmp), BF16)

    prepare()

    def query_tile(qi, carry):
        _nsa_prompt_tile(qi, kv, q_ref, g_ref, o_ref, ksa_ref, kwa_ref, vst_ref, vwt_ref, kcmp_ref, vcmpt_ref,
                         knorm_ref, qc_ref, qs_ref, qw_ref, selb_ref, ms_ref, accs_ref, mw_ref, accw_ref, T=T, TQ=TQ)
        return carry

    lax.fori_loop(0, n_kt, query_tile, 0)


def _nsa_prompt_tile(qi, kv, q_ref, g_ref, o_ref, ksa_ref, kwa_ref, vst_ref, vwt_ref, kcmp_ref, vcmpt_ref,
                     knorm_ref, qc_ref, qs_ref, qw_ref, selb_ref, ms_ref, accs_ref, mw_ref, accw_ref, *, T, TQ):
    m_cmp = T // CMP_STRIDE
    n_cmp = m_cmp - 1
    n_slc = T // SLC_BLOCK
    topk = min(SLC_TOPK, n_slc)
    n_kt = T // TQ
    lane = _lane_iota((1, LANES))
    r0 = pl.multiple_of(qi * TQ, TQ)
    for pp in range(2):
        qt = q_ref[pl.ds(r0, TQ), pp * LANES:(pp + 1) * LANES] * SCALE
        for e in range(2):
            qh = jnp.where(lane < HEAD_DIM, qt if e == 0 else pltpu.roll(qt, HEAD_DIM, 1), 0.0)
            qc_ref[2 * pp + e] = qh.astype(BF16)
            qnorm = jnp.sqrt(jnp.sum(qh * qh, axis=1, keepdims=True))
            for dst, which in ((qs_ref, 0), (qw_ref, 1)):
                bound = BOUND_SLACK * qnorm * knorm_ref[which][0:1, 0:1]
                dst[(2 * pp + e) * TQ:(2 * pp + e + 1) * TQ, :] = jnp.where(lane == HEAD_DIM, -bound, qh).astype(BF16)
    qpos = r0 + _lane_iota((1, TQ))

    n = _row_iota((m_cmp, TQ))
    mask_c = (n * CMP_STRIDE + CMP_LEN - 1 <= qpos) & (n < n_cmp)
    pg = jnp.zeros((m_cmp, TQ), F32)
    o_cmp = []
    for h in range(Q_PER_KV):
        s_t = jnp.where(mask_c, _dot_nt(kcmp_ref[...], qc_ref[h]), NEG_INF)
        p_t = jnp.where(mask_c, jnp.exp(s_t - jnp.max(s_t, axis=0, keepdims=True)), 0.0)
        inv_l = 1.0 / jnp.maximum(jnp.sum(p_t, axis=0, keepdims=True), 1e-30)
        pg = pg + p_t * inv_l
        o_cmp.append(_dot(vcmpt_ref[...], p_t.astype(BF16))[0:HEAD_DIM] * inv_l)

    ovt = _overlap(m_cmp, n_slc, transpose=True)
    imp = sum(_dot(ovt, part) for part in _split3(pg))
    j = _row_iota((n_slc, TQ))
    qblk = qpos // SLC_BLOCK
    valid = j <= qblk
    forced = (j == 0) | (j == qblk) | (j == qblk - 1)
    score = jnp.where(forced, FORCED, jnp.where(valid, imp, NEG_INF))
    rank = jnp.zeros((n_slc, TQ), jnp.int32)
    for i in range(n_slc):
        si = score[i:i + 1, :]
        beats = (si > score) | ((si == score) & (i < j))
        rank = rank + beats.astype(jnp.int32)
    def reset():
        for m_r, acc_r in ((ms_ref, accs_ref), (mw_ref, accw_ref)):
            m_r[...] = jnp.full(m_r.shape, NEG_INF, F32)
            acc_r[...] = jnp.zeros(acc_r.shape, F32)

    sel_bias = jnp.where((rank < topk) & valid, 0.0, NEG_INF)
    for jb in range(n_slc):
        selb_ref[jb] = jnp.broadcast_to(sel_bias[jb:jb + 1], (SUBLANES, TQ))
    not_after = _row_iota((TQ, 1)) <= _lane_iota((1, TQ))
    bias_diag = jnp.where(not_after, 0.0, NEG_INF)
    bias_far = jnp.where(not_after, NEG_INF, 0.0)
    blocks_per_tile = TQ // SLC_BLOCK
    win_tiles = WINDOW // TQ

    def both_branches(update):
        def apply(m_r, acc_r, s_all, bias, v_t):
            for h in range(Q_PER_KV):
                update(m_r, acc_r, h, s_all[:, h * TQ:(h + 1) * TQ] + bias, v_t)

        def slc_scores(kb):
            return _dot_nt(ksa_ref[pl.ds(pl.multiple_of(kb * TQ, TQ), TQ), :], qs_ref[...])

        def win_scores(kb):
            return _dot_nt(kwa_ref[pl.ds(pl.multiple_of(kb * TQ, TQ), TQ), :], qw_ref[...])

        def slc_bias(kb):
            rows = []
            for jb in range(blocks_per_tile):
                rows += [selb_ref[kb * blocks_per_tile + jb]] * (SLC_BLOCK // SUBLANES)
            return jnp.concatenate(rows, axis=0)

        def slc_pair(p, carry):
            ka = 2 * p
            kb = jnp.minimum(ka + 1, n_kt - 1)
            live = jnp.where(ka + 1 < qi, 0.0, NEG_INF)
            s_a, s_b = slc_scores(ka), slc_scores(kb)
            apply(ms_ref, accs_ref, s_a, slc_bias(ka), vst_ref[ka])
            apply(ms_ref, accs_ref, s_b, slc_bias(kb) + live, vst_ref[kb])
            return carry

        lax.fori_loop(0, (qi + 1) // 2, slc_pair, 0)

        win_kb = [jnp.maximum(qi - d, 0) for d in range(win_tiles + 1)]
        s_own = slc_scores(qi)
        s_win = [win_scores(kb) for kb in win_kb]
        apply(ms_ref, accs_ref, s_own, slc_bias(qi) + bias_diag, vst_ref[qi])
        for d, kb in enumerate(win_kb):
            exists = jnp.where(qi >= d, 0.0, NEG_INF)
            bias = exists + (bias_diag if d == 0 else bias_far if d == win_tiles else 0.0)
            apply(mw_ref, accw_ref, s_win[d], bias, vwt_ref[kb])

    reset()
    both_branches(_shifted_update_t)

    worst = jnp.full((1, TQ), 0.0, F32)
    for m_r in (ms_ref, mw_ref):
        for h in range(Q_PER_KV):
            top = jnp.max(m_r[h], axis=0, keepdims=True)
            worst = jnp.minimum(worst, jnp.where(top > 0.5 * NEG_INF, top, 0.0))
    redo = jnp.min(worst) < -MAX_SHIFT_GAP

    @pl.when(redo)
    def _():
        reset()
        both_branches(_online_update_t)

    g = g_ref[pl.ds(r0, TQ), :]
    gs_t = jax.nn.sigmoid(jnp.where(kv == 0, g, pltpu.roll(g, LANES - 3 * Q_PER_KV, 1))).T
    for pp in range(2):
        halves = []
        for e in range(2):
            h = 2 * pp + e
            halves.append(gs_t[3 * h:3 * h + 1] * o_cmp[h]
                          + gs_t[3 * h + 1:3 * h + 2] * _softmax_result_t(ms_ref, accs_ref, h)
                          + gs_t[3 * h + 2:3 * h + 3] * _softmax_result_t(mw_ref, accw_ref, h))
        o_ref[pl.ds(r0, TQ), pp * LANES:(pp + 1) * LANES] = jnp.concatenate(halves, axis=0).T


def _nsa_prompt(q, kc, vc, ks, vs, kw, vw, g, cw, tq):
    B, T, _ = q.shape
    m_cmp = T // CMP_STRIDE
    kc16 = kc.reshape(B, m_cmp, CMP_STRIDE * KV_WIDTH)
    vc16 = vc.reshape(B, m_cmp, CMP_STRIDE * KV_WIDTH)
    full = lambda w: pl.BlockSpec((None, T, w), lambda b, h: (b, 0, 0))
    x16 = pl.BlockSpec((None, m_cmp, CMP_STRIDE * KV_WIDTH), lambda b, h: (b, 0, 0))
    qo = pl.BlockSpec((None, T, 2 * LANES), lambda b, h: (b, 0, h))
    wspecs = [_const_spec(w.shape) for w in cw]
    k_rows = pltpu.VMEM((T, LANES), BF16)
    v_t = pltpu.VMEM((T // tq, VT_ROWS, tq), BF16)
    q_rows = pltpu.VMEM((Q_PER_KV, tq, LANES), BF16)
    q_all = pltpu.VMEM((Q_PER_KV * tq, LANES), BF16)
    run_max = pltpu.VMEM((Q_PER_KV, SUBLANES, tq), F32)
    run_acc = pltpu.VMEM((Q_PER_KV, VT_ROWS, tq), F32)
    return pl.pallas_call(
        functools.partial(_nsa_prompt_body, T=T, TQ=tq),
        grid=(B, NSA_KV_HEADS),
        in_specs=[qo, x16, x16, full(KV_WIDTH), full(KV_WIDTH), full(KV_WIDTH), full(KV_WIDTH), full(KV_WIDTH)] + wspecs,
        out_specs=qo,
        out_shape=jax.ShapeDtypeStruct((B, T, NSA_WIDTH), F32),
        scratch_shapes=[k_rows, k_rows, v_t, v_t,
                        pltpu.VMEM((m_cmp, LANES), BF16), pltpu.VMEM((LANES, m_cmp), BF16),
                        pltpu.VMEM((2, SUBLANES, LANES), F32),
                        q_rows, q_all, q_all, pltpu.VMEM((T // SLC_BLOCK, SUBLANES, tq), F32),
                        run_max, run_acc, run_max, run_acc],
        compiler_params=_params(("parallel", "arbitrary")),
        name="nsa_prompt",
    )(q, kc16, vc16, ks, kw, vs, vw, g, *cw)


def _stack_heads(q, kv, lane_half):
    rows = []
    for hl in range(Q_PER_KV):
        tile = 2 * kv + hl // 2
        x = _head_lanes(q[:, tile * LANES:(tile + 1) * LANES], hl % 2)
        rows.append(x if lane_half(hl) == hl % 2 else pltpu.roll(x, HEAD_DIM, 1))
    return jnp.concatenate(rows, axis=0)


class _PageFetch:
    def __init__(self, pt_ref, caches, bufs, sem_ref, n_pages):
        self.pt_ref, self.caches, self.bufs, self.sem_ref, self.n_pages = pt_ref, caches, bufs, sem_ref, n_pages
        self.b = pl.program_id(0)
        self.last = pl.num_programs(0) - 1
        self.slot = self.b % 2
        self.nxt = jnp.where(self.b == self.last, 0, self.b + 1)

    def _copy(self, which, seq, k, slot):
        return pltpu.make_async_copy(self.caches[which].at[self.pt_ref[seq, k]], self.bufs[which].at[slot, k],
                                     self.sem_ref.at[slot, which])

    def begin(self):
        @pl.when(self.b == 0)
        def _():
            for k in range(self.n_pages):
                for which in range(len(self.caches)):
                    self._copy(which, 0, k, 0).start()

        for k in range(self.n_pages):
            for which in range(len(self.caches)):
                self._copy(which, self.b, k, self.slot).wait()

    def prefetch(self, k):
        for which in range(len(self.caches)):
            self._copy(which, self.nxt, k, 1 - self.slot).start()

    def page(self, which, k):
        return self.bufs[which][self.slot, k]

    def end(self):
        @pl.when(self.b == self.last)
        def _():
            for k in range(self.n_pages):
                for which in range(len(self.caches)):
                    self._copy(which, self.nxt, k, 1 - self.slot).wait()


def _nsa_sample_cmp_body(pt_ref, q_ref, ck_ref, cv_ref, w1k_ref, w2k_ref, pek_ref, w1v_ref, w2v_ref, pev_ref,
                         ocmp_ref, sel_ref, xk_ref, xv_ref, kbuf_ref, vbuf_ref, sem_ref, *, P, TS):
    n_pages = P // PAGE_SIZE
    m_cmp = P // CMP_STRIDE
    n_cmp = (P + TS - CMP_LEN) // CMP_STRIDE + 1
    n_slc = -(-(P + TS) // SLC_BLOCK)
    n_slc_pad = -(-n_slc // LANES) * LANES
    topk = min(SLC_TOPK, n_slc)
    fetch = _PageFetch(pt_ref, (ck_ref, cv_ref), (kbuf_ref, vbuf_ref), sem_ref, n_pages)
    fetch.begin()

    gpp = PAGE_SIZE // CMP_STRIDE
    ri = _row_iota((PAGE_SIZE, PAGE_SIZE))
    li = _lane_iota((PAGE_SIZE, PAGE_SIZE))
    pick = jnp.where(ri == CMP_STRIDE * (li % gpp) + li // gpp, 1.0, 0.0).astype(BF16)
    zero = jnp.zeros_like(pick)
    pick2 = jnp.concatenate([jnp.concatenate([pick, zero], axis=1), jnp.concatenate([zero, pick], axis=1)], axis=0)
    for k in range(0, n_pages, 2):
        fetch.prefetch(k)
        fetch.prefetch(k + 1)
        for which, x_ref in ((0, xk_ref), (1, xv_ref)):
            two = jnp.concatenate([fetch.page(which, k), fetch.page(which, k + 1)], axis=1).astype(BF16)
            rows_by_r = _dot(two, pick2).T
            for r in range(2 * CMP_STRIDE):
                kk, rr = k + r // CMP_STRIDE, r % CMP_STRIDE
                x_ref[rr, kk * gpp:(kk + 1) * gpp, :] = rows_by_r[r * gpp:(r + 1) * gpp, :]
    hid = []
    for x_ref, w1_ref, pe_ref in ((xk_ref, w1k_ref, pek_ref), (xv_ref, w1v_ref, pev_ref)):
        x16 = jnp.concatenate([x_ref[r].astype(BF16) for r in range(CMP_STRIDE)], axis=1)
        hid.append(_cmp_hidden(_dot(x16, w1_ref[...]), w1_ref, pe_ref).astype(BF16))
    hidk, hidv = hid

    q = q_ref[...] * SCALE
    rows = Q_PER_KV * TS
    t = _row_iota((rows, 1)) % TS
    qpos = P + t
    n = _lane_iota((rows, m_cmp))
    mask = (n * CMP_STRIDE + CMP_LEN - 1 <= qpos) & (n < n_cmp)
    ov = _overlap(m_cmp, n_slc_pad, transpose=False)
    jl = _lane_iota((TS, n_slc_pad))
    qblk = (P + _row_iota((TS, 1))) // SLC_BLOCK
    valid = (jl <= qblk) & (jl < n_slc)
    forced = ((jl == 0) | (jl == qblk) | (jl == qblk - 1)) & (jl < n_slc)
    imps = []
    for kv in range(NSA_KV_HEADS):
        kcmp = _dot(hidk[:, kv * LANES:(kv + 1) * LANES], w2k_ref[...]).astype(BF16)
        vcmp = _dot(hidv[:, kv * LANES:(kv + 1) * LANES], w2v_ref[...]).astype(BF16)
        qs = _stack_heads(q, kv, lambda hl: hl % 2).astype(BF16)
        p, l = _masked_softmax_parts(_dot_nt(qs, kcmp), mask)
        ocmp_ref[kv] = _dot(p.astype(BF16), vcmp) / l
        pn = p / l
        pg = pn[0:TS] + pn[TS:2 * TS] + pn[2 * TS:3 * TS] + pn[3 * TS:4 * TS]
        imps.append(sum(_dot(part, ov) for part in _split3(pg)))
    tile2 = lambda a: jnp.concatenate([a] * NSA_KV_HEADS, axis=0)
    valid2, jl2 = tile2(valid), tile2(jl)
    score = jnp.where(tile2(forced), FORCED, jnp.where(valid2, jnp.concatenate(imps, axis=0), NEG_INF))
    rank = jnp.zeros(score.shape, jnp.int32)
    for i in range(n_slc):
        si = score[:, i:i + 1]
        beats = (si > score) | ((si == score) & (i < jl2))
        rank = rank + beats.astype(jnp.int32)
    sel = jnp.where((rank < topk) & valid2, 1.0, 0.0)
    for kv in range(NSA_KV_HEADS):
        for ch in range(n_slc_pad // LANES):
            sel_ref[kv, ch] = sel[kv * TS:(kv + 1) * TS, ch * LANES:(ch + 1) * LANES]
    fetch.end()


def _page_scratch(n_pages, n_caches):
    return ([pltpu.VMEM((2, n_pages, KV_WIDTH, PAGE_SIZE), F32)] * n_caches
            + [pltpu.SemaphoreType.DMA((2, n_caches))])


def _nsa_sample_cmp(page_table, q, ck_t, cv_t, cw):
    S, TS, _ = q.shape
    n_pages = page_table.shape[1]
    P = n_pages * PAGE_SIZE
    m_cmp = P // CMP_STRIDE
    n_slc_pad = -(-(-(-(P + TS) // SLC_BLOCK)) // LANES) * LANES
    hbm = pl.BlockSpec(memory_space=pl.ANY)
    wspecs = [pl.BlockSpec(w.shape, lambda b, pt, nd=w.ndim: (0,) * nd) for w in cw]
    grid_spec = pltpu.PrefetchScalarGridSpec(
        num_scalar_prefetch=1,
        grid=(S,),
        in_specs=[pl.BlockSpec((None, TS, NSA_WIDTH), lambda b, pt: (b, 0, 0)), hbm, hbm] + wspecs,
        out_specs=[pl.BlockSpec((None, NSA_KV_HEADS, Q_PER_KV * TS, LANES), lambda b, pt: (b, 0, 0, 0)),
                   pl.BlockSpec((None, NSA_KV_HEADS, n_slc_pad // LANES, TS, LANES), lambda b, pt: (b, 0, 0, 0, 0))],
        scratch_shapes=[pltpu.VMEM((CMP_STRIDE, m_cmp, KV_WIDTH), F32), pltpu.VMEM((CMP_STRIDE, m_cmp, KV_WIDTH), F32)]
        + _page_scratch(n_pages, 2),
    )
    return pl.pallas_call(
        functools.partial(_nsa_sample_cmp_body, P=P, TS=TS),
        grid_spec=grid_spec,
        out_shape=[jax.ShapeDtypeStruct((S, NSA_KV_HEADS, Q_PER_KV * TS, LANES), F32),
                   jax.ShapeDtypeStruct((S, NSA_KV_HEADS, n_slc_pad // LANES, TS, LANES), F32)],
        compiler_params=_params(("arbitrary",)),
        name="nsa_sample_cmp",
    )(page_table, q, ck_t, cv_t, *cw)


def _nsa_sample_slc_body(pt_ref, q_ref, sel_ref, ocmp_ref, g_ref, ksn_ref, vsn_ref, kwn_ref, vwn_ref,
                         kwin_ref, vwin_ref, ck_ref, cv_ref, o_ref, kwout_ref, vwout_ref,
                         kt_ref, vt_ref, m_ref, l_ref, acc_ref, kbuf_ref, vbuf_ref, sem_ref, *, P, TS):
    n_pages = P // PAGE_SIZE
    rows = NSA_KV_HEADS * Q_PER_KV * TS
    wb = kwin_ref.shape[1]
    lane = _lane_iota((1, LANES))
    t = _row_iota((rows, 1)) % TS

    fetch = _PageFetch(pt_ref, (ck_ref, cv_ref), (kbuf_ref, vbuf_ref), sem_ref, n_pages)
    fetch.begin()

    def sel_rows(ch):
        return jnp.concatenate([sel_ref[kv, ch] for kv in range(NSA_KV_HEADS) for _ in range(Q_PER_KV)], axis=0)

    def update(s, pv_of):
        m_old = m_ref[...]
        m_new = jnp.maximum(m_old, jnp.max(s, axis=-1, keepdims=True))
        p = jnp.exp(s - m_new[:, 0:1])
        alpha = jnp.exp(m_old - m_new)
        l_ref[...] = alpha * l_ref[...] + jnp.sum(p, axis=-1, keepdims=True)
        acc_ref[...] = alpha * acc_ref[...] + pv_of(p.astype(BF16))
        m_ref[...] = m_new

    def result():
        return jnp.where(m_ref[...] > 0.5 * NEG_INF, acc_ref[...] / jnp.maximum(l_ref[...], 1e-30), 0.0)

    def reset():
        m_ref[...] = jnp.full(m_ref.shape, NEG_INF, F32)
        l_ref[...] = jnp.zeros(l_ref.shape, F32)
        acc_ref[...] = jnp.zeros(acc_ref.shape, F32)

    q = q_ref[...] * SCALE
    q2 = jnp.concatenate([_stack_heads(q, kv, lambda hl, kv=kv: kv) for kv in range(NSA_KV_HEADS)],
                         axis=0).astype(BF16)
    reset()

    for k in range(n_pages):
        fetch.prefetch(k)
        kt_ref[:, k * PAGE_SIZE:(k + 1) * PAGE_SIZE] = fetch.page(0, k).astype(BF16)
        vt_ref[:, k * PAGE_SIZE:(k + 1) * PAGE_SIZE] = fetch.page(1, k).astype(BF16)
    chosen = []
    for i in range(P // LANES):
        b0 = i * (LANES // SLC_BLOCK)
        sel_c = sel_rows(b0 // LANES)
        chosen.append(jnp.where(lane < SLC_BLOCK,
                                jnp.broadcast_to(sel_c[:, b0 % LANES:b0 % LANES + 1], (rows, LANES)),
                                jnp.broadcast_to(sel_c[:, b0 % LANES + 1:b0 % LANES + 2], (rows, LANES))))
    kpos = _lane_iota((1, P))
    bias = jnp.where((jnp.concatenate(chosen, axis=1) > 0.5) & (kpos <= P + t), 0.0, NEG_INF)
    update(_dot(q2, kt_ref[...]) + bias, lambda p: _dot_nt(p, vt_ref[...]))

    pad_rows = lambda x: jnp.concatenate([x, jnp.zeros((LANES - TS, LANES), F32)], axis=0).astype(BF16)
    own = (lane >> 6) == (_row_iota((rows, 1)) // (Q_PER_KV * TS))
    jn = P // SLC_BLOCK
    sel_n = sel_rows(jn // LANES)[:, jn % LANES:jn % LANES + 1]
    bias_n = jnp.where((sel_n > 0.5) & (lane <= t) & (lane < TS), 0.0, NEG_INF)
    vn = pad_rows(vsn_ref[...])
    update(_dot_nt(q2, pad_rows(ksn_ref[...])) + bias_n, lambda p: _dot(p, vn))
    o_slc = result()
    reset()
    kw_t, vw_t = kwin_ref[...], vwin_ref[...]
    iw = _lane_iota((1, wb))
    rel = t + wb - iw
    bias_b = jnp.where((rel >= 0) & (rel < WINDOW), 0.0, NEG_INF)
    vw_tb = vw_t.astype(BF16)
    update(_dot(q2, kw_t.astype(BF16)) + bias_b, lambda p: _dot_nt(p, vw_tb))
    bias_w = jnp.where((lane <= t) & (lane < TS), 0.0, NEG_INF)
    vwn = pad_rows(vwn_ref[...])
    update(_dot_nt(q2, pad_rows(kwn_ref[...])) + bias_w, lambda p: _dot(p, vwn))
    o_win = result()
    gs = jax.nn.sigmoid(g_ref[...])
    gcol = lambda comp: jnp.concatenate(
        [gs[:, 3 * hq + comp:3 * hq + comp + 1] for hq in range(NSA_HEADS)], axis=0)
    ocmp = jnp.concatenate([ocmp_ref[kv] for kv in range(NSA_KV_HEADS)], axis=0)
    o = gcol(0) * ocmp + gcol(1) * o_slc + gcol(2) * o_win
    o = jnp.where(own, o, 0.0)
    for tile in range(NSA_HEADS // 2):
        a = o[(2 * tile) * TS:(2 * tile + 1) * TS]
        b = o[(2 * tile + 1) * TS:(2 * tile + 2) * TS]
        kv = (2 * tile) // Q_PER_KV
        a = a if kv == 0 else pltpu.roll(a, HEAD_DIM, 1)
        b = b if kv == 1 else pltpu.roll(b, HEAD_DIM, 1)
        o_ref[:, tile * LANES:(tile + 1) * LANES] = a + b
    kwout_ref[0:wb - TS, :] = kw_t.T[TS:wb, :]
    kwout_ref[wb - TS:wb, :] = kwn_ref[...]
    vwout_ref[0:wb - TS, :] = vw_t.T[TS:wb, :]
    vwout_ref[wb - TS:wb, :] = vwn_ref[...]
    fetch.end()


def _nsa_sample_slc(page_table, q, sel, ocmp, g, ksn, vsn, kwn, vwn, kwin_t, vwin_t, ck_t, cv_t):
    S, TS, _ = q.shape
    n_pages = page_table.shape[1]
    P = n_pages * PAGE_SIZE
    wb = kwin_t.shape[2]
    rows = NSA_KV_HEADS * Q_PER_KV * TS
    hbm = pl.BlockSpec(memory_space=pl.ANY)
    per_seq = lambda shape: pl.BlockSpec((None,) + shape, lambda b, pt, nd=len(shape): (b,) + (0,) * nd)
    grid_spec = pltpu.PrefetchScalarGridSpec(
        num_scalar_prefetch=1,
        grid=(S,),
        in_specs=[per_seq((TS, NSA_WIDTH)), per_seq(sel.shape[1:]), per_seq(ocmp.shape[1:]), per_seq((TS, LANES)),
                  per_seq((TS, KV_WIDTH)), per_seq((TS, KV_WIDTH)), per_seq((TS, KV_WIDTH)), per_seq((TS, KV_WIDTH)),
                  per_seq((KV_WIDTH, wb)), per_seq((KV_WIDTH, wb)), hbm, hbm],
        out_specs=[per_seq((TS, NSA_WIDTH)), per_seq((wb, KV_WIDTH)), per_seq((wb, KV_WIDTH))],
        scratch_shapes=[pltpu.VMEM((KV_WIDTH, P), BF16), pltpu.VMEM((KV_WIDTH, P), BF16)]
        + [pltpu.VMEM((rows, LANES), F32)] * 3 + _page_scratch(n_pages, 2),
    )
    return pl.pallas_call(
        functools.partial(_nsa_sample_slc_body, P=P, TS=TS),
        grid_spec=grid_spec,
        out_shape=[jax.ShapeDtypeStruct((S, TS, NSA_WIDTH), F32),
                   jax.ShapeDtypeStruct((S, wb, KV_WIDTH), F32), jax.ShapeDtypeStruct((S, wb, KV_WIDTH), F32)],
        compiler_params=_params(("arbitrary",)),
        name="nsa_sample_slc",
    )(page_table, q, sel, ocmp, g, ksn, vsn, kwn, vwn, kwin_t, vwin_t, ck_t, cv_t)


def _merge_body(x_ref, glu_ref, o_ref, nw_ref, wg_ref, wup_ref, wnsa_ref, wout_ref, out_ref):
    x = x_ref[...]
    gate = jax.nn.sigmoid(_dot(_rms(x, nw_ref[...]).astype(BF16), wg_ref[...]))
    s5_out = _dot(glu_ref[...].astype(BF16), wup_ref[...])
    nsa_out = _dot(o_ref[...].astype(BF16), wnsa_ref[...])
    merged = gate[:, :D_MODEL] * s5_out + gate[:, D_MODEL:] * nsa_out
    out_ref[...] = x + _dot(merged.astype(BF16), wout_ref[...])


def _merge(x, glu, o, nw, wg, wup, wnsa, wout, tt):
    B, T, _ = x.shape
    row = lambda w: pl.BlockSpec((None, tt, w), lambda b, i: (b, i, 0))
    return pl.pallas_call(
        _merge_body,
        grid=(B, T // tt),
        in_specs=[row(D_MODEL), row(S5_WIDTH), row(NSA_WIDTH), _const_spec(nw.shape), _const_spec(wg.shape),
                  _const_spec(wup.shape), _const_spec(wnsa.shape), _const_spec(wout.shape)],
        out_specs=row(D_MODEL),
        out_shape=jax.ShapeDtypeStruct((B, T, D_MODEL), F32),
        compiler_params=_params(("parallel", "parallel")),
        name="merge",
    )(x, glu, o, nw, wg, wup, wnsa, wout)


def _ffn_body(x_ref, nw_ref, w1_ref, w2_ref, out_ref):
    x = x_ref[...]
    a = jnp.maximum(_dot(_rms(x, nw_ref[...]).astype(BF16), w1_ref[...]), 0.0)
    out_ref[...] = x + _dot((a * a).astype(BF16), w2_ref[...])


def _ffn(x, nw, w1, w2, tt):
    B, T, _ = x.shape
    row = pl.BlockSpec((None, tt, D_MODEL), lambda b, i: (b, i, 0))
    return pl.pallas_call(
        _ffn_body,
        grid=(B, T // tt),
        in_specs=[row, _const_spec(nw.shape), _const_spec(w1.shape), _const_spec(w2.shape)],
        out_specs=row,
        out_shape=jax.ShapeDtypeStruct((B, T, D_MODEL), F32),
        compiler_params=_params(("parallel", "parallel")),
        name="ffn",
    )(x, nw, w1, w2)


def _rope_tables(pos):
    half = HEAD_DIM // 2
    inv = ROPE_THETA ** (-jnp.arange(half, dtype=F32) / half)
    ang = pos.astype(F32)[:, None] * inv[None, :]
    cos, sin = jnp.cos(ang), jnp.sin(ang)
    return (jnp.concatenate([cos, cos, cos, cos], axis=-1),
            jnp.concatenate([-sin, sin, -sin, sin], axis=-1))


def _block_diag(w):
    G, a, b = w.shape
    return jnp.einsum('gab,gk->gakb', w, jnp.eye(G, dtype=w.dtype)).reshape(G * a, G * b)


def _cmp_weights(pe, w1, w2):
    eye = jnp.eye(NSA_KV_HEADS, dtype=F32)
    half = CMP_LEN // 2
    big = lambda w: jnp.einsum('rdf,hk->rhdkf', w, eye).reshape(half * KV_WIDTH, NSA_KV_HEADS * CMP_HIDDEN)
    w1b = jnp.concatenate([big(w1[:half]), big(w1[half:])], axis=1).astype(BF16)
    w2d = jnp.concatenate([w2, w2], axis=1).astype(BF16)
    flat = lambda p: jnp.broadcast_to(p[:, None, :], (half, NSA_KV_HEADS, HEAD_DIM)).reshape(1, half * KV_WIDTH)
    pe8 = jnp.concatenate([flat(pe[:half]), flat(pe[half:]), jnp.zeros((SUBLANES - 2, half * KV_WIDTH), F32)], axis=0)
    return w1b, w2d, pe8


def kernel(x_prompt, x_sample, cache_k_cmp, cache_v_cmp, cache_k_slc, cache_v_slc, state_k_win, state_v_win, state_s5_re, state_s5_im, page_table, norm_mix_w, w_in, s5_lam_re, s5_lam_im, s5_log_dt, s5_b_re, s5_b_im, s5_c_re, s5_c_im, s5_d, s5_w_glu, s5_w_up, q_norm_w, k_norm_w, cmp_pe_k, cmp_wk1, cmp_wk2, cmp_pe_v, cmp_wv1, cmp_wv2, nsa_w_up, w_out, norm_ffn_w, w_ffn1, w_ffn2):
    B, T, _ = x_prompt.shape
    S, TS, _ = x_sample.shape
    n_pages = page_table.shape[1]
    P = n_pages * PAGE_SIZE
    n_pool = cache_k_cmp.shape[1]
    assert norm_mix_w.shape[0] == 1 and B % SUBLANES == 0 and S % SUBLANES == 0
    assert TS < CMP_STRIDE and P % SLC_BLOCK == 0

    w = w_in[0]
    cols = lambda a, n: w[:, a:a + n]
    w_a = jnp.concatenate(
        [cols(0, 512), cols(512, 512), cols(1024, 128), cols(1280, 128), cols(1536, 128),
         cols(1152, 128), cols(1408, 128), cols(1664, 128), cols(1792, N_GATE),
         jnp.zeros((D_MODEL, W_A_COLS - 1792 - N_GATE), F32)], axis=1).astype(BF16)
    w_g = w[:, OFF_GMIX:].astype(BF16)
    nw_mix = norm_mix_w[0][None, :]
    qkw = jnp.concatenate([jnp.tile(q_norm_w[0], NSA_HEADS)]
                          + [jnp.tile(k_norm_w[0, i], NSA_KV_HEADS) for i in range(3)])[None, :]
    li = np.arange(2 * LANES)
    seg = jnp.asarray((li[:, None] // HEAD_DIM) == (li[None, :] // HEAD_DIM), BF16)
    lre, lim = s5_lam_re[0].reshape(1, S5_LANES), s5_lam_im[0].reshape(1, S5_LANES)
    ldt = jnp.repeat(s5_log_dt[0], S5_STATE)[None, :]
    bre = _block_diag(jnp.swapaxes(s5_b_re[0], 1, 2))
    bim = _block_diag(jnp.swapaxes(s5_b_im[0], 1, 2))
    cre = _block_diag(jnp.swapaxes(s5_c_re[0], 1, 2)).astype(BF16)
    cim = _block_diag(jnp.swapaxes(s5_c_im[0], 1, 2)).astype(BF16)
    s5d = s5_d[0][None, :]
    wglu = s5_w_glu[0].astype(BF16)
    wup = s5_w_up[0].astype(BF16)
    cw = _cmp_weights(cmp_pe_k[0], cmp_wk1[0], cmp_wk2[0]) + _cmp_weights(cmp_pe_v[0], cmp_wv1[0], cmp_wv2[0])
    wnsa = nsa_w_up[0].astype(BF16)
    wout = w_out[0].astype(BF16)
    nw_ffn = norm_ffn_w[0][None, :]
    w1 = w_ffn1[0].astype(BF16)
    w2 = w_ffn2[0].astype(BF16)

    def trunk(x, glu, o, tt):
        return _ffn(_merge(x, glu, o, nw_mix, w_g, wup, wnsa, wout, tt), nw_ffn, w1, w2, tt)

    cos_p, sin_p = _rope_tables(jnp.arange(T, dtype=jnp.int32))
    u, q, kc, ks, kw, vc, vs, vw, g = _inproj(x_prompt, cos_p, sin_p, nw_mix, w_a, qkw, seg, TOKEN_TILE)
    zeros_st = jnp.zeros((B, S5_LANES), F32)
    glu_tm, sre_p, sim_p = _s5(jnp.swapaxes(u, 0, 1), zeros_st, zeros_st, lre, lim, ldt, bre, bim, cre, cim, s5d, wglu, S5_CHUNK)
    o_p = _nsa_prompt(q, kc, vc, ks, vs, kw, vw, g, cw, NSA_TILE)
    y_prompt = trunk(x_prompt, jnp.swapaxes(glu_tm, 0, 1), o_p, TOKEN_TILE)

    n_s = S * TS
    cos_s, sin_s = _rope_tables(P + jnp.arange(TS, dtype=jnp.int32))
    cos_s, sin_s = jnp.tile(cos_s, (S, 1)), jnp.tile(sin_s, (S, 1))
    xs = x_sample.reshape(1, n_s, D_MODEL)
    us, qs, kcs, kss, kws, vcs, vss, vws, gs = _inproj(xs, cos_s, sin_s, nw_mix, w_a, qkw, seg, n_s)
    seq = lambda a: a.reshape(S, TS, a.shape[-1])
    u_tm = jnp.swapaxes(seq(us), 0, 1)
    glu_s_tm, sre_s, sim_s = _s5(u_tm, state_s5_re[0].reshape(S, S5_LANES), state_s5_im[0].reshape(S, S5_LANES),
                                 lre, lim, ldt, bre, bim, cre, cim, s5d, wglu, TS)
    pool = lambda c: jnp.transpose(c[0], (0, 2, 3, 1)).reshape(n_pool, KV_WIDTH, PAGE_SIZE)
    ocmp_s, sel_s = _nsa_sample_cmp(page_table, seq(qs), pool(cache_k_cmp), pool(cache_v_cmp), cw)
    win = lambda s: jnp.transpose(s[0], (0, 2, 3, 1)).reshape(S, KV_WIDTH, s.shape[2])
    o_s, kwin_s, vwin_s = _nsa_sample_slc(page_table, seq(qs), sel_s, ocmp_s, seq(gs), seq(kss), seq(vss), seq(kws),
                                          seq(vws), win(state_k_win), win(state_v_win),
                                          pool(cache_k_slc), pool(cache_v_slc))
    glu_s = jnp.swapaxes(glu_s_tm, 0, 1).reshape(1, n_s, S5_WIDTH)
    y_sample = trunk(xs, glu_s, o_s.reshape(1, n_s, NSA_WIDTH), n_s).reshape(S, TS, D_MODEL)

    heads_p = lambda a: a.reshape(1, B, a.shape[1], NSA_KV_HEADS, HEAD_DIM)
    heads_s = lambda a: a.reshape(1, S, -1, NSA_KV_HEADS, HEAD_DIM)
    keep = min(WINDOW, T)
    st_p = lambda a: a.reshape(1, B, S5_GROUPS, S5_STATE)
    st_s = lambda a: a.reshape(1, S, S5_GROUPS, S5_STATE)
    return (y_prompt, y_sample,
            heads_p(kc), heads_s(kcs), heads_p(vc), heads_s(vcs),
            heads_p(ks), heads_s(kss), heads_p(vs), heads_s(vss),
            heads_p(kw[:, T - keep:]), heads_s(kwin_s), heads_p(vw[:, T - keep:]), heads_s(vwin_s),
            st_p(sre_p), st_s(sre_s), st_p(sim_p), st_s(sim_s))
```

```python
import functools

import numpy as np
import jax
import jax.numpy as jnp
from jax import lax
from jax.experimental import pallas as pl
from jax.experimental.pallas import tpu as pltpu

F32 = jnp.float32
BF16 = jnp.bfloat16

D_MODEL = 1024
PAGE_SIZE = 128
S5_WIDTH = 512
S5_GROUPS = 32
S5_STATE = 64
S5_LANES = S5_GROUPS * S5_STATE
HEAD_DIM = 64
NSA_HEADS = 8
NSA_KV_HEADS = 2
Q_PER_KV = 4
NSA_WIDTH = 512
KV_WIDTH = 128
CMP_LEN = 32
CMP_STRIDE = 16
CMP_HIDDEN = 128
SLC_BLOCK = 64
SLC_TOPK = 16
WINDOW = 512
ROPE_THETA = 10000.0
EPS = 1e-6
NEG_INF = -1e30
FORCED = 1e9
SCALE = HEAD_DIM ** -0.5
N_GATE = 3 * NSA_HEADS
OFF_GMIX = 2 * 512 + 6 * KV_WIDTH + N_GATE
W_A_COLS = 1920

LANES = 128
SUBLANES = 8
VMEM_LIMIT = 56 * 1024 * 1024
TOKEN_TILE = 512
S5_CHUNK = 64
NSA_TILE = 256
BOUND_SLACK = 1.02
MAX_SHIFT_GAP = 60.0


def _params(sem):
    return pltpu.CompilerParams(dimension_semantics=sem, vmem_limit_bytes=VMEM_LIMIT)


def _const_spec(shape):
    nd = len(shape)
    return pl.BlockSpec(shape, lambda *_: (0,) * nd)


def _dot(a, b):
    return jnp.dot(a, b, preferred_element_type=F32)


def _dot_nt(a, b):
    return lax.dot_general(a, b, (((1,), (1,)), ((), ())), preferred_element_type=F32)


def _rms(x, w):
    r = lax.rsqrt(jnp.mean(x * x, axis=-1, keepdims=True) + EPS)
    return x * r * w


def _lane_iota(shape):
    return lax.broadcasted_iota(jnp.int32, shape, len(shape) - 1)


def _row_iota(shape):
    return lax.broadcasted_iota(jnp.int32, shape, len(shape) - 2)


def _split3(x):
    hi = x.astype(BF16)
    r1 = x - hi.astype(F32)
    mid = r1.astype(BF16)
    lo = (r1 - mid.astype(F32)).astype(BF16)
    return hi, mid, lo


def _inproj_body(x_ref, nw_ref, w_ref, qkw_ref, seg_ref, cos_ref, sin_ref,
                 u_ref, q_ref, kc_ref, ks_ref, kw_ref, vc_ref, vs_ref, vw_ref, g_ref, kc16_ref, vc16_ref):
    xn = _rms(x_ref[...], nw_ref[...]).astype(BF16)
    z = _dot(xn, w_ref[...])
    u_ref[...] = z[:, :512]
    cos = cos_ref[...]
    sin = sin_ref[...]
    seg = seg_ref[...]
    first_half = (_lane_iota((1, LANES)) % HEAD_DIM) < (HEAD_DIM // 2)

    def norm_rope(t, w):
        width = t.shape[1]
        ss = _dot((t * t).astype(BF16), seg[:width, :width])
        t = t * lax.rsqrt(ss * (1.0 / HEAD_DIM) + EPS) * w
        out = []
        for j in range(width // LANES):
            tj = t[:, j * LANES:(j + 1) * LANES]
            rot = jnp.where(first_half, pltpu.roll(tj, LANES - HEAD_DIM // 2, 1), pltpu.roll(tj, HEAD_DIM // 2, 1))
            out.append(tj * cos + rot * sin)
        return out

    qk = [norm_rope(z[:, 512 + j * 256:512 + (j + 1) * 256], qkw_ref[:, j * 256:(j + 1) * 256]) for j in range(3)]
    qk.append(norm_rope(z[:, 1280:1408], qkw_ref[:, 768:896]))
    tiles = [t for pair in qk for t in pair]
    for j in range(4):
        q_ref[:, j * LANES:(j + 1) * LANES] = tiles[j]
    for j, r in enumerate((kc_ref, ks_ref, kw_ref)):
        r[...] = tiles[4 + j]
    vc_ref[...] = z[:, 1408:1536]
    vs_ref[...] = z[:, 1536:1664]
    vw_ref[...] = z[:, 1664:1792]
    g_ref[...] = z[:, 1792:1920]
    gpc = LANES // CMP_STRIDE
    ri = _row_iota((LANES, LANES))
    pick = jnp.where(_lane_iota((LANES, LANES)) == CMP_STRIDE * (ri % gpc) + ri // gpc, 1.0, 0.0).astype(BF16)
    for rows, out16 in ((tiles[4], kc16_ref), (z[:, 1408:1536], vc16_ref)):
        for j in range(rows.shape[0] // LANES):
            by_r = _dot(pick, rows[j * LANES:(j + 1) * LANES].astype(BF16))
            for r in range(CMP_STRIDE):
                out16[j * gpc:(j + 1) * gpc, r * LANES:(r + 1) * LANES] = by_r[r * gpc:(r + 1) * gpc]


def _inproj(x, cos_t, sin_t, nw, w_a, qkw, seg, tt):
    B, T, _ = x.shape
    row = lambda w: pl.BlockSpec((None, tt, w), lambda b, i: (b, i, 0))
    tab = pl.BlockSpec((tt, LANES), lambda b, i: (i, 0))
    kv = jax.ShapeDtypeStruct((B, T, KV_WIDTH), F32)
    g16 = jax.ShapeDtypeStruct((B, T // CMP_STRIDE, CMP_STRIDE * KV_WIDTH), F32)
    row16 = pl.BlockSpec((None, tt // CMP_STRIDE, CMP_STRIDE * KV_WIDTH), lambda b, i: (b, i, 0))
    out_shape = [jax.ShapeDtypeStruct((B, T, 512), F32), jax.ShapeDtypeStruct((B, T, 512), F32)] + [kv] * 7 + [g16] * 2
    return pl.pallas_call(
        _inproj_body,
        grid=(B, T // tt),
        in_specs=[row(D_MODEL), _const_spec((1, D_MODEL)), _const_spec((D_MODEL, W_A_COLS)),
                  _const_spec((1, 896)), _const_spec((2 * LANES, 2 * LANES)), tab, tab],
        out_specs=[row(512), row(512)] + [row(KV_WIDTH)] * 7 + [row16] * 2,
        out_shape=out_shape,
        compiler_params=_params(("parallel", "parallel")),
        name="inproj",
    )(x, nw, w_a, qkw, seg, cos_t, sin_t)


def _s5_body(u_ref, sre0_ref, sim0_ref, lre_ref, lim_ref, ldt_ref, bre_ref, bim_ref, cre_ref, cim_ref,
             d_ref, wglu_ref, out_ref, sre_ref, sim_ref,
             bbar_ref, a_ref, st_ref, xr_ref, xi_ref, *, tc):
    c = pl.program_id(1)
    rows = tc * SUBLANES

    @pl.when(c == 0)
    def _():
        lr, li = lre_ref[...], lim_ref[...]
        dt = jnp.exp(ldt_ref[...])
        mag = jnp.exp(lr * dt)
        ab_re, ab_im = mag * jnp.cos(li * dt), mag * jnp.sin(li * dt)
        den = lr * lr + li * li
        f_re = ((ab_re - 1.0) * lr + ab_im * li) / den
        f_im = (ab_im * lr - (ab_re - 1.0) * li) / den
        bre, bim = bre_ref[...], bim_ref[...]
        bbar_ref[:, :S5_LANES] = (f_re * bre - f_im * bim).astype(BF16)
        bbar_ref[:, S5_LANES:] = (f_re * bim + f_im * bre).astype(BF16)
        a_ref[0] = jnp.broadcast_to(ab_re, (SUBLANES, S5_LANES))
        a_ref[1] = jnp.broadcast_to(ab_im, (SUBLANES, S5_LANES))
        st_ref[0] = sre0_ref[...]
        st_ref[1] = sim0_ref[...]

    u = jnp.swapaxes(u_ref[...], 0, 1).reshape(rows, S5_WIDTH)
    ub = u.astype(BF16)
    hw, hl = S5_WIDTH // 2, S5_LANES // 2
    for n in range(2):
        xr_ref[:, n * hl:(n + 1) * hl] = _dot(ub[:, n * hw:(n + 1) * hw], bbar_ref[n * hw:(n + 1) * hw, n * hl:(n + 1) * hl])
        xi_ref[:, n * hl:(n + 1) * hl] = _dot(ub[:, n * hw:(n + 1) * hw],
                                              bbar_ref[n * hw:(n + 1) * hw, S5_LANES + n * hl:S5_LANES + (n + 1) * hl])

    slab = 512
    for lc in range(S5_LANES // slab):
        sl = slice(lc * slab, (lc + 1) * slab)
        ar, ai = a_ref[0, :, sl], a_ref[1, :, sl]

        def step(t, carry, sl=sl, ar=ar, ai=ai):
            sr, si = carry
            r0 = pl.multiple_of(t * SUBLANES, SUBLANES)
            nr = ar * sr - ai * si + xr_ref[pl.ds(r0, SUBLANES), sl]
            ni = ar * si + ai * sr + xi_ref[pl.ds(r0, SUBLANES), sl]
            xr_ref[pl.ds(r0, SUBLANES), sl] = nr
            xi_ref[pl.ds(r0, SUBLANES), sl] = ni
            return nr, ni

        sr, si = lax.fori_loop(0, tc, step, (st_ref[0, :, sl], st_ref[1, :, sl]), unroll=min(tc, 8))
        st_ref[0, :, sl] = sr
        st_ref[1, :, sl] = si

    y = jnp.concatenate(
        [_dot(xr_ref[:, n * hl:(n + 1) * hl].astype(BF16), cre_ref[n * hl:(n + 1) * hl, n * hw:(n + 1) * hw])
         - _dot(xi_ref[:, n * hl:(n + 1) * hl].astype(BF16), cim_ref[n * hl:(n + 1) * hl, n * hw:(n + 1) * hw])
         for n in range(2)], axis=1) + d_ref[...] * u
    yg = jax.nn.gelu(y)
    glu = yg * jax.nn.sigmoid(_dot(yg.astype(BF16), wglu_ref[...]))
    out_ref[...] = jnp.swapaxes(glu.reshape(tc, SUBLANES, S5_WIDTH), 0, 1)

    @pl.when(c == pl.num_programs(1) - 1)
    def _():
        sre_ref[...] = st_ref[0]
        sim_ref[...] = st_ref[1]


def _s5(u_tm, sre0, sim0, lre, lim, ldt, bre, bim, cre, cim, d, wglu, tc):
    B, T, _ = u_tm.shape
    st_spec = pl.BlockSpec((SUBLANES, S5_LANES), lambda b, c: (b, 0))
    blk = pl.BlockSpec((SUBLANES, tc, S5_WIDTH), lambda b, c: (b, c, 0))
    st = jax.ShapeDtypeStruct((B, S5_LANES), F32)
    return pl.pallas_call(
        functools.partial(_s5_body, tc=tc),
        grid=(B // SUBLANES, T // tc),
        in_specs=[blk, st_spec, st_spec,
                  _const_spec((1, S5_LANES)), _const_spec((1, S5_LANES)), _const_spec((1, S5_LANES)),
                  _const_spec((S5_WIDTH, S5_LANES)), _const_spec((S5_WIDTH, S5_LANES)),
                  _const_spec((S5_LANES, S5_WIDTH)), _const_spec((S5_LANES, S5_WIDTH)),
                  _const_spec((1, S5_WIDTH)), _const_spec((S5_WIDTH, S5_WIDTH))],
        out_specs=[blk, st_spec, st_spec],
        out_shape=[jax.ShapeDtypeStruct((B, T, S5_WIDTH), F32), st, st],
        scratch_shapes=[pltpu.VMEM((S5_WIDTH, 2 * S5_LANES), BF16),
                        pltpu.VMEM((2, SUBLANES, S5_LANES), F32),
                        pltpu.VMEM((2, SUBLANES, S5_LANES), F32),
                        pltpu.VMEM((tc * SUBLANES, S5_LANES), F32),
                        pltpu.VMEM((tc * SUBLANES, S5_LANES), F32)],
        compiler_params=_params(("parallel", "arbitrary")),
        name="s5",
    )(u_tm, sre0, sim0, lre, lim, ldt, bre, bim, cre, cim, d, wglu)


def _head_lanes(x, half):
    return jnp.where((_lane_iota((1, LANES)) >> 6) == half, x, 0.0)


def _masked_softmax_parts(s, mask):
    s = jnp.where(mask, s, NEG_INF)
    m = jnp.max(s, axis=-1, keepdims=True)
    p = jnp.where(mask, jnp.exp(s - m), 0.0)
    l = jnp.maximum(jnp.sum(p, axis=-1, keepdims=True), 1e-30)
    return p, l


def _cmp_hidden(h, w1_ref, pe_ref):
    m = h.shape[0]
    pe = pe_ref[...].astype(BF16)
    bias = _dot(pe, w1_ref[:, :256])[0:1] + _dot(pe, w1_ref[:, 256:])[1:2]
    return jax.nn.gelu(h[:, :256] + pltpu.roll(h[:, 256:], m - 1, 0) + bias)


def _overlap(n_rows, n_cols, transpose):
    shape = (n_cols, n_rows) if transpose else (n_rows, n_cols)
    n = _row_iota(shape) if not transpose else _lane_iota(shape)
    j = _lane_iota(shape) if not transpose else _row_iota(shape)
    lo = jnp.maximum(n * CMP_STRIDE, j * SLC_BLOCK)
    hi = jnp.minimum(n * CMP_STRIDE + CMP_LEN, j * SLC_BLOCK + SLC_BLOCK)
    return (jnp.maximum(hi - lo, 0).astype(F32) * (1.0 / CMP_LEN)).astype(BF16)


VT_ROWS = HEAD_DIM + 16


def _ones_row_rows(n):
    return jnp.where(_row_iota((VT_ROWS - HEAD_DIM, n)) == 0, 1.0, 0.0)


def _shifted_update_t(m_ref, acc_ref, h, s_t, v_t):
    tk, tq = s_t.shape
    m_ref[h] = jnp.maximum(m_ref[h], jnp.max(s_t.reshape(tk // SUBLANES, SUBLANES, tq), axis=0))
    acc_ref[h] = acc_ref[h] + _dot(v_t, jnp.exp(s_t).astype(BF16))


def _online_update_t(m_ref, acc_ref, h, s_t, v_t):
    m_old = m_ref[h]
    m_new = jnp.maximum(m_old, jnp.max(s_t, axis=0, keepdims=True))
    p_t = jnp.exp(s_t - m_new[0:1])
    alpha = jnp.exp(m_old - m_new)
    acc_ref[h] = alpha[0:1] * acc_ref[h] + _dot(v_t, p_t.astype(BF16))
    m_ref[h] = m_new


def _softmax_result_t(m_ref, acc_ref, h):
    acc = acc_ref[h]
    out = acc[0:HEAD_DIM] / jnp.maximum(acc[HEAD_DIM:HEAD_DIM + 1], 1e-30)
    return jnp.where(jnp.max(m_ref[h], axis=0, keepdims=True) > 0.5 * NEG_INF, out, 0.0)


def _key_rows(k, kv):
    lane = _lane_iota((1, LANES))
    kk = jnp.where(lane < HEAD_DIM, jnp.where(kv == 0, k, pltpu.roll(k, HEAD_DIM, 1)), 0.0)
    norm2 = jnp.max(jnp.sum(kk * kk, axis=1, keepdims=True), axis=0, keepdims=True)
    return jnp.where(lane == HEAD_DIM, 1.0, kk).astype(BF16), jnp.sqrt(norm2)


def _nsa_prompt_body(q_ref, kc16_ref, vc16_ref, ks_ref, kw_ref, vs_ref, vw_ref, g_ref,
                     w1k_ref, w2k_ref, pek_ref, w1v_ref, w2v_ref, pev_ref,
                     o_ref,
                     ksa_ref, kwa_ref, vst_ref, vwt_ref, kcmp_ref, vcmpt_ref, knorm_ref,
                     qc_ref, qs_ref, qw_ref, selb_ref, ms_ref, accs_ref, mw_ref, accw_ref, *, T, TQ):
    kv = pl.program_id(1)
    m_cmp = T // CMP_STRIDE
    n_kt = T // TQ

    def prepare():
        ksa_ref[...], ns = _key_rows(ks_ref[...], kv)
        kwa_ref[...], nw = _key_rows(kw_ref[...], kv)
        knorm_ref[0] = jnp.broadcast_to(ns, (SUBLANES, LANES))
        knorm_ref[1] = jnp.broadcast_to(nw, (SUBLANES, LANES))
        ones = _ones_row_rows(TQ).astype(BF16)
        for kt in range(n_kt):
            for src, dst in ((vs_ref, vst_ref), (vw_ref, vwt_ref)):
                vt = src[kt * TQ:(kt + 1) * TQ, :].T
                dst[kt, 0:HEAD_DIM, :] = jnp.where(kv == 0, vt[0:HEAD_DIM], vt[HEAD_DIM:]).astype(BF16)
                dst[kt, HEAD_DIM:, :] = ones

        def compressed(x16_ref, w1_ref, pe_ref, w2_ref):
            hid = _cmp_hidden(_dot(x16_ref[...].astype(BF16), w1_ref[...]), w1_ref, pe_ref)
            hid = jnp.where(kv == 0, hid[:, :LANES], hid[:, LANES:])
            return _dot(hid.astype(BF16), w2_ref[...])

        kcmp_ref[...] = compressed(kc16_ref, w1k_ref, pek_ref, w2k_ref).astype(BF16)
        vcmp_t = compressed(vc16_ref, w1v_ref, pev_ref, w2v_ref).T
        vcmpt_ref[0:HEAD_DIM, :] = vcmp_t[0:HEAD_DIM].astype(BF16)
        vcmpt_ref[HEAD_DIM:, :] = jnp.zeros((HEAD_DIM, m_cmp), BF16)

    prepare()

    def query_tile(qi, carry):
        _nsa_prompt_tile(qi, kv, q_ref, g_ref, o_ref, ksa_ref, kwa_ref, vst_ref, vwt_ref, kcmp_ref, vcmpt_ref,
                         knorm_ref, qc_ref, qs_ref, qw_ref, selb_ref, ms_ref, accs_ref, mw_ref, accw_ref, T=T, TQ=TQ)
        return carry

    lax.fori_loop(0, n_kt, query_tile, 0)


def _nsa_prompt_tile(qi, kv, q_ref, g_ref, o_ref, ksa_ref, kwa_ref, vst_ref, vwt_ref, kcmp_ref, vcmpt_ref,
                     knorm_ref, qc_ref, qs_ref, qw_ref, selb_ref, ms_ref, accs_ref, mw_ref, accw_ref, *, T, TQ):
    m_cmp = T // CMP_STRIDE
    n_cmp = m_cmp - 1
    n_slc = T // SLC_BLOCK
    topk = min(SLC_TOPK, n_slc)
    n_kt = T // TQ
    lane = _lane_iota((1, LANES))
    r0 = pl.multiple_of(qi * TQ, TQ)
    for pp in range(2):
        qt = q_ref[pl.ds(r0, TQ), pp * LANES:(pp + 1) * LANES] * SCALE
        for e in range(2):
            qh = jnp.where(lane < HEAD_DIM, qt if e == 0 else pltpu.roll(qt, HEAD_DIM, 1), 0.0)
            qc_ref[2 * pp + e] = qh.astype(BF16)
            qnorm = jnp.sqrt(jnp.sum(qh * qh, axis=1, keepdims=True))
            for dst, which in ((qs_ref, 0), (qw_ref, 1)):
                bound = BOUND_SLACK * qnorm * knorm_ref[which][0:1, 0:1]
                dst[(2 * pp + e) * TQ:(2 * pp + e + 1) * TQ, :] = jnp.where(lane == HEAD_DIM, -bound, qh).astype(BF16)
    qpos = r0 + _lane_iota((1, TQ))

    n = _row_iota((m_cmp, TQ))
    mask_c = (n * CMP_STRIDE + CMP_LEN - 1 <= qpos) & (n < n_cmp)
    pg = jnp.zeros((m_cmp, TQ), F32)
    o_cmp = []
    for h in range(Q_PER_KV):
        s_t = jnp.where(mask_c, _dot_nt(kcmp_ref[...], qc_ref[h]), NEG_INF)
        p_t = jnp.where(mask_c, jnp.exp(s_t - jnp.max(s_t, axis=0, keepdims=True)), 0.0)
        inv_l = 1.0 / jnp.maximum(jnp.sum(p_t, axis=0, keepdims=True), 1e-30)
        pg = pg + p_t * inv_l
        o_cmp.append(_dot(vcmpt_ref[...], p_t.astype(BF16))[0:HEAD_DIM] * inv_l)

    ovt = _overlap(m_cmp, n_slc, transpose=True)
    imp = sum(_dot(ovt, part) for part in _split3(pg))
    j = _row_iota((n_slc, TQ))
    qblk = qpos // SLC_BLOCK
    valid = j <= qblk
    forced = (j == 0) | (j == qblk) | (j == qblk - 1)
    score = jnp.where(forced, FORCED, jnp.where(valid, imp, NEG_INF))
    rank = jnp.zeros((n_slc, TQ), jnp.int32)
    for i in range(n_slc):
        si = score[i:i + 1, :]
        beats = (si > score) | ((si == score) & (i < j))
        rank = rank + beats.astype(jnp.int32)
    def reset():
        for m_r, acc_r in ((ms_ref, accs_ref), (mw_ref, accw_ref)):
            m_r[...] = jnp.full(m_r.shape, NEG_INF, F32)
            acc_r[...] = jnp.zeros(acc_r.shape, F32)

    sel_bias = jnp.where((rank < topk) & valid, 0.0, NEG_INF)
    for jb in range(n_slc):
        selb_ref[jb] = jnp.broadcast_to(sel_bias[jb:jb + 1], (SUBLANES, TQ))
    not_after = _row_iota((TQ, 1)) <= _lane_iota((1, TQ))
    bias_diag = jnp.where(not_after, 0.0, NEG_INF)
    bias_far = jnp.where(not_after, NEG_INF, 0.0)
    blocks_per_tile = TQ // SLC_BLOCK
    win_tiles = WINDOW // TQ

    def both_branches(update):
        def apply(m_r, acc_r, s_all, bias, v_t):
            for h in range(Q_PER_KV):
                update(m_r, acc_r, h, s_all[:, h * TQ:(h + 1) * TQ] + bias, v_t)

        def slc_scores(kb):
            return _dot_nt(ksa_ref[pl.ds(pl.multiple_of(kb * TQ, TQ), TQ), :], qs_ref[...])

        def win_scores(kb):
            return _dot_nt(kwa_ref[pl.ds(pl.multiple_of(kb * TQ, TQ), TQ), :], qw_ref[...])

        def slc_bias(kb):
            rows = []
            for jb in range(blocks_per_tile):
                rows += [selb_ref[kb * blocks_per_tile + jb]] * (SLC_BLOCK // SUBLANES)
            return jnp.concatenate(rows, axis=0)

        def slc_pair(p, carry):
            ka = 2 * p
            kb = jnp.minimum(ka + 1, n_kt - 1)
            live = jnp.where(ka + 1 < qi, 0.0, NEG_INF)
            s_a, s_b = slc_scores(ka), slc_scores(kb)
            apply(ms_ref, accs_ref, s_a, slc_bias(ka), vst_ref[ka])
            apply(ms_ref, accs_ref, s_b, slc_bias(kb) + live, vst_ref[kb])
            return carry

        lax.fori_loop(0, (qi + 1) // 2, slc_pair, 0)

        win_kb = [jnp.maximum(qi - d, 0) for d in range(win_tiles + 1)]
        s_own = slc_scores(qi)
        s_win = [win_scores(kb) for kb in win_kb]
        apply(ms_ref, accs_ref, s_own, slc_bias(qi) + bias_diag, vst_ref[qi])
        for d, kb in enumerate(win_kb):
            exists = jnp.where(qi >= d, 0.0, NEG_INF)
            bias = exists + (bias_diag if d == 0 else bias_far if d == win_tiles else 0.0)
            apply(mw_ref, accw_ref, s_win[d], bias, vwt_ref[kb])

    reset()
    both_branches(_shifted_update_t)

    worst = jnp.full((1, TQ), 0.0, F32)
    for m_r in (ms_ref, mw_ref):
        for h in range(Q_PER_KV):
            top = jnp.max(m_r[h], axis=0, keepdims=True)
            worst = jnp.minimum(worst, jnp.where(top > 0.5 * NEG_INF, top, 0.0))
    redo = jnp.min(worst) < -MAX_SHIFT_GAP

    @pl.when(redo)
    def _():
        reset()
        both_branches(_online_update_t)

    g = g_ref[pl.ds(r0, TQ), :]
    gs_t = jax.nn.sigmoid(jnp.where(kv == 0, g, pltpu.roll(g, LANES - 3 * Q_PER_KV, 1))).T
    for pp in range(2):
        halves = []
        for e in range(2):
            h = 2 * pp + e
            halves.append(gs_t[3 * h:3 * h + 1] * o_cmp[h]
                          + gs_t[3 * h + 1:3 * h + 2] * _softmax_result_t(ms_ref, accs_ref, h)
                          + gs_t[3 * h + 2:3 * h + 3] * _softmax_result_t(mw_ref, accw_ref, h))
        o_ref[pl.ds(r0, TQ), pp * LANES:(pp + 1) * LANES] = jnp.concatenate(halves, axis=0).T


def _nsa_prompt(q, kc16, vc16, ks, vs, kw, vw, g, cw, tq):
    B, T, _ = q.shape
    m_cmp = T // CMP_STRIDE
    full = lambda w: pl.BlockSpec((None, T, w), lambda b, h: (b, 0, 0))
    x16 = pl.BlockSpec((None, m_cmp, CMP_STRIDE * KV_WIDTH), lambda b, h: (b, 0, 0))
    qo = pl.BlockSpec((None, T, 2 * LANES), lambda b, h: (b, 0, h))
    wspecs = [_const_spec(w.shape) for w in cw]
    k_rows = pltpu.VMEM((T, LANES), BF16)
    v_t = pltpu.VMEM((T // tq, VT_ROWS, tq), BF16)
    q_rows = pltpu.VMEM((Q_PER_KV, tq, LANES), BF16)
    q_all = pltpu.VMEM((Q_PER_KV * tq, LANES), BF16)
    run_max = pltpu.VMEM((Q_PER_KV, SUBLANES, tq), F32)
    run_acc = pltpu.VMEM((Q_PER_KV, VT_ROWS, tq), F32)
    return pl.pallas_call(
        functools.partial(_nsa_prompt_body, T=T, TQ=tq),
        grid=(B, NSA_KV_HEADS),
        in_specs=[qo, x16, x16, full(KV_WIDTH), full(KV_WIDTH), full(KV_WIDTH), full(KV_WIDTH), full(KV_WIDTH)] + wspecs,
        out_specs=qo,
        out_shape=jax.ShapeDtypeStruct((B, T, NSA_WIDTH), F32),
        scratch_shapes=[k_rows, k_rows, v_t, v_t,
                        pltpu.VMEM((m_cmp, LANES), BF16), pltpu.VMEM((LANES, m_cmp), BF16),
                        pltpu.VMEM((2, SUBLANES, LANES), F32),
                        q_rows, q_all, q_all, pltpu.VMEM((T // SLC_BLOCK, SUBLANES, tq), F32),
                        run_max, run_acc, run_max, run_acc],
        compiler_params=_params(("parallel", "arbitrary")),
        name="nsa_prompt",
    )(q, kc16, vc16, ks, kw, vs, vw, g, *cw)


def _stack_heads(q, kv, lane_half):
    rows = []
    for hl in range(Q_PER_KV):
        tile = 2 * kv + hl // 2
        x = _head_lanes(q[:, tile * LANES:(tile + 1) * LANES], hl % 2)
        rows.append(x if lane_half(hl) == hl % 2 else pltpu.roll(x, HEAD_DIM, 1))
    return jnp.concatenate(rows, axis=0)


class _PageFetch:
    def __init__(self, pt_ref, caches, bufs, sem_ref, n_pages):
        self.pt_ref, self.caches, self.bufs, self.sem_ref, self.n_pages = pt_ref, caches, bufs, sem_ref, n_pages
        self.b = pl.program_id(0)
        self.last = pl.num_programs(0) - 1
        self.slot = self.b % 2
        self.nxt = jnp.where(self.b == self.last, 0, self.b + 1)

    def _copy(self, which, seq, k, slot):
        return pltpu.make_async_copy(self.caches[which].at[self.pt_ref[seq, k]], self.bufs[which].at[slot, k],
                                     self.sem_ref.at[slot, which])

    def begin(self):
        @pl.when(self.b == 0)
        def _():
            for k in range(self.n_pages):
                for which in range(len(self.caches)):
                    self._copy(which, 0, k, 0).start()

        for k in range(self.n_pages):
            for which in range(len(self.caches)):
                self._copy(which, self.b, k, self.slot).wait()

    def prefetch(self, k):
        for which in range(len(self.caches)):
            self._copy(which, self.nxt, k, 1 - self.slot).start()

    def page(self, which, k):
        return self.bufs[which][self.slot, k]

    def end(self):
        @pl.when(self.b == self.last)
        def _():
            for k in range(self.n_pages):
                for which in range(len(self.caches)):
                    self._copy(which, self.nxt, k, 1 - self.slot).wait()


def _nsa_sample_cmp_body(pt_ref, q_ref, ck_ref, cv_ref, w1k_ref, w2k_ref, pek_ref, w1v_ref, w2v_ref, pev_ref,
                         ocmp_ref, sel_ref, xk_ref, xv_ref, kbuf_ref, vbuf_ref, sem_ref, *, P, TS):
    n_pages = P // PAGE_SIZE
    m_cmp = P // CMP_STRIDE
    n_cmp = (P + TS - CMP_LEN) // CMP_STRIDE + 1
    n_slc = -(-(P + TS) // SLC_BLOCK)
    n_slc_pad = -(-n_slc // LANES) * LANES
    topk = min(SLC_TOPK, n_slc)
    fetch = _PageFetch(pt_ref, (ck_ref, cv_ref), (kbuf_ref, vbuf_ref), sem_ref, n_pages)
    fetch.begin()

    gpp = PAGE_SIZE // CMP_STRIDE
    ri = _row_iota((PAGE_SIZE, PAGE_SIZE))
    li = _lane_iota((PAGE_SIZE, PAGE_SIZE))
    pick = jnp.where(ri == CMP_STRIDE * (li % gpp) + li // gpp, 1.0, 0.0).astype(BF16)
    zero = jnp.zeros_like(pick)
    pick2 = jnp.concatenate([jnp.concatenate([pick, zero], axis=1), jnp.concatenate([zero, pick], axis=1)], axis=0)
    for k in range(0, n_pages, 2):
        fetch.prefetch(k)
        fetch.prefetch(k + 1)
        for which, x_ref in ((0, xk_ref), (1, xv_ref)):
            two = jnp.concatenate([fetch.page(which, k), fetch.page(which, k + 1)], axis=1).astype(BF16)
            rows_by_r = _dot(two, pick2).T
            for r in range(2 * CMP_STRIDE):
                kk, rr = k + r // CMP_STRIDE, r % CMP_STRIDE
                x_ref[rr, kk * gpp:(kk + 1) * gpp, :] = rows_by_r[r * gpp:(r + 1) * gpp, :]
    hid = []
    for x_ref, w1_ref, pe_ref in ((xk_ref, w1k_ref, pek_ref), (xv_ref, w1v_ref, pev_ref)):
        x16 = jnp.concatenate([x_ref[r].astype(BF16) for r in range(CMP_STRIDE)], axis=1)
        hid.append(_cmp_hidden(_dot(x16, w1_ref[...]), w1_ref, pe_ref).astype(BF16))
    hidk, hidv = hid

    q = q_ref[...] * SCALE
    rows = Q_PER_KV * TS
    t = _row_iota((rows, 1)) % TS
    qpos = P + t
    n = _lane_iota((rows, m_cmp))
    mask = (n * CMP_STRIDE + CMP_LEN - 1 <= qpos) & (n < n_cmp)
    ov = _overlap(m_cmp, n_slc_pad, transpose=False)
    jl = _lane_iota((TS, n_slc_pad))
    qblk = (P + _row_iota((TS, 1))) // SLC_BLOCK
    valid = (jl <= qblk) & (jl < n_slc)
    forced = ((jl == 0) | (jl == qblk) | (jl == qblk - 1)) & (jl < n_slc)
    imps = []
    for kv in range(NSA_KV_HEADS):
        kcmp = _dot(hidk[:, kv * LANES:(kv + 1) * LANES], w2k_ref[...]).astype(BF16)
        vcmp = _dot(hidv[:, kv * LANES:(kv + 1) * LANES], w2v_ref[...]).astype(BF16)
        qs = _stack_heads(q, kv, lambda hl: hl % 2).astype(BF16)
        p, l = _masked_softmax_parts(_dot_nt(qs, kcmp), mask)
        ocmp_ref[kv] = _dot(p.astype(BF16), vcmp) / l
        pn = p / l
        pg = pn[0:TS] + pn[TS:2 * TS] + pn[2 * TS:3 * TS] + pn[3 * TS:4 * TS]
        imps.append(sum(_dot(part, ov) for part in _split3(pg)))
    tile2 = lambda a: jnp.concatenate([a] * NSA_KV_HEADS, axis=0)
    valid2, jl2 = tile2(valid), tile2(jl)
    score = jnp.where(tile2(forced), FORCED, jnp.where(valid2, jnp.concatenate(imps, axis=0), NEG_INF))
    rank = jnp.zeros(score.shape, jnp.int32)
    for i in range(n_slc):
        si = score[:, i:i + 1]
        beats = (si > score) | ((si == score) & (i < jl2))
        rank = rank + beats.astype(jnp.int32)
    sel = jnp.where((rank < topk) & valid2, 1.0, 0.0)
    for kv in range(NSA_KV_HEADS):
        for ch in range(n_slc_pad // LANES):
            sel_ref[kv, ch] = sel[kv * TS:(kv + 1) * TS, ch * LANES:(ch + 1) * LANES]
    fetch.end()


def _page_scratch(n_pages, n_caches):
    return ([pltpu.VMEM((2, n_pages, KV_WIDTH, PAGE_SIZE), F32)] * n_caches
            + [pltpu.SemaphoreType.DMA((2, n_caches))])


def _nsa_sample_cmp(page_table, q, ck_t, cv_t, cw):
    S, TS, _ = q.shape
    n_pages = page_table.shape[1]
    P = n_pages * PAGE_SIZE
    m_cmp = P // CMP_STRIDE
    n_slc_pad = -(-(-(-(P + TS) // SLC_BLOCK)) // LANES) * LANES
    hbm = pl.BlockSpec(memory_space=pl.ANY)
    wspecs = [pl.BlockSpec(w.shape, lambda b, pt, nd=w.ndim: (0,) * nd) for w in cw]
    grid_spec = pltpu.PrefetchScalarGridSpec(
        num_scalar_prefetch=1,
        grid=(S,),
        in_specs=[pl.BlockSpec((None, TS, NSA_WIDTH), lambda b, pt: (b, 0, 0)), hbm, hbm] + wspecs,
        out_specs=[pl.BlockSpec((None, NSA_KV_HEADS, Q_PER_KV * TS, LANES), lambda b, pt: (b, 0, 0, 0)),
                   pl.BlockSpec((None, NSA_KV_HEADS, n_slc_pad // LANES, TS, LANES), lambda b, pt: (b, 0, 0, 0, 0))],
        scratch_shapes=[pltpu.VMEM((CMP_STRIDE, m_cmp, KV_WIDTH), F32), pltpu.VMEM((CMP_STRIDE, m_cmp, KV_WIDTH), F32)]
        + _page_scratch(n_pages, 2),
    )
    return pl.pallas_call(
        functools.partial(_nsa_sample_cmp_body, P=P, TS=TS),
        grid_spec=grid_spec,
        out_shape=[jax.ShapeDtypeStruct((S, NSA_KV_HEADS, Q_PER_KV * TS, LANES), F32),
                   jax.ShapeDtypeStruct((S, NSA_KV_HEADS, n_slc_pad // LANES, TS, LANES), F32)],
        compiler_params=_params(("arbitrary",)),
        name="nsa_sample_cmp",
    )(page_table, q, ck_t, cv_t, *cw)


def _nsa_sample_slc_body(pt_ref, q_ref, sel_ref, ocmp_ref, g_ref, ksn_ref, vsn_ref, kwn_ref, vwn_ref,
                         kwin_ref, vwin_ref, ck_ref, cv_ref, o_ref, kwout_ref, vwout_ref,
                         kt_ref, vt_ref, m_ref, l_ref, acc_ref, kbuf_ref, vbuf_ref, sem_ref, *, P, TS):
    n_pages = P // PAGE_SIZE
    rows = NSA_KV_HEADS * Q_PER_KV * TS
    wb = kwin_ref.shape[1]
    lane = _lane_iota((1, LANES))
    t = _row_iota((rows, 1)) % TS

    fetch = _PageFetch(pt_ref, (ck_ref, cv_ref), (kbuf_ref, vbuf_ref), sem_ref, n_pages)
    fetch.begin()

    def sel_rows(ch):
        return jnp.concatenate([sel_ref[kv, ch] for kv in range(NSA_KV_HEADS) for _ in range(Q_PER_KV)], axis=0)

    def update(s, pv_of):
        m_old = m_ref[...]
        m_new = jnp.maximum(m_old, jnp.max(s, axis=-1, keepdims=True))
        p = jnp.exp(s - m_new[:, 0:1])
        alpha = jnp.exp(m_old - m_new)
        l_ref[...] = alpha * l_ref[...] + jnp.sum(p, axis=-1, keepdims=True)
        acc_ref[...] = alpha * acc_ref[...] + pv_of(p.astype(BF16))
        m_ref[...] = m_new

    def result():
        return jnp.where(m_ref[...] > 0.5 * NEG_INF, acc_ref[...] / jnp.maximum(l_ref[...], 1e-30), 0.0)

    def reset():
        m_ref[...] = jnp.full(m_ref.shape, NEG_INF, F32)
        l_ref[...] = jnp.zeros(l_ref.shape, F32)
        acc_ref[...] = jnp.zeros(acc_ref.shape, F32)

    q = q_ref[...] * SCALE
    q2 = jnp.concatenate([_stack_heads(q, kv, lambda hl, kv=kv: kv) for kv in range(NSA_KV_HEADS)],
                         axis=0).astype(BF16)
    reset()

    for k in range(n_pages):
        fetch.prefetch(k)
        kt_ref[:, k * PAGE_SIZE:(k + 1) * PAGE_SIZE] = fetch.page(0, k).astype(BF16)
        vt_ref[:, k * PAGE_SIZE:(k + 1) * PAGE_SIZE] = fetch.page(1, k).astype(BF16)
    chosen = []
    for i in range(P // LANES):
        b0 = i * (LANES // SLC_BLOCK)
        sel_c = sel_rows(b0 // LANES)
        chosen.append(jnp.where(lane < SLC_BLOCK,
                                jnp.broadcast_to(sel_c[:, b0 % LANES:b0 % LANES + 1], (rows, LANES)),
                                jnp.broadcast_to(sel_c[:, b0 % LANES + 1:b0 % LANES + 2], (rows, LANES))))
    kpos = _lane_iota((1, P))
    bias = jnp.where((jnp.concatenate(chosen, axis=1) > 0.5) & (kpos <= P + t), 0.0, NEG_INF)
    update(_dot(q2, kt_ref[...]) + bias, lambda p: _dot_nt(p, vt_ref[...]))

    pad_rows = lambda x: jnp.concatenate([x, jnp.zeros((LANES - TS, LANES), F32)], axis=0).astype(BF16)
    own = (lane >> 6) == (_row_iota((rows, 1)) // (Q_PER_KV * TS))
    jn = P // SLC_BLOCK
    sel_n = sel_rows(jn // LANES)[:, jn % LANES:jn % LANES + 1]
    bias_n = jnp.where((sel_n > 0.5) & (lane <= t) & (lane < TS), 0.0, NEG_INF)
    vn = pad_rows(vsn_ref[...])
    update(_dot_nt(q2, pad_rows(ksn_ref[...])) + bias_n, lambda p: _dot(p, vn))
    o_slc = result()
    reset()
    kw_t, vw_t = kwin_ref[...], vwin_ref[...]
    iw = _lane_iota((1, wb))
    rel = t + wb - iw
    bias_b = jnp.where((rel >= 0) & (rel < WINDOW), 0.0, NEG_INF)
    vw_tb = vw_t.astype(BF16)
    update(_dot(q2, kw_t.astype(BF16)) + bias_b, lambda p: _dot_nt(p, vw_tb))
    bias_w = jnp.where((lane <= t) & (lane < TS), 0.0, NEG_INF)
    vwn = pad_rows(vwn_ref[...])
    update(_dot_nt(q2, pad_rows(kwn_ref[...])) + bias_w, lambda p: _dot(p, vwn))
    o_win = result()
    gs = jax.nn.sigmoid(g_ref[...])
    gcol = lambda comp: jnp.concatenate(
        [gs[:, 3 * hq + comp:3 * hq + comp + 1] for hq in range(NSA_HEADS)], axis=0)
    ocmp = jnp.concatenate([ocmp_ref[kv] for kv in range(NSA_KV_HEADS)], axis=0)
    o = gcol(0) * ocmp + gcol(1) * o_slc + gcol(2) * o_win
    o = jnp.where(own, o, 0.0)
    for tile in range(NSA_HEADS // 2):
        a = o[(2 * tile) * TS:(2 * tile + 1) * TS]
        b = o[(2 * tile + 1) * TS:(2 * tile + 2) * TS]
        kv = (2 * tile) // Q_PER_KV
        a = a if kv == 0 else pltpu.roll(a, HEAD_DIM, 1)
        b = b if kv == 1 else pltpu.roll(b, HEAD_DIM, 1)
        o_ref[:, tile * LANES:(tile + 1) * LANES] = a + b
    kwout_ref[0:wb - TS, :] = kw_t.T[TS:wb, :]
    kwout_ref[wb - TS:wb, :] = kwn_ref[...]
    vwout_ref[0:wb - TS, :] = vw_t.T[TS:wb, :]
    vwout_ref[wb - TS:wb, :] = vwn_ref[...]
    fetch.end()


def _nsa_sample_slc(page_table, q, sel, ocmp, g, ksn, vsn, kwn, vwn, kwin_t, vwin_t, ck_t, cv_t):
    S, TS, _ = q.shape
    n_pages = page_table.shape[1]
    P = n_pages * PAGE_SIZE
    wb = kwin_t.shape[2]
    rows = NSA_KV_HEADS * Q_PER_KV * TS
    hbm = pl.BlockSpec(memory_space=pl.ANY)
    per_seq = lambda shape: pl.BlockSpec((None,) + shape, lambda b, pt, nd=len(shape): (b,) + (0,) * nd)
    grid_spec = pltpu.PrefetchScalarGridSpec(
        num_scalar_prefetch=1,
        grid=(S,),
        in_specs=[per_seq((TS, NSA_WIDTH)), per_seq(sel.shape[1:]), per_seq(ocmp.shape[1:]), per_seq((TS, LANES)),
                  per_seq((TS, KV_WIDTH)), per_seq((TS, KV_WIDTH)), per_seq((TS, KV_WIDTH)), per_seq((TS, KV_WIDTH)),
                  per_seq((KV_WIDTH, wb)), per_seq((KV_WIDTH, wb)), hbm, hbm],
        out_specs=[per_seq((TS, NSA_WIDTH)), per_seq((wb, KV_WIDTH)), per_seq((wb, KV_WIDTH))],
        scratch_shapes=[pltpu.VMEM((KV_WIDTH, P), BF16), pltpu.VMEM((KV_WIDTH, P), BF16)]
        + [pltpu.VMEM((rows, LANES), F32)] * 3 + _page_scratch(n_pages, 2),
    )
    return pl.pallas_call(
        functools.partial(_nsa_sample_slc_body, P=P, TS=TS),
        grid_spec=grid_spec,
        out_shape=[jax.ShapeDtypeStruct((S, TS, NSA_WIDTH), F32),
                   jax.ShapeDtypeStruct((S, wb, KV_WIDTH), F32), jax.ShapeDtypeStruct((S, wb, KV_WIDTH), F32)],
        compiler_params=_params(("arbitrary",)),
        name="nsa_sample_slc",
    )(page_table, q, sel, ocmp, g, ksn, vsn, kwn, vwn, kwin_t, vwin_t, ck_t, cv_t)


def _merge_body(x_ref, glu_ref, o_ref, nw_ref, wg_ref, wup_ref, wnsa_ref, wout_ref, out_ref):
    x = x_ref[...]
    gate = jax.nn.sigmoid(_dot(_rms(x, nw_ref[...]).astype(BF16), wg_ref[...]))
    s5_out = _dot(glu_ref[...].astype(BF16), wup_ref[...])
    nsa_out = _dot(o_ref[...].astype(BF16), wnsa_ref[...])
    merged = gate[:, :D_MODEL] * s5_out + gate[:, D_MODEL:] * nsa_out
    out_ref[...] = x + _dot(merged.astype(BF16), wout_ref[...])


def _merge(x, glu, o, nw, wg, wup, wnsa, wout, tt):
    B, T, _ = x.shape
    row = lambda w: pl.BlockSpec((None, tt, w), lambda b, i: (b, i, 0))
    return pl.pallas_call(
        _merge_body,
        grid=(B, T // tt),
        in_specs=[row(D_MODEL), row(S5_WIDTH), row(NSA_WIDTH), _const_spec(nw.shape), _const_spec(wg.shape),
                  _const_spec(wup.shape), _const_spec(wnsa.shape), _const_spec(wout.shape)],
        out_specs=row(D_MODEL),
        out_shape=jax.ShapeDtypeStruct((B, T, D_MODEL), F32),
        compiler_params=_params(("parallel", "parallel")),
        name="merge",
    )(x, glu, o, nw, wg, wup, wnsa, wout)


def _ffn_body(x_ref, nw_ref, w1_ref, w2_ref, out_ref):
    x = x_ref[...]
    a = jnp.maximum(_dot(_rms(x, nw_ref[...]).astype(BF16), w1_ref[...]), 0.0)
    out_ref[...] = x + _dot((a * a).astype(BF16), w2_ref[...])


def _ffn(x, nw, w1, w2, tt):
    B, T, _ = x.shape
    row = pl.BlockSpec((None, tt, D_MODEL), lambda b, i: (b, i, 0))
    return pl.pallas_call(
        _ffn_body,
        grid=(B, T // tt),
        in_specs=[row, _const_spec(nw.shape), _const_spec(w1.shape), _const_spec(w2.shape)],
        out_specs=row,
        out_shape=jax.ShapeDtypeStruct((B, T, D_MODEL), F32),
        compiler_params=_params(("parallel", "parallel")),
        name="ffn",
    )(x, nw, w1, w2)


def _rope_tables(pos):
    half = HEAD_DIM // 2
    inv = ROPE_THETA ** (-jnp.arange(half, dtype=F32) / half)
    ang = pos.astype(F32)[:, None] * inv[None, :]
    cos, sin = jnp.cos(ang), jnp.sin(ang)
    return (jnp.concatenate([cos, cos, cos, cos], axis=-1),
            jnp.concatenate([-sin, sin, -sin, sin], axis=-1))


def _block_diag(w):
    G, a, b = w.shape
    return jnp.einsum('gab,gk->gakb', w, jnp.eye(G, dtype=w.dtype)).reshape(G * a, G * b)


def _cmp_weights(pe, w1, w2):
    eye = jnp.eye(NSA_KV_HEADS, dtype=F32)
    half = CMP_LEN // 2
    big = lambda w: jnp.einsum('rdf,hk->rhdkf', w, eye).reshape(half * KV_WIDTH, NSA_KV_HEADS * CMP_HIDDEN)
    w1b = jnp.concatenate([big(w1[:half]), big(w1[half:])], axis=1).astype(BF16)
    w2d = jnp.concatenate([w2, w2], axis=1).astype(BF16)
    flat = lambda p: jnp.broadcast_to(p[:, None, :], (half, NSA_KV_HEADS, HEAD_DIM)).reshape(1, half * KV_WIDTH)
    pe8 = jnp.concatenate([flat(pe[:half]), flat(pe[half:]), jnp.zeros((SUBLANES - 2, half * KV_WIDTH), F32)], axis=0)
    return w1b, w2d, pe8


def kernel(x_prompt, x_sample, cache_k_cmp, cache_v_cmp, cache_k_slc, cache_v_slc, state_k_win, state_v_win, state_s5_re, state_s5_im, page_table, norm_mix_w, w_in, s5_lam_re, s5_lam_im, s5_log_dt, s5_b_re, s5_b_im, s5_c_re, s5_c_im, s5_d, s5_w_glu, s5_w_up, q_norm_w, k_norm_w, cmp_pe_k, cmp_wk1, cmp_wk2, cmp_pe_v, cmp_wv1, cmp_wv2, nsa_w_up, w_out, norm_ffn_w, w_ffn1, w_ffn2):
    B, T, _ = x_prompt.shape
    S, TS, _ = x_sample.shape
    n_pages = page_table.shape[1]
    P = n_pages * PAGE_SIZE
    n_pool = cache_k_cmp.shape[1]
    assert norm_mix_w.shape[0] == 1 and B % SUBLANES == 0 and S % SUBLANES == 0
    assert TS < CMP_STRIDE and P % SLC_BLOCK == 0

    w = w_in[0]
    cols = lambda a, n: w[:, a:a + n]
    w_a = jnp.concatenate(
        [cols(0, 512), cols(512, 512), cols(1024, 128), cols(1280, 128), cols(1536, 128),
         cols(1152, 128), cols(1408, 128), cols(1664, 128), cols(1792, N_GATE),
         jnp.zeros((D_MODEL, W_A_COLS - 1792 - N_GATE), F32)], axis=1).astype(BF16)
    w_g = w[:, OFF_GMIX:].astype(BF16)
    nw_mix = norm_mix_w[0][None, :]
    qkw = jnp.concatenate([jnp.tile(q_norm_w[0], NSA_HEADS)]
                          + [jnp.tile(k_norm_w[0, i], NSA_KV_HEADS) for i in range(3)])[None, :]
    li = np.arange(2 * LANES)
    seg = jnp.asarray((li[:, None] // HEAD_DIM) == (li[None, :] // HEAD_DIM), BF16)
    lre, lim = s5_lam_re[0].reshape(1, S5_LANES), s5_lam_im[0].reshape(1, S5_LANES)
    ldt = jnp.repeat(s5_log_dt[0], S5_STATE)[None, :]
    bre = _block_diag(jnp.swapaxes(s5_b_re[0], 1, 2))
    bim = _block_diag(jnp.swapaxes(s5_b_im[0], 1, 2))
    cre = _block_diag(jnp.swapaxes(s5_c_re[0], 1, 2)).astype(BF16)
    cim = _block_diag(jnp.swapaxes(s5_c_im[0], 1, 2)).astype(BF16)
    s5d = s5_d[0][None, :]
    wglu = s5_w_glu[0].astype(BF16)
    wup = s5_w_up[0].astype(BF16)
    cw = _cmp_weights(cmp_pe_k[0], cmp_wk1[0], cmp_wk2[0]) + _cmp_weights(cmp_pe_v[0], cmp_wv1[0], cmp_wv2[0])
    wnsa = nsa_w_up[0].astype(BF16)
    wout = w_out[0].astype(BF16)
    nw_ffn = norm_ffn_w[0][None, :]
    w1 = w_ffn1[0].astype(BF16)
    w2 = w_ffn2[0].astype(BF16)

    def trunk(x, glu, o, tt):
        return _ffn(_merge(x, glu, o, nw_mix, w_g, wup, wnsa, wout, tt), nw_ffn, w1, w2, tt)

    cos_p, sin_p = _rope_tables(jnp.arange(T, dtype=jnp.int32))
    u, q, kc, ks, kw, vc, vs, vw, g, kc16, vc16 = _inproj(x_prompt, cos_p, sin_p, nw_mix, w_a, qkw, seg, TOKEN_TILE)
    zeros_st = jnp.zeros((B, S5_LANES), F32)
    glu_tm, sre_p, sim_p = _s5(u, zeros_st, zeros_st, lre, lim, ldt, bre, bim, cre, cim, s5d, wglu, S5_CHUNK)
    o_p = _nsa_prompt(q, kc16, vc16, ks, vs, kw, vw, g, cw, NSA_TILE)
    y_prompt = trunk(x_prompt, glu_tm, o_p, TOKEN_TILE)

    n_s = S * TS
    cos_s, sin_s = _rope_tables(P + jnp.arange(TS, dtype=jnp.int32))
    cos_s, sin_s = jnp.tile(cos_s, (S, 1)), jnp.tile(sin_s, (S, 1))
    xs = x_sample.reshape(1, n_s, D_MODEL)
    us, qs, kcs, kss, kws, vcs, vss, vws, gs, _, _ = _inproj(xs, cos_s, sin_s, nw_mix, w_a, qkw, seg, n_s)
    seq = lambda a: a.reshape(S, TS, a.shape[-1])
    glu_s_tm, sre_s, sim_s = _s5(seq(us), state_s5_re[0].reshape(S, S5_LANES), state_s5_im[0].reshape(S, S5_LANES),
                                 lre, lim, ldt, bre, bim, cre, cim, s5d, wglu, TS)
    pool = lambda c: jnp.transpose(c[0], (0, 2, 3, 1)).reshape(n_pool, KV_WIDTH, PAGE_SIZE)
    ocmp_s, sel_s = _nsa_sample_cmp(page_table, seq(qs), pool(cache_k_cmp), pool(cache_v_cmp), cw)
    win = lambda s: jnp.transpose(s[0], (0, 2, 3, 1)).reshape(S, KV_WIDTH, s.shape[2])
    o_s, kwin_s, vwin_s = _nsa_sample_slc(page_table, seq(qs), sel_s, ocmp_s, seq(gs), seq(kss), seq(vss), seq(kws),
                                          seq(vws), win(state_k_win), win(state_v_win),
                                          pool(cache_k_slc), pool(cache_v_slc))
    glu_s = glu_s_tm.reshape(1, n_s, S5_WIDTH)
    y_sample = trunk(xs, glu_s, o_s.reshape(1, n_s, NSA_WIDTH), n_s).reshape(S, TS, D_MODEL)

    heads_p = lambda a: a.reshape(1, B, a.shape[1], NSA_KV_HEADS, HEAD_DIM)
    heads_s = lambda a: a.reshape(1, S, -1, NSA_KV_HEADS, HEAD_DIM)
    keep = min(WINDOW, T)
    st_p = lambda a: a.reshape(1, B, S5_GROUPS, S5_STATE)
    st_s = lambda a: a.reshape(1, S, S5_GROUPS, S5_STATE)
    return (y_prompt, y_sample,
            heads_p(kc), heads_s(kcs), heads_p(vc), heads_s(vcs),
            heads_p(ks), heads_s(kss), heads_p(vs), heads_s(vss),
            heads_p(kw[:, T - keep:]), heads_s(kwin_s), heads_p(vw[:, T - keep:]), heads_s(vwin_s),
            st_p(sre_p), st_s(sre_s), st_p(sim_p), st_s(sim_s))
```

```python
import functools

import numpy as np
import jax
import jax.numpy as jnp
from jax import lax
from jax.experimental import pallas as pl
from jax.experimental.pallas import tpu as pltpu

F32 = jnp.float32
BF16 = jnp.bfloat16

D_MODEL = 1024
PAGE_SIZE = 128
S5_WIDTH = 512
S5_GROUP = 16
S5_GROUPS = 32
S5_STATE = 64
S5_LANES = S5_GROUPS * S5_STATE
HEAD_DIM = 64
NSA_HEADS = 8
NSA_KV_HEADS = 2
Q_PER_KV = 4
NSA_WIDTH = 512
KV_WIDTH = 128
CMP_LEN = 32
CMP_STRIDE = 16
CMP_HIDDEN = 128
SLC_BLOCK = 64
SLC_TOPK = 16
WINDOW = 512
ROPE_THETA = 10000.0
EPS = 1e-6
NEG_INF = -1e30
FORCED = 1e9
SCALE = HEAD_DIM ** -0.5
N_GATE = 3 * NSA_HEADS
OFF_GMIX = 2 * 512 + 6 * KV_WIDTH + N_GATE
W_A_COLS = 1920

LANES = 128
SUBLANES = 8
VMEM_LIMIT = 56 * 1024 * 1024
TOKEN_TILE = 512
S5_CHUNK = 64
NSA_TILE = 256
BOUND_SLACK = 1.02
MAX_SHIFT_GAP = 60.0


def _params(sem):
    return pltpu.CompilerParams(dimension_semantics=sem, vmem_limit_bytes=VMEM_LIMIT)


def _const_spec(shape):
    nd = len(shape)
    return pl.BlockSpec(shape, lambda *_: (0,) * nd)


def _dot(a, b):
    return jnp.dot(a, b, preferred_element_type=F32)


def _dot_nt(a, b):
    return lax.dot_general(a, b, (((1,), (1,)), ((), ())), preferred_element_type=F32)


def _rms(x, w):
    r = lax.rsqrt(jnp.mean(x * x, axis=-1, keepdims=True) + EPS)
    return x * r * w


def _lane_iota(shape):
    return lax.broadcasted_iota(jnp.int32, shape, len(shape) - 1)


def _row_iota(shape):
    return lax.broadcasted_iota(jnp.int32, shape, len(shape) - 2)


def _split3(x):
    hi = x.astype(BF16)
    r1 = x - hi.astype(F32)
    mid = r1.astype(BF16)
    lo = (r1 - mid.astype(F32)).astype(BF16)
    return hi, mid, lo


def _inproj_body(x_ref, nw_ref, w_ref, qkw_ref, seg_ref, cos_ref, sin_ref,
                 u_ref, q_ref, kc_ref, ks_ref, kw_ref, vc_ref, vs_ref, vw_ref, g_ref, kc16_ref, vc16_ref):
    xn = _rms(x_ref[...], nw_ref[...]).astype(BF16)
    z = _dot(xn, w_ref[...])
    u_ref[...] = z[:, :512]
    cos = cos_ref[...]
    sin = sin_ref[...]
    seg = seg_ref[...]
    first_half = (_lane_iota((1, LANES)) % HEAD_DIM) < (HEAD_DIM // 2)

    def norm_rope(t, w):
        width = t.shape[1]
        ss = _dot((t * t).astype(BF16), seg[:width, :width])
        t = t * lax.rsqrt(ss * (1.0 / HEAD_DIM) + EPS) * w
        out = []
        for j in range(width // LANES):
            tj = t[:, j * LANES:(j + 1) * LANES]
            rot = jnp.where(first_half, pltpu.roll(tj, LANES - HEAD_DIM // 2, 1), pltpu.roll(tj, HEAD_DIM // 2, 1))
            out.append(tj * cos + rot * sin)
        return out

    qk = [norm_rope(z[:, 512 + j * 256:512 + (j + 1) * 256], qkw_ref[:, j * 256:(j + 1) * 256]) for j in range(3)]
    qk.append(norm_rope(z[:, 1280:1408], qkw_ref[:, 768:896]))
    tiles = [t for pair in qk for t in pair]
    for j in range(4):
        q_ref[:, j * LANES:(j + 1) * LANES] = tiles[j]
    for j, r in enumerate((kc_ref, ks_ref, kw_ref)):
        r[...] = tiles[4 + j]
    vc_ref[...] = z[:, 1408:1536]
    vs_ref[...] = z[:, 1536:1664]
    vw_ref[...] = z[:, 1664:1792]
    g_ref[...] = z[:, 1792:1920]
    gpc = LANES // CMP_STRIDE
    ri = _row_iota((LANES, LANES))
    pick = jnp.where(_lane_iota((LANES, LANES)) == CMP_STRIDE * (ri % gpc) + ri // gpc, 1.0, 0.0).astype(BF16)
    for rows, out16 in ((tiles[4], kc16_ref), (z[:, 1408:1536], vc16_ref)):
        for j in range(rows.shape[0] // LANES):
            by_r = _dot(pick, rows[j * LANES:(j + 1) * LANES].astype(BF16))
            for r in range(CMP_STRIDE):
                out16[j * gpc:(j + 1) * gpc, r * LANES:(r + 1) * LANES] = by_r[r * gpc:(r + 1) * gpc]


def _inproj(x, cos_t, sin_t, nw, w_a, qkw, seg, tt):
    B, T, _ = x.shape
    row = lambda w: pl.BlockSpec((None, tt, w), lambda b, i: (b, i, 0))
    tab = pl.BlockSpec((tt, LANES), lambda b, i: (i, 0))
    kv = jax.ShapeDtypeStruct((B, T, KV_WIDTH), F32)
    g16 = jax.ShapeDtypeStruct((B, T // CMP_STRIDE, CMP_STRIDE * KV_WIDTH), F32)
    row16 = pl.BlockSpec((None, tt // CMP_STRIDE, CMP_STRIDE * KV_WIDTH), lambda b, i: (b, i, 0))
    out_shape = [jax.ShapeDtypeStruct((B, T, 512), F32), jax.ShapeDtypeStruct((B, T, 512), F32)] + [kv] * 7 + [g16] * 2
    return pl.pallas_call(
        _inproj_body,
        grid=(B, T // tt),
        in_specs=[row(D_MODEL), _const_spec((1, D_MODEL)), _const_spec((D_MODEL, W_A_COLS)),
                  _const_spec((1, 896)), _const_spec((2 * LANES, 2 * LANES)), tab, tab],
        out_specs=[row(512), row(512)] + [row(KV_WIDTH)] * 7 + [row16] * 2,
        out_shape=out_shape,
        compiler_params=_params(("parallel", "parallel")),
        name="inproj",
    )(x, nw, w_a, qkw, seg, cos_t, sin_t)


def _s5_body(u_ref, sre0_ref, sim0_ref, lre_ref, lim_ref, ldt_ref, bre_ref, bim_ref, cre_ref, cim_ref,
             d_ref, wglu_ref, out_ref, sre_ref, sim_ref,
             bbar_ref, a_ref, st_ref, xr_ref, xi_ref, *, tc):
    c = pl.program_id(1)
    rows = tc * SUBLANES

    @pl.when(c == 0)
    def _():
        lr, li = lre_ref[...], lim_ref[...]
        dt = jnp.exp(ldt_ref[...])
        mag = jnp.exp(lr * dt)
        ab_re, ab_im = mag * jnp.cos(li * dt), mag * jnp.sin(li * dt)
        den = lr * lr + li * li
        f_re = ((ab_re - 1.0) * lr + ab_im * li) / den
        f_im = (ab_im * lr - (ab_re - 1.0) * li) / den
        on_diag = (_lane_iota((S5_WIDTH, S5_LANES)) // S5_STATE) == (_row_iota((S5_WIDTH, S5_LANES)) // S5_GROUP)
        spread = lambda b_ref: jnp.where(on_diag, jnp.concatenate([b_ref[...]] * (S5_LANES // LANES), axis=1), 0.0)
        bre, bim = spread(bre_ref), spread(bim_ref)
        bbar_ref[:, :S5_LANES] = (f_re * bre - f_im * bim).astype(BF16)
        bbar_ref[:, S5_LANES:] = (f_re * bim + f_im * bre).astype(BF16)
        a_ref[0] = jnp.broadcast_to(ab_re, (SUBLANES, S5_LANES))
        a_ref[1] = jnp.broadcast_to(ab_im, (SUBLANES, S5_LANES))
        st_ref[0] = sre0_ref[...]
        st_ref[1] = sim0_ref[...]

    u = jnp.swapaxes(u_ref[...], 0, 1).reshape(rows, S5_WIDTH)
    ub = u.astype(BF16)
    hw, hl = S5_WIDTH // 2, S5_LANES // 2
    for n in range(2):
        xr_ref[:, n * hl:(n + 1) * hl] = _dot(ub[:, n * hw:(n + 1) * hw], bbar_ref[n * hw:(n + 1) * hw, n * hl:(n + 1) * hl])
        xi_ref[:, n * hl:(n + 1) * hl] = _dot(ub[:, n * hw:(n + 1) * hw],
                                              bbar_ref[n * hw:(n + 1) * hw, S5_LANES + n * hl:S5_LANES + (n + 1) * hl])

    slab = 512
    for lc in range(S5_LANES // slab):
        sl = slice(lc * slab, (lc + 1) * slab)
        ar, ai = a_ref[0, :, sl], a_ref[1, :, sl]

        def step(t, carry, sl=sl, ar=ar, ai=ai):
            sr, si = carry
            r0 = pl.multiple_of(t * SUBLANES, SUBLANES)
            nr = ar * sr - ai * si + xr_ref[pl.ds(r0, SUBLANES), sl]
            ni = ar * si + ai * sr + xi_ref[pl.ds(r0, SUBLANES), sl]
            xr_ref[pl.ds(r0, SUBLANES), sl] = nr
            xi_ref[pl.ds(r0, SUBLANES), sl] = ni
            return nr, ni

        sr, si = lax.fori_loop(0, tc, step, (st_ref[0, :, sl], st_ref[1, :, sl]), unroll=min(tc, 8))
        st_ref[0, :, sl] = sr
        st_ref[1, :, sl] = si

    y = jnp.concatenate(
        [_dot(xr_ref[:, n * hl:(n + 1) * hl].astype(BF16), cre_ref[n * hl:(n + 1) * hl, n * hw:(n + 1) * hw])
         - _dot(xi_ref[:, n * hl:(n + 1) * hl].astype(BF16), cim_ref[n * hl:(n + 1) * hl, n * hw:(n + 1) * hw])
         for n in range(2)], axis=1) + d_ref[...] * u
    yg = jax.nn.gelu(y)
    glu = yg * jax.nn.sigmoid(_dot(yg.astype(BF16), wglu_ref[...]))
    out_ref[...] = jnp.swapaxes(glu.reshape(tc, SUBLANES, S5_WIDTH), 0, 1)

    @pl.when(c == pl.num_programs(1) - 1)
    def _():
        sre_ref[...] = st_ref[0]
        sim_ref[...] = st_ref[1]


def _s5(u_tm, sre0, sim0, lre, lim, ldt, bre, bim, cre, cim, d, wglu, tc):
    B, T, _ = u_tm.shape
    st_spec = pl.BlockSpec((SUBLANES, S5_LANES), lambda b, c: (b, 0))
    blk = pl.BlockSpec((SUBLANES, tc, S5_WIDTH), lambda b, c: (b, c, 0))
    st = jax.ShapeDtypeStruct((B, S5_LANES), F32)
    return pl.pallas_call(
        functools.partial(_s5_body, tc=tc),
        grid=(B // SUBLANES, T // tc),
        in_specs=[blk, st_spec, st_spec,
                  _const_spec((1, S5_LANES)), _const_spec((1, S5_LANES)), _const_spec((1, S5_LANES)),
                  _const_spec((S5_WIDTH, LANES)), _const_spec((S5_WIDTH, LANES)),
                  _const_spec((S5_LANES, S5_WIDTH)), _const_spec((S5_LANES, S5_WIDTH)),
                  _const_spec((1, S5_WIDTH)), _const_spec((S5_WIDTH, S5_WIDTH))],
        out_specs=[blk, st_spec, st_spec],
        out_shape=[jax.ShapeDtypeStruct((B, T, S5_WIDTH), F32), st, st],
        scratch_shapes=[pltpu.VMEM((S5_WIDTH, 2 * S5_LANES), BF16),
                        pltpu.VMEM((2, SUBLANES, S5_LANES), F32),
                        pltpu.VMEM((2, SUBLANES, S5_LANES), F32),
                        pltpu.VMEM((tc * SUBLANES, S5_LANES), F32),
                        pltpu.VMEM((tc * SUBLANES, S5_LANES), F32)],
        compiler_params=_params(("parallel", "arbitrary")),
        name="s5",
    )(u_tm, sre0, sim0, lre, lim, ldt, bre, bim, cre, cim, d, wglu)


def _head_lanes(x, half):
    return jnp.where((_lane_iota((1, LANES)) >> 6) == half, x, 0.0)


def _masked_softmax_parts(s, mask):
    s = jnp.where(mask, s, NEG_INF)
    m = jnp.max(s, axis=-1, keepdims=True)
    p = jnp.where(mask, jnp.exp(s - m), 0.0)
    l = jnp.maximum(jnp.sum(p, axis=-1, keepdims=True), 1e-30)
    return p, l


def _cmp_hidden(h, w1_ref, pe_ref):
    m = h.shape[0]
    pe = pe_ref[...].astype(BF16)
    bias = _dot(pe, w1_ref[:, :256])[0:1] + _dot(pe, w1_ref[:, 256:])[1:2]
    return jax.nn.gelu(h[:, :256] + pltpu.roll(h[:, 256:], m - 1, 0) + bias)


def _overlap(n_rows, n_cols, transpose):
    shape = (n_cols, n_rows) if transpose else (n_rows, n_cols)
    n = _row_iota(shape) if not transpose else _lane_iota(shape)
    j = _lane_iota(shape) if not transpose else _row_iota(shape)
    lo = jnp.maximum(n * CMP_STRIDE, j * SLC_BLOCK)
    hi = jnp.minimum(n * CMP_STRIDE + CMP_LEN, j * SLC_BLOCK + SLC_BLOCK)
    return (jnp.maximum(hi - lo, 0).astype(F32) * (1.0 / CMP_LEN)).astype(BF16)


VT_ROWS = HEAD_DIM + 16


def _ones_row_rows(n):
    return jnp.where(_row_iota((VT_ROWS - HEAD_DIM, n)) == 0, 1.0, 0.0)


def _shifted_update_t(m_ref, acc_ref, h, s_t, v_t):
    tk, tq = s_t.shape
    m_ref[h] = jnp.maximum(m_ref[h], jnp.max(s_t.reshape(tk // SUBLANES, SUBLANES, tq), axis=0))
    acc_ref[h] = acc_ref[h] + _dot(v_t, jnp.exp(s_t).astype(BF16))


def _online_update_t(m_ref, acc_ref, h, s_t, v_t):
    m_old = m_ref[h]
    m_new = jnp.maximum(m_old, jnp.max(s_t, axis=0, keepdims=True))
    p_t = jnp.exp(s_t - m_new[0:1])
    alpha = jnp.exp(m_old - m_new)
    acc_ref[h] = alpha[0:1] * acc_ref[h] + _dot(v_t, p_t.astype(BF16))
    m_ref[h] = m_new


def _softmax_result_t(m_ref, acc_ref, h):
    acc = acc_ref[h]
    out = acc[0:HEAD_DIM] / jnp.maximum(acc[HEAD_DIM:HEAD_DIM + 1], 1e-30)
    return jnp.where(jnp.max(m_ref[h], axis=0, keepdims=True) > 0.5 * NEG_INF, out, 0.0)


def _key_rows(k, kv):
    lane = _lane_iota((1, LANES))
    kk = jnp.where(lane < HEAD_DIM, jnp.where(kv == 0, k, pltpu.roll(k, HEAD_DIM, 1)), 0.0)
    norm2 = jnp.max(jnp.sum(kk * kk, axis=1, keepdims=True), axis=0, keepdims=True)
    return jnp.where(lane == HEAD_DIM, 1.0, kk).astype(BF16), jnp.sqrt(norm2)


def _nsa_prompt_body(q_ref, kc16_ref, vc16_ref, ks_ref, kw_ref, vs_ref, vw_ref, g_ref,
                     w1k_ref, w2k_ref, pek_ref, w1v_ref, w2v_ref, pev_ref,
                     o_ref,
                     ksa_ref, kwa_ref, vst_ref, vwt_ref, kcmp_ref, vcmpt_ref, knorm_ref,
                     qc_ref, qs_ref, qw_ref, selb_ref, ms_ref, accs_ref, mw_ref, accw_ref, *, T, TQ):
    kv = pl.program_id(1)
    m_cmp = T // CMP_STRIDE
    n_kt = T // TQ

    def prepare():
        ksa_ref[...], ns = _key_rows(ks_ref[...], kv)
        kwa_ref[...], nw = _key_rows(kw_ref[...], kv)
        knorm_ref[0] = jnp.broadcast_to(ns, (SUBLANES, LANES))
        knorm_ref[1] = jnp.broadcast_to(nw, (SUBLANES, LANES))
        ones = _ones_row_rows(TQ).astype(BF16)
        for kt in range(n_kt):
            for src, dst in ((vs_ref, vst_ref), (vw_ref, vwt_ref)):
                vt = src[kt * TQ:(kt + 1) * TQ, :].T
                dst[kt, 0:HEAD_DIM, :] = jnp.where(kv == 0, vt[0:HEAD_DIM], vt[HEAD_DIM:]).astype(BF16)
                dst[kt, HEAD_DIM:, :] = ones

        def compressed(x16_ref, w1_ref, pe_ref, w2_ref):
            hid = _cmp_hidden(_dot(x16_ref[...].astype(BF16), w1_ref[...]), w1_ref, pe_ref)
            hid = jnp.where(kv == 0, hid[:, :LANES], hid[:, LANES:])
            return _dot(hid.astype(BF16), w2_ref[...])

        kcmp_ref[...] = compressed(kc16_ref, w1k_ref, pek_ref, w2k_ref).astype(BF16)
        vcmp_t = compressed(vc16_ref, w1v_ref, pev_ref, w2v_ref).T
        vcmpt_ref[0:HEAD_DIM, :] = vcmp_t[0:HEAD_DIM].astype(BF16)
        vcmpt_ref[HEAD_DIM:, :] = jnp.zeros((HEAD_DIM, m_cmp), BF16)

    prepare()

    def query_tile(qi, carry):
        _nsa_prompt_tile(qi, kv, q_ref, g_ref, o_ref, ksa_ref, kwa_ref, vst_ref, vwt_ref, kcmp_ref, vcmpt_ref,
                         knorm_ref, qc_ref, qs_ref, qw_ref, selb_ref, ms_ref, accs_ref, mw_ref, accw_ref, T=T, TQ=TQ)
        return carry

    lax.fori_loop(0, n_kt, query_tile, 0)


def _nsa_prompt_tile(qi, kv, q_ref, g_ref, o_ref, ksa_ref, kwa_ref, vst_ref, vwt_ref, kcmp_ref, vcmpt_ref,
                     knorm_ref, qc_ref, qs_ref, qw_ref, selb_ref, ms_ref, accs_ref, mw_ref, accw_ref, *, T, TQ):
    m_cmp = T // CMP_STRIDE
    n_cmp = m_cmp - 1
    n_slc = T // SLC_BLOCK
    topk = min(SLC_TOPK, n_slc)
    n_kt = T // TQ
    lane = _lane_iota((1, LANES))
    r0 = pl.multiple_of(qi * TQ, TQ)
    for pp in range(2):
        qt = q_ref[pl.ds(r0, TQ), pp * LANES:(pp + 1) * LANES] * SCALE
        for e in range(2):
            qh = jnp.where(lane < HEAD_DIM, qt if e == 0 else pltpu.roll(qt, HEAD_DIM, 1), 0.0)
            qc_ref[2 * pp + e] = qh.astype(BF16)
            qnorm = jnp.sqrt(jnp.sum(qh * qh, axis=1, keepdims=True))
            for dst, which in ((qs_ref, 0), (qw_ref, 1)):
                bound = BOUND_SLACK * qnorm * knorm_ref[which][0:1, 0:1]
                dst[(2 * pp + e) * TQ:(2 * pp + e + 1) * TQ, :] = jnp.where(lane == HEAD_DIM, -bound, qh).astype(BF16)
    qpos = r0 + _lane_iota((1, TQ))

    n = _row_iota((m_cmp, TQ))
    mask_c = (n * CMP_STRIDE + CMP_LEN - 1 <= qpos) & (n < n_cmp)
    pg = jnp.zeros((m_cmp, TQ), F32)
    o_cmp = []
    for h in range(Q_PER_KV):
        s_t = jnp.where(mask_c, _dot_nt(kcmp_ref[...], qc_ref[h]), NEG_INF)
        p_t = jnp.where(mask_c, jnp.exp(s_t - jnp.max(s_t, axis=0, keepdims=True)), 0.0)
        inv_l = 1.0 / jnp.maximum(jnp.sum(p_t, axis=0, keepdims=True), 1e-30)
        pg = pg + p_t * inv_l
        o_cmp.append(_dot(vcmpt_ref[...], p_t.astype(BF16))[0:HEAD_DIM] * inv_l)

    ovt = _overlap(m_cmp, n_slc, transpose=True)
    imp = sum(_dot(ovt, part) for part in _split3(pg))
    j = _row_iota((n_slc, TQ))
    qblk = qpos // SLC_BLOCK
    valid = j <= qblk
    forced = (j == 0) | (j == qblk) | (j == qblk - 1)
    score = jnp.where(forced, FORCED, jnp.where(valid, imp, NEG_INF))
    rank = jnp.zeros((n_slc, TQ), jnp.int32)
    for i in range(n_slc):
        si = score[i:i + 1, :]
        beats = (si > score) | ((si == score) & (i < j))
        rank = rank + beats.astype(jnp.int32)
    def reset():
        for m_r, acc_r in ((ms_ref, accs_ref), (mw_ref, accw_ref)):
            m_r[...] = jnp.full(m_r.shape, NEG_INF, F32)
            acc_r[...] = jnp.zeros(acc_r.shape, F32)

    sel_bias = jnp.where((rank < topk) & valid, 0.0, NEG_INF)
    for jb in range(n_slc):
        selb_ref[jb] = jnp.broadcast_to(sel_bias[jb:jb + 1], (SUBLANES, TQ))
    not_after = _row_iota((TQ, 1)) <= _lane_iota((1, TQ))
    bias_diag = jnp.where(not_after, 0.0, NEG_INF)
    bias_far = jnp.where(not_after, NEG_INF, 0.0)
    blocks_per_tile = TQ // SLC_BLOCK
    win_tiles = WINDOW // TQ

    def both_branches(update):
        def apply(m_r, acc_r, s_all, bias, v_t):
            for h in range(Q_PER_KV):
                update(m_r, acc_r, h, s_all[:, h * TQ:(h + 1) * TQ] + bias, v_t)

        def slc_scores(kb):
            return _dot_nt(ksa_ref[pl.ds(pl.multiple_of(kb * TQ, TQ), TQ), :], qs_ref[...])

        def win_scores(kb):
            return _dot_nt(kwa_ref[pl.ds(pl.multiple_of(kb * TQ, TQ), TQ), :], qw_ref[...])

        def slc_bias(kb):
            rows = []
            for jb in range(blocks_per_tile):
                rows += [selb_ref[kb * blocks_per_tile + jb]] * (SLC_BLOCK // SUBLANES)
            return jnp.concatenate(rows, axis=0)

        def slc_pair(p, carry):
            ka = 2 * p
            kb = jnp.minimum(ka + 1, n_kt - 1)
            live = jnp.where(ka + 1 < qi, 0.0, NEG_INF)
            s_a, s_b = slc_scores(ka), slc_scores(kb)
            apply(ms_ref, accs_ref, s_a, slc_bias(ka), vst_ref[ka])
            apply(ms_ref, accs_ref, s_b, slc_bias(kb) + live, vst_ref[kb])
            return carry

        lax.fori_loop(0, (qi + 1) // 2, slc_pair, 0)

        win_kb = [jnp.maximum(qi - d, 0) for d in range(win_tiles + 1)]
        s_own = slc_scores(qi)
        s_win = [win_scores(kb) for kb in win_kb]
        apply(ms_ref, accs_ref, s_own, slc_bias(qi) + bias_diag, vst_ref[qi])
        for d, kb in enumerate(win_kb):
            exists = jnp.where(qi >= d, 0.0, NEG_INF)
            bias = exists + (bias_diag if d == 0 else bias_far if d == win_tiles else 0.0)
            apply(mw_ref, accw_ref, s_win[d], bias, vwt_ref[kb])

    reset()
    both_branches(_shifted_update_t)

    worst = jnp.full((1, TQ), 0.0, F32)
    for m_r in (ms_ref, mw_ref):
        for h in range(Q_PER_KV):
            top = jnp.max(m_r[h], axis=0, keepdims=True)
            worst = jnp.minimum(worst, jnp.where(top > 0.5 * NEG_INF, top, 0.0))
    redo = jnp.min(worst) < -MAX_SHIFT_GAP

    @pl.when(redo)
    def _():
        reset()
        both_branches(_online_update_t)

    g = g_ref[pl.ds(r0, TQ), :]
    gs_t = jax.nn.sigmoid(jnp.where(kv == 0, g, pltpu.roll(g, LANES - 3 * Q_PER_KV, 1))).T
    for pp in range(2):
        halves = []
        for e in range(2):
            h = 2 * pp + e
            halves.append(gs_t[3 * h:3 * h + 1] * o_cmp[h]
                          + gs_t[3 * h + 1:3 * h + 2] * _softmax_result_t(ms_ref, accs_ref, h)
                          + gs_t[3 * h + 2:3 * h + 3] * _softmax_result_t(mw_ref, accw_ref, h))
        o_ref[pl.ds(r0, TQ), pp * LANES:(pp + 1) * LANES] = jnp.concatenate(halves, axis=0).T


def _nsa_prompt(q, kc16, vc16, ks, vs, kw, vw, g, cw, tq):
    B, T, _ = q.shape
    m_cmp = T // CMP_STRIDE
    full = lambda w: pl.BlockSpec((None, T, w), lambda b, h: (b, 0, 0))
    x16 = pl.BlockSpec((None, m_cmp, CMP_STRIDE * KV_WIDTH), lambda b, h: (b, 0, 0))
    qo = pl.BlockSpec((None, T, 2 * LANES), lambda b, h: (b, 0, h))
    wspecs = [_const_spec(w.shape) for w in cw]
    k_rows = pltpu.VMEM((T, LANES), BF16)
    v_t = pltpu.VMEM((T // tq, VT_ROWS, tq), BF16)
    q_rows = pltpu.VMEM((Q_PER_KV, tq, LANES), BF16)
    q_all = pltpu.VMEM((Q_PER_KV * tq, LANES), BF16)
    run_max = pltpu.VMEM((Q_PER_KV, SUBLANES, tq), F32)
    run_acc = pltpu.VMEM((Q_PER_KV, VT_ROWS, tq), F32)
    return pl.pallas_call(
        functools.partial(_nsa_prompt_body, T=T, TQ=tq),
        grid=(B, NSA_KV_HEADS),
        in_specs=[qo, x16, x16, full(KV_WIDTH), full(KV_WIDTH), full(KV_WIDTH), full(KV_WIDTH), full(KV_WIDTH)] + wspecs,
        out_specs=qo,
        out_shape=jax.ShapeDtypeStruct((B, T, NSA_WIDTH), F32),
        scratch_shapes=[k_rows, k_rows, v_t, v_t,
                        pltpu.VMEM((m_cmp, LANES), BF16), pltpu.VMEM((LANES, m_cmp), BF16),
                        pltpu.VMEM((2, SUBLANES, LANES), F32),
                        q_rows, q_all, q_all, pltpu.VMEM((T // SLC_BLOCK, SUBLANES, tq), F32),
                        run_max, run_acc, run_max, run_acc],
        compiler_params=_params(("parallel", "arbitrary")),
        name="nsa_prompt",
    )(q, kc16, vc16, ks, kw, vs, vw, g, *cw)


def _stack_heads(q, kv, lane_half):
    rows = []
    for hl in range(Q_PER_KV):
        tile = 2 * kv + hl // 2
        x = _head_lanes(q[:, tile * LANES:(tile + 1) * LANES], hl % 2)
        rows.append(x if lane_half(hl) == hl % 2 else pltpu.roll(x, HEAD_DIM, 1))
    return jnp.concatenate(rows, axis=0)


class _PageFetch:
    def __init__(self, pt_ref, caches, bufs, sem_ref, n_pages):
        self.pt_ref, self.caches, self.bufs, self.sem_ref, self.n_pages = pt_ref, caches, bufs, sem_ref, n_pages
        self.b = pl.program_id(0)
        self.last = pl.num_programs(0) - 1
        self.slot = self.b % 2
        self.nxt = jnp.where(self.b == self.last, 0, self.b + 1)

    def _copy(self, which, seq, k, slot):
        return pltpu.make_async_copy(self.caches[which].at[self.pt_ref[seq, k]], self.bufs[which].at[slot, k],
                                     self.sem_ref.at[slot, which])

    def begin(self):
        @pl.when(self.b == 0)
        def _():
            for k in range(self.n_pages):
                for which in range(len(self.caches)):
                    self._copy(which, 0, k, 0).start()

        for k in range(self.n_pages):
            for which in range(len(self.caches)):
                self._copy(which, self.b, k, self.slot).wait()

    def prefetch(self, k):
        for which in range(len(self.caches)):
            self._copy(which, self.nxt, k, 1 - self.slot).start()

    def page(self, which, k):
        return self.bufs[which][self.slot, k]

    def end(self):
        @pl.when(self.b == self.last)
        def _():
            for k in range(self.n_pages):
                for which in range(len(self.caches)):
                    self._copy(which, self.nxt, k, 1 - self.slot).wait()


def _nsa_sample_cmp_body(pt_ref, q_ref, ck_ref, cv_ref, w1k_ref, w2k_ref, pek_ref, w1v_ref, w2v_ref, pev_ref,
                         ocmp_ref, sel_ref, xk_ref, xv_ref, kbuf_ref, vbuf_ref, sem_ref, *, P, TS):
    n_pages = P // PAGE_SIZE
    m_cmp = P // CMP_STRIDE
    n_cmp = (P + TS - CMP_LEN) // CMP_STRIDE + 1
    n_slc = -(-(P + TS) // SLC_BLOCK)
    n_slc_pad = -(-n_slc // LANES) * LANES
    topk = min(SLC_TOPK, n_slc)
    fetch = _PageFetch(pt_ref, (ck_ref, cv_ref), (kbuf_ref, vbuf_ref), sem_ref, n_pages)
    fetch.begin()

    gpp = PAGE_SIZE // CMP_STRIDE
    ri = _row_iota((PAGE_SIZE, PAGE_SIZE))
    li = _lane_iota((PAGE_SIZE, PAGE_SIZE))
    pick = jnp.where(ri == CMP_STRIDE * (li % gpp) + li // gpp, 1.0, 0.0).astype(BF16)
    zero = jnp.zeros_like(pick)
    pick2 = jnp.concatenate([jnp.concatenate([pick, zero], axis=1), jnp.concatenate([zero, pick], axis=1)], axis=0)
    for k in range(0, n_pages, 2):
        fetch.prefetch(k)
        fetch.prefetch(k + 1)
        for which, x_ref in ((0, xk_ref), (1, xv_ref)):
            two = jnp.concatenate([fetch.page(which, k), fetch.page(which, k + 1)], axis=1).astype(BF16)
            rows_by_r = _dot(two, pick2).T
            for r in range(2 * CMP_STRIDE):
                kk, rr = k + r // CMP_STRIDE, r % CMP_STRIDE
                x_ref[rr, kk * gpp:(kk + 1) * gpp, :] = rows_by_r[r * gpp:(r + 1) * gpp, :]
    hid = []
    for x_ref, w1_ref, pe_ref in ((xk_ref, w1k_ref, pek_ref), (xv_ref, w1v_ref, pev_ref)):
        x16 = jnp.concatenate([x_ref[r].astype(BF16) for r in range(CMP_STRIDE)], axis=1)
        hid.append(_cmp_hidden(_dot(x16, w1_ref[...]), w1_ref, pe_ref).astype(BF16))
    hidk, hidv = hid

    q = q_ref[...] * SCALE
    rows = Q_PER_KV * TS
    t = _row_iota((rows, 1)) % TS
    qpos = P + t
    n = _lane_iota((rows, m_cmp))
    mask = (n * CMP_STRIDE + CMP_LEN - 1 <= qpos) & (n < n_cmp)
    ov = _overlap(m_cmp, n_slc_pad, transpose=False)
    jl = _lane_iota((TS, n_slc_pad))
    qblk = (P + _row_iota((TS, 1))) // SLC_BLOCK
    valid = (jl <= qblk) & (jl < n_slc)
    forced = ((jl == 0) | (jl == qblk) | (jl == qblk - 1)) & (jl < n_slc)
    imps = []
    for kv in range(NSA_KV_HEADS):
        kcmp = _dot(hidk[:, kv * LANES:(kv + 1) * LANES], w2k_ref[...]).astype(BF16)
        vcmp = _dot(hidv[:, kv * LANES:(kv + 1) * LANES], w2v_ref[...]).astype(BF16)
        qs = _stack_heads(q, kv, lambda hl: hl % 2).astype(BF16)
        p, l = _masked_softmax_parts(_dot_nt(qs, kcmp), mask)
        ocmp_ref[kv] = _dot(p.astype(BF16), vcmp) / l
        pn = p / l
        pg = pn[0:TS] + pn[TS:2 * TS] + pn[2 * TS:3 * TS] + pn[3 * TS:4 * TS]
        imps.append(sum(_dot(part, ov) for part in _split3(pg)))
    tile2 = lambda a: jnp.concatenate([a] * NSA_KV_HEADS, axis=0)
    valid2, jl2 = tile2(valid), tile2(jl)
    score = jnp.where(tile2(forced), FORCED, jnp.where(valid2, jnp.concatenate(imps, axis=0), NEG_INF))
    rank = jnp.zeros(score.shape, jnp.int32)
    for i in range(n_slc):
        si = score[:, i:i + 1]
        beats = (si > score) | ((si == score) & (i < jl2))
        rank = rank + beats.astype(jnp.int32)
    sel = jnp.where((rank < topk) & valid2, 1.0, 0.0)
    for kv in range(NSA_KV_HEADS):
        for ch in range(n_slc_pad // LANES):
            sel_ref[kv, ch] = sel[kv * TS:(kv + 1) * TS, ch * LANES:(ch + 1) * LANES]
    fetch.end()


def _page_scratch(n_pages, n_caches):
    return ([pltpu.VMEM((2, n_pages, KV_WIDTH, PAGE_SIZE), F32)] * n_caches
            + [pltpu.SemaphoreType.DMA((2, n_caches))])


def _nsa_sample_cmp(page_table, q, ck_t, cv_t, cw):
    S, TS, _ = q.shape
    n_pages = page_table.shape[1]
    P = n_pages * PAGE_SIZE
    m_cmp = P // CMP_STRIDE
    n_slc_pad = -(-(-(-(P + TS) // SLC_BLOCK)) // LANES) * LANES
    hbm = pl.BlockSpec(memory_space=pl.ANY)
    wspecs = [pl.BlockSpec(w.shape, lambda b, pt, nd=w.ndim: (0,) * nd) for w in cw]
    grid_spec = pltpu.PrefetchScalarGridSpec(
        num_scalar_prefetch=1,
        grid=(S,),
        in_specs=[pl.BlockSpec((None, TS, NSA_WIDTH), lambda b, pt: (b, 0, 0)), hbm, hbm] + wspecs,
        out_specs=[pl.BlockSpec((None, NSA_KV_HEADS, Q_PER_KV * TS, LANES), lambda b, pt: (b, 0, 0, 0)),
                   pl.BlockSpec((None, NSA_KV_HEADS, n_slc_pad // LANES, TS, LANES), lambda b, pt: (b, 0, 0, 0, 0))],
        scratch_shapes=[pltpu.VMEM((CMP_STRIDE, m_cmp, KV_WIDTH), F32), pltpu.VMEM((CMP_STRIDE, m_cmp, KV_WIDTH), F32)]
        + _page_scratch(n_pages, 2),
    )
    return pl.pallas_call(
        functools.partial(_nsa_sample_cmp_body, P=P, TS=TS),
        grid_spec=grid_spec,
        out_shape=[jax.ShapeDtypeStruct((S, NSA_KV_HEADS, Q_PER_KV * TS, LANES), F32),
                   jax.ShapeDtypeStruct((S, NSA_KV_HEADS, n_slc_pad // LANES, TS, LANES), F32)],
        compiler_params=_params(("arbitrary",)),
        name="nsa_sample_cmp",
    )(page_table, q, ck_t, cv_t, *cw)


def _nsa_sample_slc_body(pt_ref, q_ref, sel_ref, ocmp_ref, g_ref, ksn_ref, vsn_ref, kwn_ref, vwn_ref,
                         kwin_ref, vwin_ref, ck_ref, cv_ref, o_ref, kwout_ref, vwout_ref,
                         kt_ref, vt_ref, m_ref, l_ref, acc_ref, kbuf_ref, vbuf_ref, sem_ref, *, P, TS):
    n_pages = P // PAGE_SIZE
    rows = NSA_KV_HEADS * Q_PER_KV * TS
    wb = kwin_ref.shape[1]
    lane = _lane_iota((1, LANES))
    t = _row_iota((rows, 1)) % TS

    fetch = _PageFetch(pt_ref, (ck_ref, cv_ref), (kbuf_ref, vbuf_ref), sem_ref, n_pages)
    fetch.begin()

    def sel_rows(ch):
        return jnp.concatenate([sel_ref[kv, ch] for kv in range(NSA_KV_HEADS) for _ in range(Q_PER_KV)], axis=0)

    def update(s, pv_of):
        m_old = m_ref[...]
        m_new = jnp.maximum(m_old, jnp.max(s, axis=-1, keepdims=True))
        p = jnp.exp(s - m_new[:, 0:1])
        alpha = jnp.exp(m_old - m_new)
        l_ref[...] = alpha * l_ref[...] + jnp.sum(p, axis=-1, keepdims=True)
        acc_ref[...] = alpha * acc_ref[...] + pv_of(p.astype(BF16))
        m_ref[...] = m_new

    def result():
        return jnp.where(m_ref[...] > 0.5 * NEG_INF, acc_ref[...] / jnp.maximum(l_ref[...], 1e-30), 0.0)

    def reset():
        m_ref[...] = jnp.full(m_ref.shape, NEG_INF, F32)
        l_ref[...] = jnp.zeros(l_ref.shape, F32)
        acc_ref[...] = jnp.zeros(acc_ref.shape, F32)

    q = q_ref[...] * SCALE
    q2 = jnp.concatenate([_stack_heads(q, kv, lambda hl, kv=kv: kv) for kv in range(NSA_KV_HEADS)],
                         axis=0).astype(BF16)
    reset()

    for k in range(n_pages):
        fetch.prefetch(k)
        kt_ref[:, k * PAGE_SIZE:(k + 1) * PAGE_SIZE] = fetch.page(0, k).astype(BF16)
        vt_ref[:, k * PAGE_SIZE:(k + 1) * PAGE_SIZE] = fetch.page(1, k).astype(BF16)
    chosen = []
    for i in range(P // LANES):
        b0 = i * (LANES // SLC_BLOCK)
        sel_c = sel_rows(b0 // LANES)
        chosen.append(jnp.where(lane < SLC_BLOCK,
                                jnp.broadcast_to(sel_c[:, b0 % LANES:b0 % LANES + 1], (rows, LANES)),
                                jnp.broadcast_to(sel_c[:, b0 % LANES + 1:b0 % LANES + 2], (rows, LANES))))
    kpos = _lane_iota((1, P))
    bias = jnp.where((jnp.concatenate(chosen, axis=1) > 0.5) & (kpos <= P + t), 0.0, NEG_INF)
    update(_dot(q2, kt_ref[...]) + bias, lambda p: _dot_nt(p, vt_ref[...]))

    pad_rows = lambda x: jnp.concatenate([x, jnp.zeros((LANES - TS, LANES), F32)], axis=0).astype(BF16)
    own = (lane >> 6) == (_row_iota((rows, 1)) // (Q_PER_KV * TS))
    jn = P // SLC_BLOCK
    sel_n = sel_rows(jn // LANES)[:, jn % LANES:jn % LANES + 1]
    bias_n = jnp.where((sel_n > 0.5) & (lane <= t) & (lane < TS), 0.0, NEG_INF)
    vn = pad_rows(vsn_ref[...])
    update(_dot_nt(q2, pad_rows(ksn_ref[...])) + bias_n, lambda p: _dot(p, vn))
    o_slc = result()
    reset()
    kw_t, vw_t = kwin_ref[...], vwin_ref[...]
    iw = _lane_iota((1, wb))
    rel = t + wb - iw
    bias_b = jnp.where((rel >= 0) & (rel < WINDOW), 0.0, NEG_INF)
    vw_tb = vw_t.astype(BF16)
    update(_dot(q2, kw_t.astype(BF16)) + bias_b, lambda p: _dot_nt(p, vw_tb))
    bias_w = jnp.where((lane <= t) & (lane < TS), 0.0, NEG_INF)
    vwn = pad_rows(vwn_ref[...])
    update(_dot_nt(q2, pad_rows(kwn_ref[...])) + bias_w, lambda p: _dot(p, vwn))
    o_win = result()
    gs = jax.nn.sigmoid(g_ref[...])
    gcol = lambda comp: jnp.concatenate(
        [gs[:, 3 * hq + comp:3 * hq + comp + 1] for hq in range(NSA_HEADS)], axis=0)
    ocmp = jnp.concatenate([ocmp_ref[kv] for kv in range(NSA_KV_HEADS)], axis=0)
    o = gcol(0) * ocmp + gcol(1) * o_slc + gcol(2) * o_win
    o = jnp.where(own, o, 0.0)
    for tile in range(NSA_HEADS // 2):
        a = o[(2 * tile) * TS:(2 * tile + 1) * TS]
        b = o[(2 * tile + 1) * TS:(2 * tile + 2) * TS]
        kv = (2 * tile) // Q_PER_KV
        a = a if kv == 0 else pltpu.roll(a, HEAD_DIM, 1)
        b = b if kv == 1 else pltpu.roll(b, HEAD_DIM, 1)
        o_ref[:, tile * LANES:(tile + 1) * LANES] = a + b
    kwout_ref[0:wb - TS, :] = kw_t.T[TS:wb, :]
    kwout_ref[wb - TS:wb, :] = kwn_ref[...]
    vwout_ref[0:wb - TS, :] = vw_t.T[TS:wb, :]
    vwout_ref[wb - TS:wb, :] = vwn_ref[...]
    fetch.end()


def _nsa_sample_slc(page_table, q, sel, ocmp, g, ksn, vsn, kwn, vwn, kwin_t, vwin_t, ck_t, cv_t):
    S, TS, _ = q.shape
    n_pages = page_table.shape[1]
    P = n_pages * PAGE_SIZE
    wb = kwin_t.shape[2]
    rows = NSA_KV_HEADS * Q_PER_KV * TS
    hbm = pl.BlockSpec(memory_space=pl.ANY)
    per_seq = lambda shape: pl.BlockSpec((None,) + shape, lambda b, pt, nd=len(shape): (b,) + (0,) * nd)
    grid_spec = pltpu.PrefetchScalarGridSpec(
        num_scalar_prefetch=1,
        grid=(S,),
        in_specs=[per_seq((TS, NSA_WIDTH)), per_seq(sel.shape[1:]), per_seq(ocmp.shape[1:]), per_seq((TS, LANES)),
                  per_seq((TS, KV_WIDTH)), per_seq((TS, KV_WIDTH)), per_seq((TS, KV_WIDTH)), per_seq((TS, KV_WIDTH)),
                  per_seq((KV_WIDTH, wb)), per_seq((KV_WIDTH, wb)), hbm, hbm],
        out_specs=[per_seq((TS, NSA_WIDTH)), per_seq((wb, KV_WIDTH)), per_seq((wb, KV_WIDTH))],
        scratch_shapes=[pltpu.VMEM((KV_WIDTH, P), BF16), pltpu.VMEM((KV_WIDTH, P), BF16)]
        + [pltpu.VMEM((rows, LANES), F32)] * 3 + _page_scratch(n_pages, 2),
    )
    return pl.pallas_call(
        functools.partial(_nsa_sample_slc_body, P=P, TS=TS),
        grid_spec=grid_spec,
        out_shape=[jax.ShapeDtypeStruct((S, TS, NSA_WIDTH), F32),
                   jax.ShapeDtypeStruct((S, wb, KV_WIDTH), F32), jax.ShapeDtypeStruct((S, wb, KV_WIDTH), F32)],
        compiler_params=_params(("arbitrary",)),
        name="nsa_sample_slc",
    )(page_table, q, sel, ocmp, g, ksn, vsn, kwn, vwn, kwin_t, vwin_t, ck_t, cv_t)


def _merge_body(x_ref, glu_ref, o_ref, nw_ref, wg_ref, wup_ref, wnsa_ref, wout_ref, out_ref):
    x = x_ref[...]
    gate = jax.nn.sigmoid(_dot(_rms(x, nw_ref[...]).astype(BF16), wg_ref[...]))
    s5_out = _dot(glu_ref[...].astype(BF16), wup_ref[...])
    nsa_out = _dot(o_ref[...].astype(BF16), wnsa_ref[...])
    merged = gate[:, :D_MODEL] * s5_out + gate[:, D_MODEL:] * nsa_out
    out_ref[...] = x + _dot(merged.astype(BF16), wout_ref[...])


def _merge(x, glu, o, nw, wg, wup, wnsa, wout, tt):
    B, T, _ = x.shape
    row = lambda w: pl.BlockSpec((None, tt, w), lambda b, i: (b, i, 0))
    return pl.pallas_call(
        _merge_body,
        grid=(B, T // tt),
        in_specs=[row(D_MODEL), row(S5_WIDTH), row(NSA_WIDTH), _const_spec(nw.shape), _const_spec(wg.shape),
                  _const_spec(wup.shape), _const_spec(wnsa.shape), _const_spec(wout.shape)],
        out_specs=row(D_MODEL),
        out_shape=jax.ShapeDtypeStruct((B, T, D_MODEL), F32),
        compiler_params=_params(("parallel", "parallel")),
        name="merge",
    )(x, glu, o, nw, wg, wup, wnsa, wout)


def _ffn_body(x_ref, nw_ref, w1_ref, w2_ref, out_ref):
    x = x_ref[...]
    a = jnp.maximum(_dot(_rms(x, nw_ref[...]).astype(BF16), w1_ref[...]), 0.0)
    out_ref[...] = x + _dot((a * a).astype(BF16), w2_ref[...])


def _ffn(x, nw, w1, w2, tt):
    B, T, _ = x.shape
    row = pl.BlockSpec((None, tt, D_MODEL), lambda b, i: (b, i, 0))
    return pl.pallas_call(
        _ffn_body,
        grid=(B, T // tt),
        in_specs=[row, _const_spec(nw.shape), _const_spec(w1.shape), _const_spec(w2.shape)],
        out_specs=row,
        out_shape=jax.ShapeDtypeStruct((B, T, D_MODEL), F32),
        compiler_params=_params(("parallel", "parallel")),
        name="ffn",
    )(x, nw, w1, w2)


def _rope_tables(pos):
    half = HEAD_DIM // 2
    inv = ROPE_THETA ** (-jnp.arange(half, dtype=F32) / half)
    ang = pos.astype(F32)[:, None] * inv[None, :]
    cos, sin = jnp.cos(ang), jnp.sin(ang)
    return (jnp.concatenate([cos, cos, cos, cos], axis=-1),
            jnp.concatenate([-sin, sin, -sin, sin], axis=-1))


def _block_diag(w):
    G, a, b = w.shape
    return jnp.einsum('gab,gk->gakb', w, jnp.eye(G, dtype=w.dtype)).reshape(G * a, G * b)


def _cmp_weights(pe, w1, w2):
    eye = jnp.eye(NSA_KV_HEADS, dtype=F32)
    half = CMP_LEN // 2
    big = lambda w: jnp.einsum('rdf,hk->rhdkf', w, eye).reshape(half * KV_WIDTH, NSA_KV_HEADS * CMP_HIDDEN)
    w1b = jnp.concatenate([big(w1[:half]), big(w1[half:])], axis=1).astype(BF16)
    w2d = jnp.concatenate([w2, w2], axis=1).astype(BF16)
    flat = lambda p: jnp.broadcast_to(p[:, None, :], (half, NSA_KV_HEADS, HEAD_DIM)).reshape(1, half * KV_WIDTH)
    pe8 = jnp.concatenate([flat(pe[:half]), flat(pe[half:]), jnp.zeros((SUBLANES - 2, half * KV_WIDTH), F32)], axis=0)
    return w1b, w2d, pe8


def kernel(x_prompt, x_sample, cache_k_cmp, cache_v_cmp, cache_k_slc, cache_v_slc, state_k_win, state_v_win, state_s5_re, state_s5_im, page_table, norm_mix_w, w_in, s5_lam_re, s5_lam_im, s5_log_dt, s5_b_re, s5_b_im, s5_c_re, s5_c_im, s5_d, s5_w_glu, s5_w_up, q_norm_w, k_norm_w, cmp_pe_k, cmp_wk1, cmp_wk2, cmp_pe_v, cmp_wv1, cmp_wv2, nsa_w_up, w_out, norm_ffn_w, w_ffn1, w_ffn2):
    B, T, _ = x_prompt.shape
    S, TS, _ = x_sample.shape
    n_pages = page_table.shape[1]
    P = n_pages * PAGE_SIZE
    n_pool = cache_k_cmp.shape[1]
    assert norm_mix_w.shape[0] == 1 and B % SUBLANES == 0 and S % SUBLANES == 0
    assert TS < CMP_STRIDE and P % SLC_BLOCK == 0

    w = w_in[0]
    cols = lambda a, n: w[:, a:a + n]
    w_a = jnp.concatenate(
        [cols(0, 512), cols(512, 512), cols(1024, 128), cols(1280, 128), cols(1536, 128),
         cols(1152, 128), cols(1408, 128), cols(1664, 128), cols(1792, N_GATE),
         jnp.zeros((D_MODEL, W_A_COLS - 1792 - N_GATE), F32)], axis=1).astype(BF16)
    w_g = w[:, OFF_GMIX:].astype(BF16)
    nw_mix = norm_mix_w[0][None, :]
    qkw = jnp.concatenate([jnp.tile(q_norm_w[0], NSA_HEADS)]
                          + [jnp.tile(k_norm_w[0, i], NSA_KV_HEADS) for i in range(3)])[None, :]
    li = np.arange(2 * LANES)
    seg = jnp.asarray((li[:, None] // HEAD_DIM) == (li[None, :] // HEAD_DIM), BF16)
    lre, lim = s5_lam_re[0].reshape(1, S5_LANES), s5_lam_im[0].reshape(1, S5_LANES)
    ldt = jnp.repeat(s5_log_dt[0], S5_STATE)[None, :]
    twice = lambda b: jnp.tile(jnp.swapaxes(b, 1, 2).reshape(S5_WIDTH, S5_STATE), (1, LANES // S5_STATE))
    bre, bim = twice(s5_b_re[0]), twice(s5_b_im[0])
    cre = _block_diag(jnp.swapaxes(s5_c_re[0], 1, 2)).astype(BF16)
    cim = _block_diag(jnp.swapaxes(s5_c_im[0], 1, 2)).astype(BF16)
    s5d = s5_d[0][None, :]
    wglu = s5_w_glu[0].astype(BF16)
    wup = s5_w_up[0].astype(BF16)
    cw = _cmp_weights(cmp_pe_k[0], cmp_wk1[0], cmp_wk2[0]) + _cmp_weights(cmp_pe_v[0], cmp_wv1[0], cmp_wv2[0])
    wnsa = nsa_w_up[0].astype(BF16)
    wout = w_out[0].astype(BF16)
    nw_ffn = norm_ffn_w[0][None, :]
    w1 = w_ffn1[0].astype(BF16)
    w2 = w_ffn2[0].astype(BF16)

    def trunk(x, glu, o, tt):
        return _ffn(_merge(x, glu, o, nw_mix, w_g, wup, wnsa, wout, tt), nw_ffn, w1, w2, tt)

    cos_p, sin_p = _rope_tables(jnp.arange(T, dtype=jnp.int32))
    u, q, kc, ks, kw, vc, vs, vw, g, kc16, vc16 = _inproj(x_prompt, cos_p, sin_p, nw_mix, w_a, qkw, seg, TOKEN_TILE)
    zeros_st = jnp.zeros((B, S5_LANES), F32)
    glu_tm, sre_p, sim_p = _s5(u, zeros_st, zeros_st, lre, lim, ldt, bre, bim, cre, cim, s5d, wglu, S5_CHUNK)
    o_p = _nsa_prompt(q, kc16, vc16, ks, vs, kw, vw, g, cw, NSA_TILE)
    y_prompt = trunk(x_prompt, glu_tm, o_p, TOKEN_TILE)

    n_s = S * TS
    cos_s, sin_s = _rope_tables(P + jnp.arange(TS, dtype=jnp.int32))
    cos_s, sin_s = jnp.tile(cos_s, (S, 1)), jnp.tile(sin_s, (S, 1))
    xs = x_sample.reshape(1, n_s, D_MODEL)
    us, qs, kcs, kss, kws, vcs, vss, vws, gs, _, _ = _inproj(xs, cos_s, sin_s, nw_mix, w_a, qkw, seg, n_s)
    seq = lambda a: a.reshape(S, TS, a.shape[-1])
    glu_s_tm, sre_s, sim_s = _s5(seq(us), state_s5_re[0].reshape(S, S5_LANES), state_s5_im[0].reshape(S, S5_LANES),
                                 lre, lim, ldt, bre, bim, cre, cim, s5d, wglu, TS)
    pool = lambda c: jnp.transpose(c[0], (0, 2, 3, 1)).reshape(n_pool, KV_WIDTH, PAGE_SIZE)
    ocmp_s, sel_s = _nsa_sample_cmp(page_table, seq(qs), pool(cache_k_cmp), pool(cache_v_cmp), cw)
    win = lambda s: jnp.transpose(s[0], (0, 2, 3, 1)).reshape(S, KV_WIDTH, s.shape[2])
    o_s, kwin_s, vwin_s = _nsa_sample_slc(page_table, seq(qs), sel_s, ocmp_s, seq(gs), seq(kss), seq(vss), seq(kws),
                                          seq(vws), win(state_k_win), win(state_v_win),
                                          pool(cache_k_slc), pool(cache_v_slc))
    glu_s = glu_s_tm.reshape(1, n_s, S5_WIDTH)
    y_sample = trunk(xs, glu_s, o_s.reshape(1, n_s, NSA_WIDTH), n_s).reshape(S, TS, D_MODEL)

    heads_p = lambda a: a.reshape(1, B, a.shape[1], NSA_KV_HEADS, HEAD_DIM)
    heads_s = lambda a: a.reshape(1, S, -1, NSA_KV_HEADS, HEAD_DIM)
    keep = min(WINDOW, T)
    st_p = lambda a: a.reshape(1, B, S5_GROUPS, S5_STATE)
    st_s = lambda a: a.reshape(1, S, S5_GROUPS, S5_STATE)
    return (y_prompt, y_sample,
            heads_p(kc), heads_s(kcs), heads_p(vc), heads_s(vcs),
            heads_p(ks), heads_s(kss), heads_p(vs), heads_s(vss),
            heads_p(kw[:, T - keep:]), heads_s(kwin_s), heads_p(vw[:, T - keep:]), heads_s(vwin_s),
            st_p(sre_p), st_s(sre_s), st_p(sim_p), st_s(sim_s))
```
